```python
import jax, jax.numpy as jnp
from jax import lax
import numpy as np

D_MODEL = 1024
BATCH = 32
SEQ = 256
DEPTH = 2
DEC_BATCH = 4
DEC_SEQ = 4096
PAST_LEN = 512

GRID_W = 64
N_EVEN = (DEPTH + 1) // 2
N_ODD = DEPTH // 2
ATT_HEADS = 8
ATT_KV_HEADS = 2
HEAD_DIM = 64
ATT_GROUP = ATT_HEADS // ATT_KV_HEADS
ATT_W = ATT_HEADS * HEAD_DIM
KV_W = ATT_KV_HEADS * HEAD_DIM
ROPE_AXIS_DIM = HEAD_DIM // 2
ROPE_THETA = 10000.0
Q_BLOCK = 128
CONV_CH = D_MODEL // 2
CONV_K = 31
EVEN_IN = ATT_W + 2 * KV_W + 2 * CONV_CH
MIX_W = ATT_W + CONV_CH
M_HEADS = 4
M_INNER = D_MODEL
M_DK = M_INNER // M_HEADS
M_DV = M_INNER // M_HEADS
M_CHUNK = 64
ODD_IN = 4 * M_INNER + 4 * M_HEADS
D_FF = 4 * D_MODEL
ALPHA = (2 * DEPTH) ** 0.25
BETA = (8 * DEPTH) ** -0.25
EPS = 1e-6

kernel_name = "hybrid_diffusion_ctx_prefix_step"


def _layernorm(x, g, b):
    xf = x.astype(jnp.float32)
    mu = jnp.mean(xf, axis=-1, keepdims=True)
    var = jnp.mean(jnp.square(xf - mu), axis=-1, keepdims=True)
    y = (xf - mu) * lax.rsqrt(var + EPS) * g.astype(jnp.float32) + b.astype(jnp.float32)
    return y.astype(x.dtype)


def _rmsnorm(x, g):
    xf = x.astype(jnp.float32)
    y = xf * lax.rsqrt(jnp.mean(xf * xf, axis=-1, keepdims=True) + EPS) * g.astype(jnp.float32)
    return y.astype(x.dtype)


def _adaln(cond, w, b):
    m = jax.nn.silu(cond) @ w + b
    return jnp.split(m[:, None, :], 6, axis=-1)


def _modulate(x, shift, scale):
    return x * (1 + scale) + shift


def _residual_norm(x, y, gate, g, b):
    return _layernorm(ALPHA * x + gate * y, g, b)


def _ffn(h, w1, w2):
    return jnp.square(jax.nn.relu(h @ w1)) @ w2


def _axial_rope(n):
    rows = n // GRID_W
    row = jnp.repeat(jnp.arange(rows), GRID_W).astype(jnp.float32)
    col = (jnp.arange(n) % GRID_W).astype(jnp.float32)
    freqs = ROPE_THETA ** (-jnp.arange(0, ROPE_AXIS_DIM, 2, dtype=jnp.float32) / ROPE_AXIS_DIM)
    ang = jnp.concatenate([row[:, None] * freqs, col[:, None] * freqs], axis=-1)
    return jnp.cos(ang)[:, None, :], jnp.sin(ang)[:, None, :]


def _apply_rope(x, cos, sin):
    xf = x.astype(jnp.float32).reshape(*x.shape[:-1], HEAD_DIM // 2, 2)
    x0, x1 = xf[..., 0], xf[..., 1]
    out = jnp.stack([x0 * cos - x1 * sin, x0 * sin + x1 * cos], axis=-1)
    return out.reshape(x.shape).astype(x.dtype)


def _block_attention(q, k, v):
    b, sq = q.shape[0], q.shape[1]
    nb = sq // Q_BLOCK
    qb = q.reshape(b, nb, Q_BLOCK, ATT_KV_HEADS, ATT_GROUP, HEAD_DIM).transpose(1, 0, 2, 3, 4, 5)
    scale = HEAD_DIM ** -0.5

    def one_block(qi):
        s = jnp.einsum('bqkgd,bskd->bkgqs', qi, k).astype(jnp.float32) * scale
        p = jax.nn.softmax(s, axis=-1).astype(v.dtype)
        return jnp.einsum('bkgqs,bskd->bqkgd', p, v)

    o = lax.map(one_block, qb)
    return o.transpose(1, 0, 2, 3, 4, 5).reshape(b, sq, ATT_W)


def _depthwise_conv(u, w, bias):
    y = lax.conv_general_dilated(
        u, w[:, None, :].astype(u.dtype), window_strides=(1,),
        padding=[(CONV_K // 2, CONV_K // 2)], dimension_numbers=('NWC', 'WIO', 'NWC'),
        feature_group_count=CONV_CH)
    return y + bias


def _attn_conv_mixer(h, w_in, q_gain, k_gain, conv_w, conv_b, cln_g, cln_b, w_out, ctx_k=None, ctx_v=None):
    b, s, _ = h.shape
    proj = h @ w_in
    q, k, v, u = jnp.split(proj, [ATT_W, ATT_W + KV_W, ATT_W + 2 * KV_W], axis=-1)
    q = _rmsnorm(q.reshape(b, s, ATT_HEADS, HEAD_DIM), q_gain)
    k = _rmsnorm(k.reshape(b, s, ATT_KV_HEADS, HEAD_DIM), k_gain)
    v = v.reshape(b, s, ATT_KV_HEADS, HEAD_DIM)
    if ctx_k is None:
        att = _block_attention(q, k, v)
    else:
        cos, sin = _axial_rope(s)
        q_r = _apply_rope(q, cos, sin)
        k_r = _apply_rope(k, cos, sin)
        k_all = jnp.concatenate([k_r, ctx_k.astype(k.dtype)], axis=1)
        v_all = jnp.concatenate([v, ctx_v.astype(v.dtype)], axis=1)
        att = _block_attention(q_r, k_all, v_all)
    a, gt = jnp.split(u, 2, axis=-1)
    u = a * jax.nn.sigmoid(gt)
    u = _depthwise_conv(u, conv_w, conv_b)
    u = jax.nn.silu(_layernorm(u, cln_g, cln_b))
    out = jnp.concatenate([att, u], axis=-1) @ w_out
    return out, k, v


def _to_chunks(t):
    b, s = t.shape[0], t.shape[1]
    t = t.reshape(b, s // M_CHUNK, M_CHUNK, *t.shape[2:])
    return jnp.moveaxis(jnp.moveaxis(t, 1, 0), 3, 2)


def _mlstm_scan(q, k, v, li, lf, c0, n0, m0):
    b, s = q.shape[0], q.shape[1]
    mask = jnp.tril(jnp.ones((M_CHUNK, M_CHUNK), dtype=bool))

    def step(carry, inp):
        c, n, m = carry
        qc, kc, vc, lic, lfc = inp
        cum = jnp.cumsum(lfc, axis=-1)
        d = cum[..., :, None] - cum[..., None, :] + lic[..., None, :]
        d = jnp.where(mask, d, -jnp.inf)
        a_inter = cum + m[..., None]
        m_t = jnp.maximum(a_inter, jnp.max(d, axis=-1))
        w = jnp.exp(d - m_t[..., None])
        s_inter = jnp.exp(a_inter - m_t)
        qk = jnp.einsum('bhtd,bhsd->bhts', qc, kc) * w
        num = jnp.einsum('bhts,bhsv->bhtv', qk, vc) + s_inter[..., None] * jnp.einsum('bhtd,bhdv->bhtv', qc, c)
        den = jnp.sum(qk, axis=-1) + s_inter * jnp.einsum('bhtd,bhd->bht', qc, n)
        h = num / jnp.maximum(jnp.abs(den), jnp.exp(-m_t))[..., None]
        tot = cum[..., -1]
        g = tot[..., None] - cum + lic
        m_new = jnp.maximum(tot + m, jnp.max(g, axis=-1))
        ws = jnp.exp(g - m_new[..., None])
        decay = jnp.exp(tot + m - m_new)
        kw = kc * ws[..., None]
        c_new = decay[..., None, None] * c + jnp.einsum('bhsd,bhsv->bhdv', kw, vc)
        n_new = decay[..., None] * n + jnp.sum(kw, axis=2)
        return (c_new, n_new, m_new), h

    xs = (_to_chunks(q), _to_chunks(k), _to_chunks(v), _to_chunks(li), _to_chunks(lf))
    (c_t, n_t, m_t), h = lax.scan(step, (c0, n0, m0), xs)
    h = h.transpose(1, 0, 3, 2, 4).reshape(b, s, M_HEADS, M_DV)
    return h, c_t, n_t, m_t


def _mlstm_mixer(h, w_in, b_gate, mh_gain, w_out, st_c, st_n, st_m):
    b, s, _ = h.shape
    proj = h @ w_in
    q, k, v, o, gates = jnp.split(proj, [M_INNER, 2 * M_INNER, 3 * M_INNER, 4 * M_INNER], axis=-1)
    q = q.reshape(b, s, M_HEADS, M_DK).astype(jnp.float32)
    k = k.reshape(b, s, M_HEADS, M_DK).astype(jnp.float32) * (M_DK ** -0.5)
    v = v.reshape(b, s, M_HEADS, M_DV).astype(jnp.float32)
    gates = (gates + b_gate).astype(jnp.float32).reshape(b, s, 4, M_HEADS)
    li_f, lf_f = gates[:, :, 0], jax.nn.log_sigmoid(gates[:, :, 1])
    li_b, lf_b = gates[:, :, 2], jax.nn.log_sigmoid(gates[:, :, 3])
    sc = st_c.astype(jnp.float32)
    sn = st_n.astype(jnp.float32)
    sm = st_m.astype(jnp.float32)
    h_f, cf, nf, mf = _mlstm_scan(q, k, v, li_f, lf_f, sc[:, 0], sn[:, 0], sm[:, 0])
    fl = lambda t: jnp.flip(t, axis=1)
    h_b, cb, nb, mb = _mlstm_scan(fl(q), fl(k), fl(v), fl(li_b), fl(lf_b), sc[:, 1], sn[:, 1], sm[:, 1])
    h_sum = h_f + fl(h_b)
    hn = _rmsnorm(h_sum, mh_gain.reshape(M_HEADS, M_DV)).reshape(b, s, M_INNER)
    out = (jax.nn.sigmoid(o.astype(jnp.float32)) * hn).astype(h.dtype) @ w_out
    return out, jnp.stack([cf, cb], axis=1), jnp.stack([nf, nb], axis=1), jnp.stack([mf, mb], axis=1)


def setup_inputs(seed: int = 0) -> dict:
    key = jax.random.key(seed)
    ks = jax.random.split(key, 28)
    f32 = jnp.float32

    def nrm(k, shape, s=1.0):
        return jax.random.normal(k, shape, f32) * s

    lin = jnp.linspace(3.0, 6.0, M_HEADS, dtype=f32)
    zh = jnp.zeros((M_HEADS,), f32)
    gate_base = jnp.concatenate([zh, lin, zh, lin])
    return {
        'x_prompt': nrm(ks[0], (BATCH, SEQ, D_MODEL)),
        'x_sample': nrm(ks[1], (DEC_BATCH, DEC_SEQ, D_MODEL)),
        'cache_k': nrm(ks[2], (DEC_BATCH, N_EVEN, PAST_LEN, ATT_KV_HEADS, HEAD_DIM)),
        'cache_v': nrm(ks[3], (DEC_BATCH, N_EVEN, PAST_LEN, ATT_KV_HEADS, HEAD_DIM)),
        'state_c': nrm(ks[4], (DEC_BATCH, N_ODD, 2, M_HEADS, M_DK, M_DV), 0.3),
        'state_n': nrm(ks[5], (DEC_BATCH, N_ODD, 2, M_HEADS, M_DK), 0.3),
        'state_m': nrm(ks[6], (DEC_BATCH, N_ODD, 2, M_HEADS), 0.5),
        'c': nrm(ks[7], (DEC_BATCH, D_MODEL)),
        'c_ctx': nrm(ks[8], (D_MODEL,)),
        'w_ada': nrm(ks[9], (DEPTH, D_MODEL, 6 * D_MODEL), D_MODEL ** -0.5),
        'b_ada': nrm(ks[10], (DEPTH, 6 * D_MODEL), 0.02),
        'ln_g': 1.0 + nrm(ks[11], (DEPTH, 2, D_MODEL), 0.02),
        'ln_b': nrm(ks[12], (DEPTH, 2, D_MODEL), 0.02),
        'w_ff1': nrm(ks[13], (DEPTH, D_MODEL, D_FF), D_MODEL ** -0.5),
        'w_ff2': nrm(ks[14], (DEPTH, D_FF, D_MODEL), BETA * D_FF ** -0.5),
        'w_in_a': nrm(ks[15], (N_EVEN, D_MODEL, EVEN_IN), D_MODEL ** -0.5),
        'q_gain': 1.0 + nrm(ks[16], (N_EVEN, HEAD_DIM), 0.02),
        'k_gain': 1.0 + nrm(ks[17], (N_EVEN, HEAD_DIM), 0.02),
        'conv_w': nrm(ks[18], (N_EVEN, CONV_K, CONV_CH), CONV_K ** -0.5),
        'conv_b': nrm(ks[19], (N_EVEN, CONV_CH), 0.02),
        'conv_ln_g': 1.0 + nrm(ks[20], (N_EVEN, CONV_CH), 0.02),
        'conv_ln_b': nrm(ks[21], (N_EVEN, CONV_CH), 0.02),
        'w_out_a': nrm(ks[22], (N_EVEN, MIX_W, D_MODEL), BETA * MIX_W ** -0.5),
        'w_in_m': nrm(ks[23], (N_ODD, D_MODEL, ODD_IN), D_MODEL ** -0.5),
        'b_gate_m': gate_base + nrm(ks[24], (N_ODD, 4 * M_HEADS), 0.1),
        'mh_gain': 1.0 + nrm(ks[25], (N_ODD, M_INNER), 0.02),
        'w_out_m': nrm(ks[26], (N_ODD, M_INNER, D_MODEL), BETA * M_INNER ** -0.5),
    }


def reference(x_prompt, x_sample, cache_k, cache_v, state_c, state_n, state_m, c, c_ctx,
              w_ada, b_ada, ln_g, ln_b, w_ff1, w_ff2,
              w_in_a, q_gain, k_gain, conv_w, conv_b, conv_ln_g, conv_ln_b, w_out_a,
              w_in_m, b_gate_m, mh_gain, w_out_m):
    xp, xs = x_prompt, x_sample
    new_k, new_v, new_c, new_n, new_m = [], [], [], [], []
    for l in range(DEPTH):
        mod_p = _adaln(c_ctx[None, :], w_ada[l], b_ada[l])
        mod_s = _adaln(c, w_ada[l], b_ada[l])
        hp = _modulate(xp, mod_p[0], mod_p[1])
        hs = _modulate(xs, mod_s[0], mod_s[1])
        if l % 2 == 0:
            e = l // 2
            wa = (w_in_a[e], q_gain[e], k_gain[e], conv_w[e], conv_b[e], conv_ln_g[e], conv_ln_b[e], w_out_a[e])
            yp, kp, vp = _attn_conv_mixer(hp, *wa)
            ys, _, _ = _attn_conv_mixer(hs, *wa, ctx_k=cache_k[:, e], ctx_v=cache_v[:, e])
            new_k.append(kp)
            new_v.append(vp)
        else:
            o = l // 2
            wm = (w_in_m[o], b_gate_m[o], mh_gain[o], w_out_m[o])
            bp = hp.shape[0]
            z_c = jnp.zeros((bp, 2, M_HEADS, M_DK, M_DV), jnp.float32)
            z_n = jnp.zeros((bp, 2, M_HEADS, M_DK), jnp.float32)
            z_m = jnp.zeros((bp, 2, M_HEADS), jnp.float32)
            yp, sc_p, sn_p, sm_p = _mlstm_mixer(hp, *wm, z_c, z_n, z_m)
            ys, _, _, _ = _mlstm_mixer(hs, *wm, state_c[:, o], state_n[:, o], state_m[:, o])
            new_c.append(sc_p)
            new_n.append(sn_p)
            new_m.append(sm_p)
        xp = _residual_norm(xp, yp, mod_p[2], ln_g[l, 0], ln_b[l, 0])
        xs = _residual_norm(xs, ys, mod_s[2], ln_g[l, 0], ln_b[l, 0])
        xp = _residual_norm(xp, _ffn(_modulate(xp, mod_p[3], mod_p[4]), w_ff1[l], w_ff2[l]), mod_p[5], ln_g[l, 1], ln_b[l, 1])
        xs = _residual_norm(xs, _ffn(_modulate(xs, mod_s[3], mod_s[4]), w_ff1[l], w_ff2[l]), mod_s[5], ln_g[l, 1], ln_b[l, 1])
    dt = x_prompt.dtype
    new_cache_k = jnp.stack(new_k, axis=1).astype(dt)
    new_cache_v = jnp.stack(new_v, axis=1).astype(dt)
    new_state_c = jnp.stack(new_c, axis=1).astype(dt)
    new_state_n = jnp.stack(new_n, axis=1).astype(dt)
    new_state_m = jnp.stack(new_m, axis=1).astype(dt)
    return (xp, xs, new_cache_k, new_cache_v, new_state_c, new_state_n, new_state_m)
```

```python
import functools

import jax
import jax.numpy as jnp
from jax import lax
from jax.experimental import pallas as pl
from jax.experimental.pallas import tpu as pltpu

F32 = jnp.float32
BF16 = jnp.bfloat16

D_MODEL = 1024
BATCH = 32
SEQ = 256
DEPTH = 2
DEC_BATCH = 4
DEC_SEQ = 4096
PAST_LEN = 512
GRID_W = 64
ATT_HEADS = 8
ATT_KV_HEADS = 2
HEAD_DIM = 64
ATT_GROUP = ATT_HEADS // ATT_KV_HEADS
ATT_W = ATT_HEADS * HEAD_DIM
KV_W = ATT_KV_HEADS * HEAD_DIM
ROPE_AXIS_DIM = HEAD_DIM // 2
ROPE_THETA = 10000.0
CONV_CH = D_MODEL // 2
CONV_K = 31
EVEN_IN = ATT_W + 2 * KV_W + 2 * CONV_CH
M_HEADS = 4
M_INNER = D_MODEL
M_DK = M_INNER // M_HEADS
M_DV = M_INNER // M_HEADS
D_FF = 4 * D_MODEL
ALPHA = (2 * DEPTH) ** 0.25
EPS = 1e-6

NP = BATCH * SEQ
NS = DEC_BATCH * DEC_SEQ
NT = NP + NS
N_COND = 8

LANES = 128
VMEM_LIMIT = 56 * 1024 * 1024

TM_PROJ = 512
TM_FFN = 1024
TF_FFN = 1024
TQ_ATT = 256
TK_ATT = 512
R_CONV = 256
HALO = 16
L_CHUNK = 256
GATE_STRIDE = 8


def _cond_index(i_global, tm):
    npt = NP // tm
    tps = DEC_SEQ // tm
    return jnp.where(i_global < npt, 0, 1 + (i_global - npt) // tps)


def _layernorm(r, g, b):
    mu = jnp.mean(r, axis=-1, keepdims=True)
    d = r - mu
    var = jnp.mean(d * d, axis=-1, keepdims=True)
    return d * lax.rsqrt(var + EPS) * g + b


def _sigmoid(x):
    return 1.0 / (1.0 + jnp.exp(-x))


def _params(sem, vmem=VMEM_LIMIT):
    return pltpu.CompilerParams(dimension_semantics=sem, vmem_limit_bytes=vmem)


def _adaln_kernel(cond_ref, w_ref, b_ref, o_ref):
    c = cond_ref[...]
    s = (c * _sigmoid(c)).astype(BF16)
    o_ref[...] = jnp.dot(s, w_ref[...].astype(BF16), preferred_element_type=F32) + b_ref[...]


def _adaln(cond, w_ada, b_ada):
    tn = 1536
    n = 6 * D_MODEL
    out = pl.pallas_call(
        _adaln_kernel,
        out_shape=jax.ShapeDtypeStruct((DEPTH, N_COND, n), F32),
        grid=(DEPTH, n // tn),
        in_specs=[
            pl.BlockSpec((N_COND, D_MODEL), lambda l, j: (0, 0)),
            pl.BlockSpec((None, D_MODEL, tn), lambda l, j: (l, 0, j)),
            pl.BlockSpec((None, 1, tn), lambda l, j: (l, 0, j)),
        ],
        out_specs=pl.BlockSpec((None, N_COND, tn), lambda l, j: (l, 0, j)),
        compiler_params=_params(("arbitrary", "arbitrary")),
        name="adaln",
    )(cond, w_ada, b_ada.reshape(DEPTH, 1, n))
    return out.reshape(DEPTH, N_COND, 6, D_MODEL)


def _inproj_a_kernel(xp_ref, xs_ref, mod_ref, w_ref, qg_ref, kg_ref, cos_ref, sin_ref,
                     q_ref, k_ref, vt_ref, kf_ref, vf_ref, u_ref, *, npt):
    i = pl.program_id(0)
    x = jnp.where(i < npt, xp_ref[...], xs_ref[...])
    mod = mod_ref[...]
    h = (x * (1.0 + mod[1:2]) + mod[0:1]).astype(BF16)
    proj = jnp.dot(h, w_ref[...], preferred_element_type=F32)
    tm = proj.shape[0]

    ri = lax.broadcasted_iota(jnp.int32, (LANES, LANES), 0) // HEAD_DIM
    ci = lax.broadcasted_iota(jnp.int32, (LANES, LANES), 1) // HEAD_DIM
    seg = jnp.where(ri == ci, 1.0, 0.0).astype(BF16)
    lane = lax.broadcasted_iota(jnp.int32, (tm, LANES), 1)
    even = (lane % 2) == 0
    cos = cos_ref[...]
    sin = sin_ref[...]

    def norm(xc, gain):
        ss = jnp.dot((xc * xc).astype(BF16), seg, preferred_element_type=F32)
        return xc * lax.rsqrt(ss * (1.0 / HEAD_DIM) + EPS) * gain

    def rope(xn):
        partner = jnp.where(even, pltpu.roll(xn, LANES - 1, 1), pltpu.roll(xn, 1, 1))
        return xn * cos + partner * sin

    qg = qg_ref[...]
    for c in range(ATT_W // LANES):
        qr = rope(norm(proj[:, c * LANES:(c + 1) * LANES], qg))
        qs = (qr * (HEAD_DIM ** -0.5)).astype(BF16)
        q_ref[2 * c] = qs[:, :HEAD_DIM]
        q_ref[2 * c + 1] = qs[:, HEAD_DIM:]

    kn = norm(proj[:, ATT_W:ATT_W + KV_W], kg_ref[...])
    kf_ref[...] = kn
    kr = rope(kn).astype(BF16)
    k_ref[0] = kr[:, :HEAD_DIM]
    k_ref[1] = kr[:, HEAD_DIM:]

    v = proj[:, ATT_W + KV_W:ATT_W + 2 * KV_W]
    vf_ref[...] = v
    vt_ref[...] = v.T.astype(BF16)

    off = ATT_W + 2 * KV_W
    a = proj[:, off:off + CONV_CH]
    gt = proj[:, off + CONV_CH:off + 2 * CONV_CH]
    u_ref[...] = a * _sigmoid(gt)


def _rope_tables(tm):
    t = jnp.arange(DEC_SEQ)
    row = (t // GRID_W).astype(F32)
    col = (t % GRID_W).astype(F32)
    freqs = ROPE_THETA ** (-jnp.arange(0, ROPE_AXIS_DIM, 2, dtype=F32) / ROPE_AXIS_DIM)
    ang = jnp.concatenate([row[:, None] * freqs, col[:, None] * freqs], axis=-1)
    pair = (jnp.arange(LANES) % HEAD_DIM) // 2
    sign = jnp.where(jnp.arange(LANES) % 2 == 0, -1.0, 1.0).astype(F32)
    cos = jnp.cos(ang)[:, pair]
    sin = jnp.sin(ang)[:, pair] * sign
    cos = jnp.concatenate([jnp.ones((tm, LANES), F32), cos], axis=0)
    sin = jnp.concatenate([jnp.zeros((tm, LANES), F32), sin], axis=0)
    return cos, sin


def _inproj_a(xp, xs, mod, w, q_gain, k_gain):
    tm = TM_PROJ
    npt = NP // tm
    tps = DEC_SEQ // tm
    nt = NT // tm
    cos, sin = _rope_tables(tm)
    qg = jnp.tile(q_gain, LANES // HEAD_DIM).reshape(1, LANES)
    kg = jnp.tile(k_gain, LANES // HEAD_DIM).reshape(1, LANES)

    def rope_idx(i):
        return (jnp.where(i < npt, 0, 1 + (i - npt) % tps), 0)

    return pl.pallas_call(
        functools.partial(_inproj_a_kernel, npt=npt),
        out_shape=[
            jax.ShapeDtypeStruct((ATT_HEADS, NT, HEAD_DIM), BF16),
            jax.ShapeDtypeStruct((ATT_KV_HEADS, NT, HEAD_DIM), BF16),
            jax.ShapeDtypeStruct((nt, KV_W, tm), BF16),
            jax.ShapeDtypeStruct((NT, KV_W), F32),
            jax.ShapeDtypeStruct((NT, KV_W), F32),
            jax.ShapeDtypeStruct((NT, CONV_CH), F32),
        ],
        grid=(nt,),
        in_specs=[
            pl.BlockSpec((tm, D_MODEL), lambda i: (jnp.minimum(i, npt - 1), 0)),
            pl.BlockSpec((tm, D_MODEL), lambda i: (jnp.maximum(i - npt, 0), 0)),
            pl.BlockSpec((None, 6, D_MODEL), lambda i: (_cond_index(i, tm), 0, 0)),
            pl.BlockSpec((D_MODEL, EVEN_IN), lambda i: (0, 0)),
            pl.BlockSpec((1, LANES), lambda i: (0, 0)),
            pl.BlockSpec((1, LANES), lambda i: (0, 0)),
            pl.BlockSpec((tm, LANES), rope_idx),
            pl.BlockSpec((tm, LANES), rope_idx),
        ],
        out_specs=[
            pl.BlockSpec((ATT_HEADS, tm, HEAD_DIM), lambda i: (0, i, 0)),
            pl.BlockSpec((ATT_KV_HEADS, tm, HEAD_DIM), lambda i: (0, i, 0)),
            pl.BlockSpec((None, KV_W, tm), lambda i: (i, 0, 0)),
            pl.BlockSpec((tm, KV_W), lambda i: (i, 0)),
            pl.BlockSpec((tm, KV_W), lambda i: (i, 0)),
            pl.BlockSpec((tm, CONV_CH), lambda i: (i, 0)),
        ],
        compiler_params=_params(("arbitrary",)),
        name="inproj_a",
    )(xp, xs, mod, w, qg, kg, cos, sin)


def _attn_step(qs, kc, vtc, m, l, acc):
    s = lax.dot_general(kc, qs, (((1,), (1,)), ((), ())), preferred_element_type=F32)
    m_new = jnp.maximum(m, jnp.max(s, axis=0, keepdims=True))
    alpha = jnp.exp(m - m_new)
    p = jnp.exp(s - m_new)
    l = alpha * l + jnp.sum(p, axis=0, keepdims=True)
    acc = alpha * acc + jnp.dot(vtc, p.astype(BF16), preferred_element_type=F32)
    return m_new, l, acc


def _attn_finish(l, acc, o_ref, tq):
    o = acc * (1.0 / l)
    ot = jnp.concatenate([o[:, h * tq:(h + 1) * tq] for h in range(ATT_GROUP)], axis=0)
    o_ref[...] = ot.T.astype(BF16)


def _attn_init(nq):
    return (jnp.full((1, nq), -1e30, F32), jnp.zeros((1, nq), F32), jnp.zeros((HEAD_DIM, nq), F32))


def _attn_prompt_kernel(q_ref, k_ref, vt_ref, o_ref):
    tq = q_ref.shape[1]
    qs = q_ref[...].reshape(ATT_GROUP * tq, HEAD_DIM)
    m, l, acc = _attn_init(ATT_GROUP * tq)
    m, l, acc = _attn_step(qs, k_ref[...], vt_ref[...], m, l, acc)
    _attn_finish(l, acc, o_ref, tq)


def _attn_sample_kernel(q_ref, k_ref, vt_ref, ck_ref, cvt_ref, o_ref, *, nchunks):
    tq = q_ref.shape[1]
    qs = q_ref[...].reshape(ATT_GROUP * tq, HEAD_DIM)

    def body(j, carry):
        r0 = pl.multiple_of(j * TK_ATT, TK_ATT)
        return _attn_step(qs, k_ref[pl.ds(r0, TK_ATT), :], vt_ref[j], *carry)

    m, l, acc = lax.fori_loop(0, nchunks, body, _attn_init(ATT_GROUP * tq))
    m, l, acc = _attn_step(qs, ck_ref[...], cvt_ref[...], m, l, acc)
    _attn_finish(l, acc, o_ref, tq)


def _attention_prompt(q, k, vt):
    return pl.pallas_call(
        _attn_prompt_kernel,
        out_shape=jax.ShapeDtypeStruct((NP, ATT_W), BF16),
        grid=(BATCH, ATT_KV_HEADS),
        in_specs=[
            pl.BlockSpec((ATT_GROUP, SEQ, HEAD_DIM), lambda b, g: (g, b, 0)),
            pl.BlockSpec((None, SEQ, HEAD_DIM), lambda b, g: (g, b, 0)),
            pl.BlockSpec((None, HEAD_DIM, SEQ),
                         lambda b, g: (b // (TM_PROJ // SEQ), g, b % (TM_PROJ // SEQ))),
        ],
        out_specs=pl.BlockSpec((SEQ, ATT_GROUP * HEAD_DIM), lambda b, g: (b, g)),
        compiler_params=_params(("arbitrary", "arbitrary")),
        name="attn_prompt",
    )(q, k, vt)


def _attention_sample(q, k, vt, ctx_k, ctx_vt):
    tq = TQ_ATT
    nchunks = DEC_SEQ // TK_ATT
    q_off = NP // tq
    kv_off = NP // DEC_SEQ
    return pl.pallas_call(
        functools.partial(_attn_sample_kernel, nchunks=nchunks),
        out_shape=jax.ShapeDtypeStruct((NS, ATT_W), BF16),
        grid=(DEC_BATCH, ATT_KV_HEADS, DEC_SEQ // tq),
        in_specs=[
            pl.BlockSpec((ATT_GROUP, tq, HEAD_DIM),
                         lambda b, g, i: (g, q_off + b * (DEC_SEQ // tq) + i, 0)),
            pl.BlockSpec((None, DEC_SEQ, HEAD_DIM), lambda b, g, i: (g, kv_off + b, 0)),
            pl.BlockSpec((nchunks, HEAD_DIM, TK_ATT), lambda b, g, i: (kv_off + b, g, 0)),
            pl.BlockSpec((None, None, PAST_LEN, HEAD_DIM), lambda b, g, i: (b, g, 0, 0)),
            pl.BlockSpec((None, None, HEAD_DIM, PAST_LEN), lambda b, g, i: (b, g, 0, 0)),
        ],
        out_specs=pl.BlockSpec((tq, ATT_GROUP * HEAD_DIM),
                               lambda b, g, i: (b * (DEC_SEQ // tq) + i, g)),
        compiler_params=_params(("arbitrary", "arbitrary", "arbitrary")),
        name="attn_sample",
    )(q, k, vt, ctx_k, ctx_vt)


def _conv_kernel(prev_ref, cur_ref, next_ref, w_ref, b_ref, g_ref, bb_ref, o_ref, win_ref, y_ref,
                 *, npt, tps):
    r = cur_ref.shape[0]
    i = pl.program_id(0)
    j = (i - npt) % tps
    first = jnp.logical_or(i < npt, j == 0)
    last = jnp.logical_or(i < npt, j == tps - 1)
    win_ref[0:HALO, :] = jnp.where(first, 0.0, prev_ref[...])
    win_ref[HALO:HALO + r, :] = cur_ref[...]
    win_ref[HALO + r:2 * HALO + r, :] = jnp.where(last, 0.0, next_ref[...])
    base = HALO - CONV_K // 2
    for c in range(CONV_CH // LANES):
        cs = slice(c * LANES, (c + 1) * LANES)
        acc = jnp.zeros((r, LANES), F32)
        for k in range(CONV_K):
            acc = acc + win_ref[base + k:base + k + r, cs] * w_ref[k:k + 1, cs]
        y_ref[:, cs] = acc + b_ref[:, cs]
    y = _layernorm(y_ref[...], g_ref[...], bb_ref[...])
    o_ref[...] = (y * _sigmoid(y)).astype(BF16)


def _conv_module(u, conv_w, conv_b, cln_g, cln_b):
    r = R_CONV
    npt = NP // r
    tps = DEC_SEQ // r
    hb = r // HALO
    nh = NT // HALO
    w = jnp.concatenate([conv_w, jnp.zeros((1, CONV_CH), F32)], axis=0)
    return pl.pallas_call(
        functools.partial(_conv_kernel, npt=npt, tps=tps),
        out_shape=jax.ShapeDtypeStruct((NT, CONV_CH), BF16),
        grid=(NT // r,),
        in_specs=[
            pl.BlockSpec((HALO, CONV_CH), lambda i: (jnp.maximum(i * hb - 1, 0), 0)),
            pl.BlockSpec((r, CONV_CH), lambda i: (i, 0)),
            pl.BlockSpec((HALO, CONV_CH), lambda i: (jnp.minimum((i + 1) * hb, nh - 1), 0)),
            pl.BlockSpec((CONV_K + 1, CONV_CH), lambda i: (0, 0)),
            pl.BlockSpec((1, CONV_CH), lambda i: (0, 0)),
            pl.BlockSpec((1, CONV_CH), lambda i: (0, 0)),
            pl.BlockSpec((1, CONV_CH), lambda i: (0, 0)),
        ],
        out_specs=pl.BlockSpec((r, CONV_CH), lambda i: (i, 0)),
        scratch_shapes=[pltpu.VMEM((r + 2 * HALO, CONV_CH), F32), pltpu.VMEM((r, CONV_CH), F32)],
        compiler_params=_params(("arbitrary",)),
        name="conv_module",
    )(u, u, u, w, conv_b.reshape(1, CONV_CH), cln_g.reshape(1, CONV_CH), cln_b.reshape(1, CONV_CH))


def _outproj_a_kernel(ap_ref, as_ref, uc_ref, xp_ref, xs_ref, mod_ref, w_ref, g_ref, b_ref, o_ref,
                      *, npt):
    i = pl.program_id(0)
    att = jnp.where(i < npt, ap_ref[...], as_ref[...])
    x = jnp.where(i < npt, xp_ref[...], xs_ref[...])
    y = jnp.dot(att, w_ref[0:ATT_W, :], preferred_element_type=F32)
    y = y + jnp.dot(uc_ref[...], w_ref[ATT_W:ATT_W + CONV_CH, :], preferred_element_type=F32)
    mod = mod_ref[...]
    o_ref[...] = _layernorm(ALPHA * x + mod[2:3] * y, g_ref[...], b_ref[...])


def _outproj_a(att_p, att_s, uc, xp, xs, mod, w, g, b):
    tm = TM_PROJ
    npt = NP // tm
    first = lambda i: (jnp.minimum(i, npt - 1), 0)
    second = lambda i: (jnp.maximum(i - npt, 0), 0)
    return pl.pallas_call(
        functools.partial(_outproj_a_kernel, npt=npt),
        out_shape=jax.ShapeDtypeStruct((NT, D_MODEL), F32),
        grid=(NT // tm,),
        in_specs=[
            pl.BlockSpec((tm, ATT_W), first),
            pl.BlockSpec((tm, ATT_W), second),
            pl.BlockSpec((tm, CONV_CH), lambda i: (i, 0)),
            pl.BlockSpec((tm, D_MODEL), first),
            pl.BlockSpec((tm, D_MODEL), second),
            pl.BlockSpec((None, 6, D_MODEL), lambda i: (_cond_index(i, tm), 0, 0)),
            pl.BlockSpec((ATT_W + CONV_CH, D_MODEL), lambda i: (0, 0)),
            pl.BlockSpec((1, D_MODEL), lambda i: (0, 0)),
            pl.BlockSpec((1, D_MODEL), lambda i: (0, 0)),
        ],
        out_specs=pl.BlockSpec((tm, D_MODEL), lambda i: (i, 0)),
        compiler_params=_params(("arbitrary",)),
        name="outproj_a",
    )(att_p, att_s, uc, xp, xs, mod, w, g, b)


def _outproj_m_kernel(hp_ref, hs_ref, x_ref, mod_ref, w_ref, g_ref, b_ref, o_ref, *, npt):
    i = pl.program_id(0)
    hg = jnp.where(i < npt, hp_ref[...], hs_ref[...])
    y = jnp.dot(hg, w_ref[...], preferred_element_type=F32)
    mod = mod_ref[...]
    o_ref[...] = _layernorm(ALPHA * x_ref[...] + mod[2:3] * y, g_ref[...], b_ref[...])


def _outproj_m(hg_p, hg_s, x, mod, w, g, b):
    tm = TM_PROJ
    npt = NP // tm
    return pl.pallas_call(
        functools.partial(_outproj_m_kernel, npt=npt),
        out_shape=jax.ShapeDtypeStruct((NT, D_MODEL), F32),
        grid=(NT // tm,),
        in_specs=[
            pl.BlockSpec((tm, M_INNER), lambda i: (jnp.minimum(i, npt - 1), 0)),
            pl.BlockSpec((tm, M_INNER), lambda i: (jnp.maximum(i - npt, 0), 0)),
            pl.BlockSpec((tm, D_MODEL), lambda i: (i, 0)),
            pl.BlockSpec((None, 6, D_MODEL), lambda i: (_cond_index(i, tm), 0, 0)),
            pl.BlockSpec((M_INNER, D_MODEL), lambda i: (0, 0)),
            pl.BlockSpec((1, D_MODEL), lambda i: (0, 0)),
            pl.BlockSpec((1, D_MODEL), lambda i: (0, 0)),
        ],
        out_specs=pl.BlockSpec((tm, D_MODEL), lambda i: (i, 0)),
        compiler_params=_params(("arbitrary",)),
        name="outproj_m",
    )(hg_p, hg_s, x, mod, w, g, b)


def _ffn_kernel(x_ref, mod_ref, w1_ref, w2_ref, g_ref, b_ref, o_ref, h_scr, acc_scr):
    j = pl.program_id(1)

    @pl.when(j == 0)
    def _():
        mod = mod_ref[...]
        h_scr[...] = (x_ref[...] * (1.0 + mod[4:5]) + mod[3:4]).astype(BF16)
        acc_scr[...] = jnp.zeros_like(acc_scr)

    a = jnp.dot(h_scr[...], w1_ref[...], preferred_element_type=F32)
    a = jnp.maximum(a, 0.0)
    acc_scr[...] += jnp.dot((a * a).astype(BF16), w2_ref[...], preferred_element_type=F32)

    @pl.when(j == pl.num_programs(1) - 1)
    def _():
        mod = mod_ref[...]
        o_ref[...] = _layernorm(ALPHA * x_ref[...] + mod[5:6] * acc_scr[...], g_ref[...], b_ref[...])


def _ffn(x, row0, nrows, mod, w1, w2, g, b):
    tm, tf = TM_FFN, TF_FFN
    off = row0 // tm
    return pl.pallas_call(
        _ffn_kernel,
        out_shape=jax.ShapeDtypeStruct((nrows, D_MODEL), F32),
        grid=(nrows // tm, D_FF // tf),
        in_specs=[
            pl.BlockSpec((tm, D_MODEL), lambda i, j: (i + off, 0)),
            pl.BlockSpec((None, 6, D_MODEL), lambda i, j: (_cond_index(i + off, tm), 0, 0)),
            pl.BlockSpec((D_MODEL, tf), lambda i, j: (0, j)),
            pl.BlockSpec((tf, D_MODEL), lambda i, j: (j, 0)),
            pl.BlockSpec((1, D_MODEL), lambda i, j: (0, 0)),
            pl.BlockSpec((1, D_MODEL), lambda i, j: (0, 0)),
        ],
        out_specs=pl.BlockSpec((tm, D_MODEL), lambda i, j: (i, 0)),
        scratch_shapes=[pltpu.VMEM((tm, D_MODEL), BF16), pltpu.VMEM((tm, D_MODEL), F32)],
        compiler_params=_params(("arbitrary", "arbitrary")),
        name="ffn",
    )(x, mod, w1, w2, g, b)


def _inproj_m_kernel(x_ref, mod_ref, w_ref, wg_ref, bg_ref, o_ref, gate_ref, h_scr):
    j = pl.program_id(1)

    @pl.when(j == 0)
    def _():
        mod = mod_ref[...]
        h = (x_ref[...] * (1.0 + mod[1:2]) + mod[0:1]).astype(BF16)
        h_scr[...] = h
        gate_ref[...] = jnp.dot(h, wg_ref[...], preferred_element_type=F32) + bg_ref[...]

    p = jnp.dot(h_scr[...], w_ref[...], preferred_element_type=F32)
    p = p * jnp.where(j == 1, M_DK ** -0.5, 1.0)
    o_ref[...] = p.astype(BF16)


def _inproj_m(x, mod, w, wg, bg):
    tm = TM_FFN
    return pl.pallas_call(
        _inproj_m_kernel,
        out_shape=[
            jax.ShapeDtypeStruct((NT, 4 * M_INNER), BF16),
            jax.ShapeDtypeStruct((NT, LANES), F32),
        ],
        grid=(NT // tm, 4),
        in_specs=[
            pl.BlockSpec((tm, D_MODEL), lambda i, j: (i, 0)),
            pl.BlockSpec((None, 6, D_MODEL), lambda i, j: (_cond_index(i, tm), 0, 0)),
            pl.BlockSpec((D_MODEL, M_INNER), lambda i, j: (0, j)),
            pl.BlockSpec((D_MODEL, LANES), lambda i, j: (0, 0)),
            pl.BlockSpec((1, LANES), lambda i, j: (0, 0)),
        ],
        out_specs=[
            pl.BlockSpec((tm, M_INNER), lambda i, j: (i, j)),
            pl.BlockSpec((tm, LANES), lambda i, j: (i, 0)),
        ],
        scratch_shapes=[pltpu.VMEM((tm, D_MODEL), BF16)],
        compiler_params=_params(("arbitrary", "arbitrary")),
        name="inproj_m",
    )(x, mod, w, wg, bg)


def _log_sigmoid(x):
    return jnp.minimum(x, 0.0) - jnp.log1p(jnp.exp(-jnp.abs(x)))


def _split3(x):
    x1 = x.astype(BF16)
    r1 = x - x1.astype(F32)
    x2 = r1.astype(BF16)
    x3 = (r1 - x2.astype(F32)).astype(BF16)
    return x1, x2, x3


def _mlstm_kernel(*refs, nc, has_init, emit_state):
    q_ref, k_ref, v_ref, og_ref, g_ref, gain_ref = refs[:6]
    pos = 6
    if has_init:
        c0_ref, n0_ref, m0_ref = refs[pos:pos + 3]
        pos += 3
    out_ref = refs[pos]
    pos += 1
    if emit_state:
        co_ref, no_ref, mo_ref = refs[pos:pos + 3]
        pos += 3
    hs_ref, c_scr, n_scr, m_scr = refs[pos:pos + 4]

    L = L_CHUNK
    head = pl.program_id(1)
    shift = (LANES - GATE_STRIDE * head) % LANES

    hs_ref[...] = jnp.zeros_like(hs_ref)
    if has_init:
        c_scr[...] = c0_ref[...]
        n_scr[...] = n0_ref[...]
        m_scr[...] = jnp.broadcast_to(m0_ref[...], m_scr.shape)
    else:
        c_scr[...] = jnp.zeros_like(c_scr)
        n_scr[...] = jnp.zeros_like(n_scr)
        m_scr[...] = jnp.zeros_like(m_scr)

    tpos = lax.broadcasted_iota(jnp.int32, (L, L), 0)
    spos = lax.broadcasted_iota(jnp.int32, (L, L), 1)

    def chunk(d, j):
        rows = pl.ds(j * L, L) if isinstance(j, int) else pl.ds(pl.multiple_of(j * L, L), L)
        qc = q_ref[rows, :]
        kc = k_ref[rows, :]
        vc = v_ref[rows, :]
        gh = pltpu.roll(g_ref[rows, :], shift, 1)
        fs = pltpu.roll(_log_sigmoid(gh), LANES - 2, 1)
        keep = (spos <= tpos) if d == 0 else (spos >= tpos)
        tri = jnp.where(keep, 1.0, 0.0).astype(BF16)
        f1, f2, f3 = _split3(fs)
        cs = (jnp.dot(tri, f1, preferred_element_type=F32)
              + jnp.dot(tri, f2, preferred_element_type=F32)
              + jnp.dot(tri, f3, preferred_element_type=F32))
        a_col = cs[:, d:d + 1]
        bc = gh - cs
        b_row = bc.T[d:d + 1, :]
        tot = cs[L - 1:L, d:d + 1] if d == 0 else cs[0:1, d:d + 1]
        m = m_scr[d][:, 0:1]

        dmat = jnp.where(keep, a_col + b_row, -jnp.inf)
        a_inter = a_col + m
        m_t = jnp.maximum(a_inter, jnp.max(dmat, axis=1, keepdims=True))
        w = jnp.exp(dmat - m_t)
        s_inter = jnp.exp(a_inter - m_t)
        s = lax.dot_general(qc, kc, (((1,), (1,)), ((), ())), preferred_element_type=F32)
        qk = s * w
        c = c_scr[d]
        n = n_scr[d]
        num = (jnp.dot(qk.astype(BF16), vc, preferred_element_type=F32)
               + s_inter * jnp.dot(qc, c.astype(BF16), preferred_element_type=F32))
        den = (jnp.sum(qk, axis=1, keepdims=True)
               + s_inter * jnp.sum(qc.astype(F32) * n, axis=1, keepdims=True))
        h = num / jnp.maximum(jnp.abs(den), jnp.exp(-m_t))
        hs_ref[rows, :] = hs_ref[rows, :] + h

        g_row = tot + b_row
        m_new = jnp.maximum(tot + m, jnp.max(g_row, axis=1, keepdims=True))
        ws = jnp.exp(tot + bc[:, d:d + 1] - m_new)
        decay = jnp.exp(tot + m - m_new)
        kw = kc.astype(F32) * ws
        c_scr[d] = decay * c + jnp.dot(kw.T.astype(BF16), vc, preferred_element_type=F32)
        n_scr[d] = decay * n + jnp.sum(kw, axis=0, keepdims=True)
        m_scr[d] = jnp.broadcast_to(m_new, (1, LANES))

    if nc == 1:
        chunk(0, 0)
        chunk(1, 0)
    else:
        def body(j, carry):
            chunk(0, j)
            chunk(1, nc - 1 - j)
            return carry
        lax.fori_loop(0, nc, body, 0)

    gain = gain_ref[...]

    def finish(j, carry):
        rows = pl.ds(j * L, L) if isinstance(j, int) else pl.ds(pl.multiple_of(j * L, L), L)
        hsum = hs_ref[rows, :]
        hn = hsum * lax.rsqrt(jnp.mean(hsum * hsum, axis=1, keepdims=True) + EPS) * gain
        out_ref[rows, :] = (_sigmoid(og_ref[rows, :].astype(F32)) * hn).astype(BF16)
        return carry

    if nc == 1:
        finish(0, 0)
    else:
        lax.fori_loop(0, nc, finish, 0)

    if emit_state:
        co_ref[...] = c_scr[...]
        no_ref[...] = n_scr[...]
        mo_ref[...] = m_scr[:, :, 0:1]


def _mlstm(qkvo, gates, gain, nb, s, row_blk0, init=None, emit_state=False):
    nc = s // L_CHUNK
    hq = M_INNER // M_DK
    in_specs = [
        pl.BlockSpec((s, M_DK), lambda b, h: (row_blk0 + b, h)),
        pl.BlockSpec((s, M_DK), lambda b, h: (row_blk0 + b, hq + h)),
        pl.BlockSpec((s, M_DV), lambda b, h: (row_blk0 + b, 2 * hq + h)),
        pl.BlockSpec((s, M_DV), lambda b, h: (row_blk0 + b, 3 * hq + h)),
        pl.BlockSpec((s, LANES), lambda b, h: (row_blk0 + b, 0)),
        pl.BlockSpec((1, M_DV), lambda b, h: (0, h)),
    ]
    args = [qkvo, qkvo, qkvo, qkvo, gates, gain]
    state_specs = [
        pl.BlockSpec((None, 2, None, M_DK, M_DV), lambda b, h: (b, 0, h, 0, 0)),
        pl.BlockSpec((None, 2, None, 1, M_DK), lambda b, h: (b, 0, h, 0, 0)),
        pl.BlockSpec((None, 2, None, 1, 1), lambda b, h: (b, 0, h, 0, 0)),
    ]
    if init is not None:
        in_specs += state_specs
        args += list(init)
    out_shape = [jax.ShapeDtypeStruct((nb * s, M_INNER), BF16)]
    out_specs = [pl.BlockSpec((s, M_DV), lambda b, h: (b, h))]
    if emit_state:
        out_shape += [
            jax.ShapeDtypeStruct((nb, 2, M_HEADS, M_DK, M_DV), F32),
            jax.ShapeDtypeStruct((nb, 2, M_HEADS, 1, M_DK), F32),
            jax.ShapeDtypeStruct((nb, 2, M_HEADS, 1, 1), F32),
        ]
        out_specs += state_specs
    return pl.pallas_call(
        functools.partial(_mlstm_kernel, nc=nc, has_init=init is not None, emit_state=emit_state),
        out_shape=out_shape,
        grid=(nb, M_HEADS),
        in_specs=in_specs,
        out_specs=out_specs,
        scratch_shapes=[
            pltpu.VMEM((s, M_DV), F32),
            pltpu.VMEM((2, M_DK, M_DV), F32),
            pltpu.VMEM((2, 1, M_DK), F32),
            pltpu.VMEM((2, 1, LANES), F32),
        ],
        compiler_params=_params(("arbitrary", "arbitrary")),
        name="mlstm_state" if emit_state else "mlstm",
    )(*args)


@jax.jit
def kernel(x_prompt, x_sample, cache_k, cache_v, state_c, state_n, state_m, c, c_ctx, w_ada, b_ada,
           ln_g, ln_b, w_ff1, w_ff2, w_in_a, q_gain, k_gain, conv_w, conv_b, conv_ln_g, conv_ln_b,
           w_out_a, w_in_m, b_gate_m, mh_gain, w_out_m):
    xp = x_prompt.reshape(NP, D_MODEL)
    xs = x_sample.reshape(NS, D_MODEL)

    cond = jnp.concatenate(
        [c_ctx[None, :], c, jnp.zeros((N_COND - 1 - DEC_BATCH, D_MODEL), F32)], axis=0)
    mods = _adaln(cond, w_ada, b_ada)

    ln = lambda l, s: (ln_g[l, s].reshape(1, D_MODEL), ln_b[l, s].reshape(1, D_MODEL))

    q, k, vt, kf, vf, u = _inproj_a(xp, xs, mods[0], w_in_a[0].astype(BF16), q_gain[0], k_gain[0])
    att_p = _attention_prompt(q, k, vt)
    ctx_k = jnp.transpose(cache_k[:, 0], (0, 2, 1, 3)).astype(BF16)
    ctx_vt = jnp.transpose(cache_v[:, 0], (0, 2, 3, 1)).astype(BF16)
    att_s = _attention_sample(q, k, vt, ctx_k, ctx_vt)
    uc = _conv_module(u, conv_w[0], conv_b[0], conv_ln_g[0], conv_ln_b[0])
    x1 = _outproj_a(att_p, att_s, uc, xp, xs, mods[0], w_out_a[0].astype(BF16), *ln(0, 0))
    x2 = _ffn(x1, 0, NT, mods[0], w_ff1[0].astype(BF16), w_ff2[0].astype(BF16), *ln(0, 1))

    wm = w_in_m[0]
    src = jnp.array([4 * M_INNER + g * M_HEADS + h
                     for h in range(M_HEADS) for g in (0, 2, 1, 3)], jnp.int32)
    dst = jnp.array([h * GATE_STRIDE + t for h in range(M_HEADS) for t in range(4)], jnp.int32)
    wg = jnp.zeros((D_MODEL, LANES), F32).at[:, dst].set(wm[:, src]).astype(BF16)
    bg = jnp.zeros((1, LANES), F32).at[0, dst].set(b_gate_m[0][src - 4 * M_INNER])
    qkvo, gates = _inproj_m(x2, mods[1], wm[:, :4 * M_INNER].astype(BF16), wg, bg)
    gain = mh_gain[0].reshape(1, M_INNER)
    hg_p, st_c, st_n, st_m = _mlstm(qkvo, gates, gain, BATCH, SEQ, 0, emit_state=True)
    init = (state_c[:, 0], state_n[:, 0].reshape(DEC_BATCH, 2, M_HEADS, 1, M_DK),
            state_m[:, 0].reshape(DEC_BATCH, 2, M_HEADS, 1, 1))
    (hg_s,) = _mlstm(qkvo, gates, gain, DEC_BATCH, DEC_SEQ, NP // DEC_SEQ, init=init)
    x3 = _outproj_m(hg_p, hg_s, x2, mods[1], w_out_m[0].astype(BF16), *ln(1, 0))
    w1 = w_ff1[1].astype(BF16)
    w2 = w_ff2[1].astype(BF16)
    y_p = _ffn(x3, 0, NP, mods[1], w1, w2, *ln(1, 1))
    y_s = _ffn(x3, NP, NS, mods[1], w1, w2, *ln(1, 1))

    new_k = kf[:NP].reshape(BATCH, 1, SEQ, ATT_KV_HEADS, HEAD_DIM)
    new_v = vf[:NP].reshape(BATCH, 1, SEQ, ATT_KV_HEADS, HEAD_DIM)
    return (y_p.reshape(BATCH, SEQ, D_MODEL), y_s.reshape(DEC_BATCH, DEC_SEQ, D_MODEL),
            new_k, new_v,
            st_c.reshape(BATCH, 1, 2, M_HEADS, M_DK, M_DV),
            st_n.reshape(BATCH, 1, 2, M_HEADS, M_DK),
            st_m.reshape(BATCH, 1, 2, M_HEADS))
```

```python
import functools

import jax
import jax.numpy as jnp
from jax import lax
from jax.experimental import pallas as pl
from jax.experimental.pallas import tpu as pltpu

F32 = jnp.float32
BF16 = jnp.bfloat16

D_MODEL = 1024
BATCH = 32
SEQ = 256
DEPTH = 2
DEC_BATCH = 4
DEC_SEQ = 4096
PAST_LEN = 512
GRID_W = 64
ATT_HEADS = 8
ATT_KV_HEADS = 2
HEAD_DIM = 64
ATT_GROUP = ATT_HEADS // ATT_KV_HEADS
ATT_W = ATT_HEADS * HEAD_DIM
KV_W = ATT_KV_HEADS * HEAD_DIM
ROPE_AXIS_DIM = HEAD_DIM // 2
ROPE_THETA = 10000.0
CONV_CH = D_MODEL // 2
CONV_K = 31
EVEN_IN = ATT_W + 2 * KV_W + 2 * CONV_CH
M_HEADS = 4
M_INNER = D_MODEL
M_DK = M_INNER // M_HEADS
M_DV = M_INNER // M_HEADS
D_FF = 4 * D_MODEL
ALPHA = (2 * DEPTH) ** 0.25
EPS = 1e-6
LOG2E = 1.4426950408889634

NP = BATCH * SEQ
NS = DEC_BATCH * DEC_SEQ
NT = NP + NS
N_COND = 8

LANES = 128
VMEM_LIMIT = 56 * 1024 * 1024

TM_PROJ = 512
TM_FFN = 1024
TF_FFN = 1024
TQ_ATT = 256
TK_ATT = 512
R_CONV = 256
HALO = 16
L_CHUNK = 256
GATE_STRIDE = 8
ONES_ROWS = 16
GATE_ROWS = 32
ATT_SPLIT = 1
MLSTM_UNROLL = 2
MLSTM_SEQS = 4
ATT_UNROLL = 4


def _cond_index(i_global, tm):
    npt = NP // tm
    tps = DEC_SEQ // tm
    return jnp.where(i_global < npt, 0, 1 + (i_global - npt) // tps)


def _layernorm(r, g, b):
    mu = jnp.mean(r, axis=-1, keepdims=True)
    d = r - mu
    var = jnp.mean(d * d, axis=-1, keepdims=True)
    return d * lax.rsqrt(var + EPS) * g + b


def _sigmoid(x):
    return 1.0 / (1.0 + jnp.exp(-x))


def _params(sem, vmem=VMEM_LIMIT):
    return pltpu.CompilerParams(dimension_semantics=sem, vmem_limit_bytes=vmem)


def _adaln_kernel(cond_ref, w_ref, b_ref, o_ref):
    c = cond_ref[...]
    s = (c * _sigmoid(c)).astype(BF16)
    o_ref[...] = jnp.dot(s, w_ref[...].astype(BF16), preferred_element_type=F32) + b_ref[...]


def _adaln(cond, w_ada, b_ada):
    tn = 1536
    n = 6 * D_MODEL
    out = pl.pallas_call(
        _adaln_kernel,
        out_shape=jax.ShapeDtypeStruct((DEPTH, N_COND, n), F32),
        grid=(DEPTH, n // tn),
        in_specs=[
            pl.BlockSpec((N_COND, D_MODEL), lambda l, j: (0, 0)),
            pl.BlockSpec((None, D_MODEL, tn), lambda l, j: (l, 0, j)),
            pl.BlockSpec((None, 1, tn), lambda l, j: (l, 0, j)),
        ],
        out_specs=pl.BlockSpec((None, N_COND, tn), lambda l, j: (l, 0, j)),
        compiler_params=_params(("arbitrary", "arbitrary")),
        name="adaln",
    )(cond, w_ada, b_ada.reshape(DEPTH, 1, n))
    return out.reshape(DEPTH, N_COND, 6, D_MODEL)


def _inproj_a_kernel(xp_ref, xs_ref, mod_ref, w_ref, qg_ref, kg_ref, cos_ref, sin_ref,
                     q_ref, k_ref, vt_ref, kf_ref, vf_ref, u_ref, *, npt):
    i = pl.program_id(0)
    x = jnp.where(i < npt, xp_ref[...], xs_ref[...])
    mod = mod_ref[...]
    h = (x * (1.0 + mod[1:2]) + mod[0:1]).astype(BF16)
    proj = jnp.dot(h, w_ref[...], preferred_element_type=F32)
    tm = proj.shape[0]

    ri = lax.broadcasted_iota(jnp.int32, (LANES, LANES), 0) // HEAD_DIM
    ci = lax.broadcasted_iota(jnp.int32, (LANES, LANES), 1) // HEAD_DIM
    seg = jnp.where(ri == ci, 1.0, 0.0).astype(BF16)
    lane = lax.broadcasted_iota(jnp.int32, (tm, LANES), 1)
    even = (lane % 2) == 0
    cos = cos_ref[...]
    sin = sin_ref[...]

    def norm(xc, gain):
        ss = jnp.dot((xc * xc).astype(BF16), seg, preferred_element_type=F32)
        return xc * lax.rsqrt(ss * (1.0 / HEAD_DIM) + EPS) * gain

    def rope(xn):
        partner = jnp.where(even, pltpu.roll(xn, LANES - 1, 1), pltpu.roll(xn, 1, 1))
        return xn * cos + partner * sin

    qg = qg_ref[...]
    for c in range(ATT_W // LANES):
        qr = rope(norm(proj[:, c * LANES:(c + 1) * LANES], qg))
        qs = (qr * (HEAD_DIM ** -0.5 * LOG2E)).astype(BF16)
        q_ref[2 * c] = qs[:, :HEAD_DIM]
        q_ref[2 * c + 1] = qs[:, HEAD_DIM:]

    kn = norm(proj[:, ATT_W:ATT_W + KV_W], kg_ref[...])
    kf_ref[...] = kn
    kr = rope(kn).astype(BF16)
    k_ref[0] = kr[:, :HEAD_DIM]
    k_ref[1] = kr[:, HEAD_DIM:]

    v = proj[:, ATT_W + KV_W:ATT_W + 2 * KV_W]
    vf_ref[...] = v
    vt_ref[...] = v.T.astype(BF16)

    off = ATT_W + 2 * KV_W
    a = proj[:, off:off + CONV_CH]
    gt = proj[:, off + CONV_CH:off + 2 * CONV_CH]
    u_ref[...] = a * _sigmoid(gt)


def _rope_tables(tm):
    t = jnp.arange(DEC_SEQ)
    row = (t // GRID_W).astype(F32)
    col = (t % GRID_W).astype(F32)
    freqs = ROPE_THETA ** (-jnp.arange(0, ROPE_AXIS_DIM, 2, dtype=F32) / ROPE_AXIS_DIM)
    ang = jnp.concatenate([row[:, None] * freqs, col[:, None] * freqs], axis=-1)
    pair = (jnp.arange(LANES) % HEAD_DIM) // 2
    sign = jnp.where(jnp.arange(LANES) % 2 == 0, -1.0, 1.0).astype(F32)
    cos = jnp.cos(ang)[:, pair]
    sin = jnp.sin(ang)[:, pair] * sign
    cos = jnp.concatenate([jnp.ones((tm, LANES), F32), cos], axis=0)
    sin = jnp.concatenate([jnp.zeros((tm, LANES), F32), sin], axis=0)
    return cos, sin


def _inproj_a(xp, xs, mod, w, q_gain, k_gain):
    tm = TM_PROJ
    npt = NP // tm
    tps = DEC_SEQ // tm
    nt = NT // tm
    cos, sin = _rope_tables(tm)
    qg = jnp.tile(q_gain, LANES // HEAD_DIM).reshape(1, LANES)
    kg = jnp.tile(k_gain, LANES // HEAD_DIM).reshape(1, LANES)

    def rope_idx(i):
        return (jnp.where(i < npt, 0, 1 + (i - npt) % tps), 0)

    return pl.pallas_call(
        functools.partial(_inproj_a_kernel, npt=npt),
        out_shape=[
            jax.ShapeDtypeStruct((ATT_HEADS, NT, HEAD_DIM), BF16),
            jax.ShapeDtypeStruct((ATT_KV_HEADS, NT, HEAD_DIM), BF16),
            jax.ShapeDtypeStruct((nt, KV_W, tm), BF16),
            jax.ShapeDtypeStruct((NT, KV_W), F32),
            jax.ShapeDtypeStruct((NT, KV_W), F32),
            jax.ShapeDtypeStruct((NT, CONV_CH), F32),
        ],
        grid=(nt,),
        in_specs=[
            pl.BlockSpec((tm, D_MODEL), lambda i: (jnp.minimum(i, npt - 1), 0)),
            pl.BlockSpec((tm, D_MODEL), lambda i: (jnp.maximum(i - npt, 0), 0)),
            pl.BlockSpec((None, 6, D_MODEL), lambda i: (_cond_index(i, tm), 0, 0)),
            pl.BlockSpec((D_MODEL, EVEN_IN), lambda i: (0, 0)),
            pl.BlockSpec((1, LANES), lambda i: (0, 0)),
            pl.BlockSpec((1, LANES), lambda i: (0, 0)),
            pl.BlockSpec((tm, LANES), rope_idx),
            pl.BlockSpec((tm, LANES), rope_idx),
        ],
        out_specs=[
            pl.BlockSpec((ATT_HEADS, tm, HEAD_DIM), lambda i: (0, i, 0)),
            pl.BlockSpec((ATT_KV_HEADS, tm, HEAD_DIM), lambda i: (0, i, 0)),
            pl.BlockSpec((None, KV_W, tm), lambda i: (i, 0, 0)),
            pl.BlockSpec((tm, KV_W), lambda i: (i, 0)),
            pl.BlockSpec((tm, KV_W), lambda i: (i, 0)),
            pl.BlockSpec((tm, CONV_CH), lambda i: (i, 0)),
        ],
        compiler_params=_params(("arbitrary",)),
        name="inproj_a",
    )(xp, xs, mod, w, qg, kg, cos, sin)


def _attn_step(qh, kc, vta, m, acc):
    s = lax.dot_general(kc, qh, (((1,), (1,)), ((), ())), preferred_element_type=F32)
    m_new = jnp.maximum(m, jnp.max(s, axis=0, keepdims=True))
    alpha = jnp.exp2(m - m_new)
    p = jnp.exp2(s - m_new).astype(BF16)
    acc = alpha * acc + jnp.dot(vta, p, preferred_element_type=F32)
    return m_new, acc


def _with_ones(vtc):
    return jnp.concatenate([vtc, jnp.ones((ONES_ROWS, vtc.shape[1]), BF16)], axis=0)


def _attn_chunk(q_ref, kc, vtc, carry):
    vta = _with_ones(vtc)
    hg = ATT_GROUP // ATT_SPLIT
    tq = q_ref.shape[1]
    out = []
    for g in range(ATT_SPLIT):
        qg = q_ref[g * hg:(g + 1) * hg].reshape(hg * tq, HEAD_DIM)
        out.append(_attn_step(qg, kc, vta, *carry[g]))
    return tuple(out)


def _attn_init(tq):
    nq = (ATT_GROUP // ATT_SPLIT) * tq
    one = (jnp.full((1, nq), -1e30, F32), jnp.zeros((HEAD_DIM + ONES_ROWS, nq), F32))
    return (one,) * ATT_SPLIT


def _attn_finish(carry, o_ref):
    tq = o_ref.shape[0]
    outs = []
    for _, acc in carry:
        o = acc[:HEAD_DIM] * (1.0 / acc[HEAD_DIM:HEAD_DIM + 1])
        outs += [o[:, h * tq:(h + 1) * tq] for h in range(ATT_GROUP // ATT_SPLIT)]
    o_ref[...] = jnp.concatenate(outs, axis=0).T.astype(BF16)


def _attn_prompt_kernel(q_ref, k_ref, vt_ref, o_ref):
    carry = _attn_chunk(q_ref, k_ref[...], vt_ref[...], _attn_init(q_ref.shape[1]))
    _attn_finish(carry, o_ref)


def _attn_sample_kernel(q_ref, k_ref, vt_ref, ck_ref, cvt_ref, o_ref, *, nchunks):
    def body(j, carry):
        r0 = pl.multiple_of(j * TK_ATT, TK_ATT)
        return _attn_chunk(q_ref, k_ref[pl.ds(r0, TK_ATT), :], vt_ref[j], carry)

    carry = lax.fori_loop(0, nchunks, body, _attn_init(q_ref.shape[1]), unroll=ATT_UNROLL)
    carry = _attn_chunk(q_ref, ck_ref[...], cvt_ref[...], carry)
    _attn_finish(carry, o_ref)


def _attention_prompt(q, k, vt):
    return pl.pallas_call(
        _attn_prompt_kernel,
        out_shape=jax.ShapeDtypeStruct((NP, ATT_W), BF16),
        grid=(BATCH, ATT_KV_HEADS),
        in_specs=[
            pl.BlockSpec((ATT_GROUP, SEQ, HEAD_DIM), lambda b, g: (g, b, 0)),
            pl.BlockSpec((None, SEQ, HEAD_DIM), lambda b, g: (g, b, 0)),
            pl.BlockSpec((None, HEAD_DIM, SEQ),
                         lambda b, g: (b // (TM_PROJ // SEQ), g, b % (TM_PROJ // SEQ))),
        ],
        out_specs=pl.BlockSpec((SEQ, ATT_GROUP * HEAD_DIM), lambda b, g: (b, g)),
        compiler_params=_params(("arbitrary", "arbitrary")),
        name="attn_prompt",
    )(q, k, vt)


def _attention_sample(q, k, vt, ctx_k, ctx_vt):
    tq = TQ_ATT
    nchunks = DEC_SEQ // TK_ATT
    q_off = NP // tq
    kv_off = NP // DEC_SEQ
    return pl.pallas_call(
        functools.partial(_attn_sample_kernel, nchunks=nchunks),
        out_shape=jax.ShapeDtypeStruct((NS, ATT_W), BF16),
        grid=(DEC_BATCH, ATT_KV_HEADS, DEC_SEQ // tq),
        in_specs=[
            pl.BlockSpec((ATT_GROUP, tq, HEAD_DIM),
                         lambda b, g, i: (g, q_off + b * (DEC_SEQ // tq) + i, 0)),
            pl.BlockSpec((None, DEC_SEQ, HEAD_DIM), lambda b, g, i: (g, kv_off + b, 0)),
            pl.BlockSpec((nchunks, HEAD_DIM, TK_ATT), lambda b, g, i: (kv_off + b, g, 0)),
            pl.BlockSpec((None, None, PAST_LEN, HEAD_DIM), lambda b, g, i: (b, g, 0, 0)),
            pl.BlockSpec((None, None, HEAD_DIM, PAST_LEN), lambda b, g, i: (b, g, 0, 0)),
        ],
        out_specs=pl.BlockSpec((tq, ATT_GROUP * HEAD_DIM),
                               lambda b, g, i: (b * (DEC_SEQ // tq) + i, g)),
        compiler_params=_params(("arbitrary", "arbitrary", "arbitrary")),
        name="attn_sample",
    )(q, k, vt, ctx_k, ctx_vt)


def _conv_kernel(prev_ref, cur_ref, next_ref, w_ref, b_ref, g_ref, bb_ref, o_ref, win_ref, y_ref,
                 *, npt, tps):
    r = cur_ref.shape[0]
    i = pl.program_id(0)
    j = (i - npt) % tps
    first = jnp.logical_or(i < npt, j == 0)
    last = jnp.logical_or(i < npt, j == tps - 1)
    win_ref[0:HALO, :] = jnp.where(first, 0.0, prev_ref[...])
    win_ref[HALO:HALO + r, :] = cur_ref[...]
    win_ref[HALO + r:2 * HALO + r, :] = jnp.where(last, 0.0, next_ref[...])
    base = HALO - CONV_K // 2
    for c in range(CONV_CH // LANES):
        cs = slice(c * LANES, (c + 1) * LANES)
        acc = jnp.zeros((r, LANES), F32)
        for k in range(CONV_K):
            acc = acc + win_ref[base + k:base + k + r, cs] * w_ref[k:k + 1, cs]
        y_ref[:, cs] = acc + b_ref[:, cs]
    y = _layernorm(y_ref[...], g_ref[...], bb_ref[...])
    o_ref[...] = (y * _sigmoid(y)).astype(BF16)


def _conv_module(u, conv_w, conv_b, cln_g, cln_b):
    r = R_CONV
    npt = NP // r
    tps = DEC_SEQ // r
    hb = r // HALO
    nh = NT // HALO
    w = jnp.concatenate([conv_w, jnp.zeros((1, CONV_CH), F32)], axis=0)
    return pl.pallas_call(
        functools.partial(_conv_kernel, npt=npt, tps=tps),
        out_shape=jax.ShapeDtypeStruct((NT, CONV_CH), BF16),
        grid=(NT // r,),
        in_specs=[
            pl.BlockSpec((HALO, CONV_CH), lambda i: (jnp.maximum(i * hb - 1, 0), 0)),
            pl.BlockSpec((r, CONV_CH), lambda i: (i, 0)),
            pl.BlockSpec((HALO, CONV_CH), lambda i: (jnp.minimum((i + 1) * hb, nh - 1), 0)),
            pl.BlockSpec((CONV_K + 1, CONV_CH), lambda i: (0, 0)),
            pl.BlockSpec((1, CONV_CH), lambda i: (0, 0)),
            pl.BlockSpec((1, CONV_CH), lambda i: (0, 0)),
            pl.BlockSpec((1, CONV_CH), lambda i: (0, 0)),
        ],
        out_specs=pl.BlockSpec((r, CONV_CH), lambda i: (i, 0)),
        scratch_shapes=[pltpu.VMEM((r + 2 * HALO, CONV_CH), F32), pltpu.VMEM((r, CONV_CH), F32)],
        compiler_params=_params(("arbitrary",)),
        name="conv_module",
    )(u, u, u, w, conv_b.reshape(1, CONV_CH), cln_g.reshape(1, CONV_CH), cln_b.reshape(1, CONV_CH))


def _outproj_a_kernel(ap_ref, as_ref, uc_ref, xp_ref, xs_ref, mod_ref, w_ref, g_ref, b_ref, o_ref,
                      *, npt):
    i = pl.program_id(0)
    att = jnp.where(i < npt, ap_ref[...], as_ref[...])
    x = jnp.where(i < npt, xp_ref[...], xs_ref[...])
    y = jnp.dot(att, w_ref[0:ATT_W, :], preferred_element_type=F32)
    y = y + jnp.dot(uc_ref[...], w_ref[ATT_W:ATT_W + CONV_CH, :], preferred_element_type=F32)
    mod = mod_ref[...]
    o_ref[...] = _layernorm(ALPHA * x + mod[2:3] * y, g_ref[...], b_ref[...])


def _outproj_a(att_p, att_s, uc, xp, xs, mod, w, g, b):
    tm = TM_PROJ
    npt = NP // tm
    first = lambda i: (jnp.minimum(i, npt - 1), 0)
    second = lambda i: (jnp.maximum(i - npt, 0), 0)
    return pl.pallas_call(
        functools.partial(_outproj_a_kernel, npt=npt),
        out_shape=jax.ShapeDtypeStruct((NT, D_MODEL), F32),
        grid=(NT // tm,),
        in_specs=[
            pl.BlockSpec((tm, ATT_W), first),
            pl.BlockSpec((tm, ATT_W), second),
            pl.BlockSpec((tm, CONV_CH), lambda i: (i, 0)),
            pl.BlockSpec((tm, D_MODEL), first),
            pl.BlockSpec((tm, D_MODEL), second),
            pl.BlockSpec((None, 6, D_MODEL), lambda i: (_cond_index(i, tm), 0, 0)),
            pl.BlockSpec((ATT_W + CONV_CH, D_MODEL), lambda i: (0, 0)),
            pl.BlockSpec((1, D_MODEL), lambda i: (0, 0)),
            pl.BlockSpec((1, D_MODEL), lambda i: (0, 0)),
        ],
        out_specs=pl.BlockSpec((tm, D_MODEL), lambda i: (i, 0)),
        compiler_params=_params(("arbitrary",)),
        name="outproj_a",
    )(att_p, att_s, uc, xp, xs, mod, w, g, b)


def _outproj_m_kernel(hp_ref, hs_ref, x_ref, mod_ref, w_ref, g_ref, b_ref, o_ref, *, npt):
    i = pl.program_id(0)
    hg = jnp.where(i < npt, hp_ref[...], hs_ref[...])
    y = jnp.dot(hg, w_ref[...], preferred_element_type=F32)
    mod = mod_ref[...]
    o_ref[...] = _layernorm(ALPHA * x_ref[...] + mod[2:3] * y, g_ref[...], b_ref[...])


def _outproj_m(hg_p, hg_s, x, mod, w, g, b):
    tm = TM_PROJ
    npt = NP // tm
    return pl.pallas_call(
        functools.partial(_outproj_m_kernel, npt=npt),
        out_shape=jax.ShapeDtypeStruct((NT, D_MODEL), F32),
        grid=(NT // tm,),
        in_specs=[
            pl.BlockSpec((tm, M_INNER), lambda i: (jnp.minimum(i, npt - 1), 0)),
            pl.BlockSpec((tm, M_INNER), lambda i: (jnp.maximum(i - npt, 0), 0)),
            pl.BlockSpec((tm, D_MODEL), lambda i: (i, 0)),
            pl.BlockSpec((None, 6, D_MODEL), lambda i: (_cond_index(i, tm), 0, 0)),
            pl.BlockSpec((M_INNER, D_MODEL), lambda i: (0, 0)),
            pl.BlockSpec((1, D_MODEL), lambda i: (0, 0)),
            pl.BlockSpec((1, D_MODEL), lambda i: (0, 0)),
        ],
        out_specs=pl.BlockSpec((tm, D_MODEL), lambda i: (i, 0)),
        compiler_params=_params(("arbitrary",)),
        name="outproj_m",
    )(hg_p, hg_s, x, mod, w, g, b)


def _ffn_kernel(x_ref, mod_ref, w1_ref, w2_ref, g_ref, b_ref, o_ref, h_scr, acc_scr):
    j = pl.program_id(1)

    @pl.when(j == 0)
    def _():
        mod = mod_ref[...]
        h_scr[...] = (x_ref[...] * (1.0 + mod[4:5]) + mod[3:4]).astype(BF16)
        acc_scr[...] = jnp.zeros_like(acc_scr)

    a = jnp.dot(h_scr[...], w1_ref[...], preferred_element_type=F32)
    a = jnp.maximum(a, 0.0)
    acc_scr[...] += jnp.dot((a * a).astype(BF16), w2_ref[...], preferred_element_type=F32)

    @pl.when(j == pl.num_programs(1) - 1)
    def _():
        mod = mod_ref[...]
        o_ref[...] = _layernorm(ALPHA * x_ref[...] + mod[5:6] * acc_scr[...], g_ref[...], b_ref[...])


def _ffn(x, row0, nrows, mod, w1, w2, g, b):
    tm, tf = TM_FFN, TF_FFN
    off = row0 // tm
    return pl.pallas_call(
        _ffn_kernel,
        out_shape=jax.ShapeDtypeStruct((nrows, D_MODEL), F32),
        grid=(nrows // tm, D_FF // tf),
        in_specs=[
            pl.BlockSpec((tm, D_MODEL), lambda i, j: (i + off, 0)),
            pl.BlockSpec((None, 6, D_MODEL), lambda i, j: (_cond_index(i + off, tm), 0, 0)),
            pl.BlockSpec((D_MODEL, tf), lambda i, j: (0, j)),
            pl.BlockSpec((tf, D_MODEL), lambda i, j: (j, 0)),
            pl.BlockSpec((1, D_MODEL), lambda i, j: (0, 0)),
            pl.BlockSpec((1, D_MODEL), lambda i, j: (0, 0)),
        ],
        out_specs=pl.BlockSpec((tm, D_MODEL), lambda i, j: (i, 0)),
        scratch_shapes=[pltpu.VMEM((tm, D_MODEL), BF16), pltpu.VMEM((tm, D_MODEL), F32)],
        compiler_params=_params(("arbitrary", "arbitrary")),
        name="ffn",
    )(x, mod, w1, w2, g, b)


def _inproj_m_kernel(x_ref, mod_ref, w_ref, wg_ref, bg_ref, o_ref, gt_ref, vt_ref, h_scr):
    j = pl.program_id(1)
    nslab = vt_ref.shape[0]
    L = L_CHUNK

    @pl.when(j == 0)
    def _():
        mod = mod_ref[...]
        h = (x_ref[...] * (1.0 + mod[1:2]) + mod[0:1]).astype(BF16)
        h_scr[...] = h
        gates = jnp.dot(h, wg_ref[...], preferred_element_type=F32) + bg_ref[...]
        kind = lax.broadcasted_iota(jnp.int32, gates.shape, 1) % GATE_STRIDE
        gates = jnp.where((kind == 2) | (kind == 3), _log_sigmoid(gates), gates)
        gates_t = gates.T
        for s in range(nslab):
            gt_ref[s] = gates_t[0:GATE_ROWS, s * L:(s + 1) * L]

    p = jnp.dot(h_scr[...], w_ref[...], preferred_element_type=F32)
    p = p * jnp.where(j == 1, M_DK ** -0.5, 1.0)
    o_ref[...] = p.astype(BF16)

    @pl.when(j == 2)
    def _():
        v_t = p.T.astype(BF16)
        for s in range(nslab):
            vt_ref[s] = v_t[:, s * L:(s + 1) * L]


def _inproj_m(x, mod, w, wg, bg):
    tm = TM_FFN
    nslab = tm // L_CHUNK
    return pl.pallas_call(
        _inproj_m_kernel,
        out_shape=[
            jax.ShapeDtypeStruct((NT, 4 * M_INNER), BF16),
            jax.ShapeDtypeStruct((NT // L_CHUNK, GATE_ROWS, L_CHUNK), F32),
            jax.ShapeDtypeStruct((NT // L_CHUNK, M_INNER, L_CHUNK), BF16),
        ],
        grid=(NT // tm, 4),
        in_specs=[
            pl.BlockSpec((tm, D_MODEL), lambda i, j: (i, 0)),
            pl.BlockSpec((None, 6, D_MODEL), lambda i, j: (_cond_index(i, tm), 0, 0)),
            pl.BlockSpec((D_MODEL, M_INNER), lambda i, j: (0, j)),
            pl.BlockSpec((D_MODEL, LANES), lambda i, j: (0, 0)),
            pl.BlockSpec((1, LANES), lambda i, j: (0, 0)),
        ],
        out_specs=[
            pl.BlockSpec((tm, M_INNER), lambda i, j: (i, j)),
            pl.BlockSpec((nslab, GATE_ROWS, L_CHUNK), lambda i, j: (i, 0, 0)),
            pl.BlockSpec((nslab, M_INNER, L_CHUNK), lambda i, j: (i, 0, 0)),
        ],
        scratch_shapes=[pltpu.VMEM((tm, D_MODEL), BF16)],
        compiler_params=_params(("arbitrary", "arbitrary")),
        name="inproj_m",
    )(x, mod, w, wg, bg)


def _log_sigmoid(x):
    return jnp.minimum(x, 0.0) - jnp.log1p(jnp.exp(-jnp.abs(x)))


def _split3(x):
    x1 = x.astype(BF16)
    r1 = x - x1.astype(F32)
    x2 = r1.astype(BF16)
    x3 = (r1 - x2.astype(F32)).astype(BF16)
    return x1, x2, x3


def _fold_lanes(x, op):
    acc = x[:, :LANES]
    for c in range(1, x.shape[1] // LANES):
        acc = op(acc, x[:, c * LANES:(c + 1) * LANES])
    return acc


def _row_sum(x):
    return jnp.sum(_fold_lanes(x, jnp.add), axis=1, keepdims=True)


def _row_max(x):
    return jnp.max(_fold_lanes(x, jnp.maximum), axis=1, keepdims=True)


def _lane_scan(x, op, fill, reverse):
    n = x.shape[1]
    lane = lax.broadcasted_iota(jnp.int32, x.shape, 1)
    sh = 1
    while sh < n:
        if reverse:
            x = op(x, jnp.where(lane < n - sh, pltpu.roll(x, n - sh, 1), fill))
        else:
            x = op(x, jnp.where(lane >= sh, pltpu.roll(x, sh, 1), fill))
        sh *= 2
    return x


def _mlstm_kernel(*refs, nc, nseq, has_init, emit_state):
    q_ref, k_ref, vt_ref, og_ref, gt_ref, gain_ref = refs[:6]
    pos = 6
    if has_init:
        c0_ref, n0_ref, m0_ref = refs[pos:pos + 3]
        pos += 3
    out_ref = refs[pos]
    pos += 1
    if emit_state:
        co_ref, no_ref, mo_ref = refs[pos:pos + 3]
        pos += 3
    hs_ref, ct_scr, n_scr, m_scr = refs[pos:pos + 4]

    L = L_CHUNK
    head = pl.program_id(1)
    grow = pl.multiple_of(head * GATE_STRIDE, GATE_STRIDE)
    zero_state = (not has_init) and nc == 1

    hs_ref[...] = jnp.zeros_like(hs_ref)
    if has_init:
        for d in range(2):
            ct_scr[d] = c0_ref[d].T
        n_scr[...] = n0_ref[...]
        m_scr[...] = jnp.broadcast_to(m0_ref[...], m_scr.shape)
    elif zero_state:
        pass
    else:
        ct_scr[...] = jnp.zeros_like(ct_scr)
        n_scr[...] = jnp.zeros_like(n_scr)
        m_scr[...] = jnp.zeros_like(m_scr)

    tpos = lax.broadcasted_iota(jnp.int32, (L, L), 0)
    spos = lax.broadcasted_iota(jnp.int32, (L, L), 1)

    def prep(j):
        rows = pl.ds(j * L, L) if isinstance(j, int) else pl.ds(pl.multiple_of(j * L, L), L)
        qc = q_ref[rows, :]
        kc = k_ref[rows, :]
        vtc = vt_ref[j]
        ght = gt_ref[j, pl.ds(grow, GATE_STRIDE), :]
        parts = jnp.concatenate([t.astype(F32) for t in _split3(ght)]
                                + [jnp.zeros((GATE_STRIDE, L), F32)], axis=0).astype(BF16)
        upper = jnp.where(tpos <= spos, 1.0, 0.0).astype(BF16)
        r = jnp.dot(parts, upper, preferred_element_type=F32)
        pre = r[0:GATE_STRIDE] + r[GATE_STRIDE:2 * GATE_STRIDE] + r[2 * GATE_STRIDE:3 * GATE_STRIDE]
        s = lax.dot_general(kc, qc, (((1,), (1,)), ((), ())), preferred_element_type=F32)
        return qc, kc, vtc, ght, pre, s

    def chunk(d, j, sd, prepared):
        qc, kc, vtc, ght, pre, s = prepared
        rev = d == 1
        keep = (tpos >= spos) if rev else (tpos <= spos)
        if rev:
            tot = pre[3:4, L - 1:L]
            a_row = tot - pre[3:4, :] + ght[3:4, :]
        else:
            a_row = pre[2:3, :]
            tot = a_row[:, L - 1:L]
        b_row = ght[d:d + 1, :] - a_row
        m = m_scr[sd][:, 0:1] if not zero_state else jnp.zeros((1, 1), F32)

        b_col = jnp.broadcast_to(b_row, (8, L)).T[:, 0:1]
        bm = jnp.where(keep, b_col, -jnp.inf)
        mm = jnp.maximum(m, jnp.max(bm, axis=0, keepdims=True))
        w = jnp.exp(bm - mm)
        qk = s * w
        num = jnp.dot(vtc, qk.astype(BF16), preferred_element_type=F32)
        den = jnp.sum(qk, axis=0, keepdims=True)
        if not zero_state:
            ct = ct_scr[sd]
            n = n_scr[sd]
            nrows = jnp.concatenate([t.astype(F32) for t in _split3(n)]
                                    + [jnp.zeros((ONES_ROWS - 3, M_DK), F32)], axis=0)
            cn = jnp.concatenate([ct, nrows], axis=0).astype(BF16)
            inter = lax.dot_general(cn, qc, (((1,), (1,)), ((), ())), preferred_element_type=F32)
            s_inter = jnp.exp(m - mm)
            num = num + s_inter * inter[:M_DV]
            qn = inter[M_DV:M_DV + 1] + inter[M_DV + 1:M_DV + 2] + inter[M_DV + 2:M_DV + 3]
            den = den + s_inter * qn
        h = num * (1.0 / jnp.maximum(jnp.abs(den), jnp.exp(-(a_row + mm))))
        hs_ref[j] = hs_ref[j] + h

        g_row = tot + b_row
        m_new = jnp.maximum(tot + m, jnp.max(g_row, axis=1, keepdims=True))
        ws = jnp.exp(g_row - m_new)
        wrows = jnp.concatenate([t.astype(F32) for t in _split3(ws)]
                                + [jnp.zeros((ONES_ROWS - 3, L), F32)], axis=0).astype(BF16)
        lhs = jnp.concatenate([(vtc.astype(F32) * ws).astype(BF16), wrows], axis=0)
        upd_all = jnp.dot(lhs, kc, preferred_element_type=F32)
        upd = upd_all[:M_DV]
        nsum = upd_all[M_DV:M_DV + 1] + upd_all[M_DV + 1:M_DV + 2] + upd_all[M_DV + 2:M_DV + 3]
        if zero_state:
            ct_scr[sd] = upd
            n_scr[sd] = nsum
        else:
            decay = jnp.exp(tot + m - m_new)
            ct_scr[sd] = decay * ct + upd
            n_scr[sd] = decay * n + nsum
        m_scr[sd] = jnp.broadcast_to(m_new, (1, LANES))

    if nc == 1:
        for bb in range(nseq):
            prepared = prep(bb)
            chunk(0, bb, 2 * bb, prepared)
            chunk(1, bb, 2 * bb + 1, prepared)
    else:
        def body(j, carry):
            chunk(0, j, 0, prep(j))
            chunk(1, nc - 1 - j, 1, prep(nc - 1 - j))
            return carry
        lax.fori_loop(0, nc, body, 0, unroll=MLSTM_UNROLL)

    gain = gain_ref[...]

    def finish(j, carry):
        rows = pl.ds(j * L, L) if isinstance(j, int) else pl.ds(pl.multiple_of(j * L, L), L)
        hsum = hs_ref[j].T
        hn = hsum * lax.rsqrt(_row_sum(hsum * hsum) * (1.0 / M_DV) + EPS) * gain
        out_ref[rows, :] = (_sigmoid(og_ref[rows, :].astype(F32)) * hn).astype(BF16)
        return carry

    if nc == 1:
        for bb in range(nseq):
            finish(bb, 0)
    else:
        lax.fori_loop(0, nc, finish, 0)

    if emit_state:
        for bb in range(nseq):
            for d in range(2):
                co_ref[bb, d] = ct_scr[2 * bb + d].T
                no_ref[bb, d] = n_scr[2 * bb + d]
                mo_ref[bb, d] = m_scr[2 * bb + d][:, 0:1]


def _mlstm(qkvo, vt, gates_t, gain, nb, seq, nseq, row_blk0, init=None, emit_state=False):
    assert nseq == 1 or seq == L_CHUNK
    s = nseq * seq
    nc = seq // L_CHUNK
    nslab = s // L_CHUNK
    hq = M_INNER // M_DK
    in_specs = [
        pl.BlockSpec((s, M_DK), lambda b, h: (row_blk0 + b, h)),
        pl.BlockSpec((s, M_DK), lambda b, h: (row_blk0 + b, hq + h)),
        pl.BlockSpec((nslab, M_DV, L_CHUNK), lambda b, h: (row_blk0 + b, h, 0)),
        pl.BlockSpec((s, M_DV), lambda b, h: (row_blk0 + b, 3 * hq + h)),
        pl.BlockSpec((nslab, GATE_ROWS, L_CHUNK), lambda b, h: (row_blk0 + b, 0, 0)),
        pl.BlockSpec((1, M_DV), lambda b, h: (0, h)),
    ]
    args = [qkvo, qkvo, vt, qkvo, gates_t, gain]
    state_specs = [
        pl.BlockSpec((nseq, 2, None, M_DK, M_DV), lambda b, h: (b, 0, h, 0, 0)),
        pl.BlockSpec((nseq, 2, None, 1, M_DK), lambda b, h: (b, 0, h, 0, 0)),
        pl.BlockSpec((nseq, 2, None, 1, 1), lambda b, h: (b, 0, h, 0, 0)),
    ]
    if init is not None:
        assert nseq == 1
        in_specs += [
            pl.BlockSpec((None, 2, None, M_DK, M_DV), lambda b, h: (b, 0, h, 0, 0)),
            pl.BlockSpec((None, 2, None, 1, M_DK), lambda b, h: (b, 0, h, 0, 0)),
            pl.BlockSpec((None, 2, None, 1, 1), lambda b, h: (b, 0, h, 0, 0)),
        ]
        args += list(init)
    out_shape = [jax.ShapeDtypeStruct((nb * seq, M_INNER), BF16)]
    out_specs = [pl.BlockSpec((s, M_DV), lambda b, h: (b, h))]
    if emit_state:
        out_shape += [
            jax.ShapeDtypeStruct((nb, 2, M_HEADS, M_DK, M_DV), F32),
            jax.ShapeDtypeStruct((nb, 2, M_HEADS, 1, M_DK), F32),
            jax.ShapeDtypeStruct((nb, 2, M_HEADS, 1, 1), F32),
        ]
        out_specs += state_specs
    return pl.pallas_call(
        functools.partial(_mlstm_kernel, nc=nc, nseq=nseq, has_init=init is not None,
                          emit_state=emit_state),
        out_shape=out_shape,
        grid=(nb // nseq, M_HEADS),
        in_specs=in_specs,
        out_specs=out_specs,
        scratch_shapes=[
            pltpu.VMEM((nslab, M_DV, L_CHUNK), F32),
            pltpu.VMEM((2 * nseq, M_DV, M_DK), F32),
            pltpu.VMEM((2 * nseq, 1, M_DK), F32),
            pltpu.VMEM((2 * nseq, 1, LANES), F32),
        ],
        compiler_params=_params(("arbitrary", "arbitrary")),
        name="mlstm_state" if emit_state else "mlstm",
    )(*args)


@jax.jit
def kernel(x_prompt, x_sample, cache_k, cache_v, state_c, state_n, state_m, c, c_ctx, w_ada, b_ada,
           ln_g, ln_b, w_ff1, w_ff2, w_in_a, q_gain, k_gain, conv_w, conv_b, conv_ln_g, conv_ln_b,
           w_out_a, w_in_m, b_gate_m, mh_gain, w_out_m):
    xp = x_prompt.reshape(NP, D_MODEL)
    xs = x_sample.reshape(NS, D_MODEL)

    cond = jnp.concatenate(
        [c_ctx[None, :], c, jnp.zeros((N_COND - 1 - DEC_BATCH, D_MODEL), F32)], axis=0)
    mods = _adaln(cond, w_ada, b_ada)

    ln = lambda l, s: (ln_g[l, s].reshape(1, D_MODEL), ln_b[l, s].reshape(1, D_MODEL))

    q, k, vt, kf, vf, u = _inproj_a(xp, xs, mods[0], w_in_a[0].astype(BF16), q_gain[0], k_gain[0])
    att_p = _attention_prompt(q, k, vt)
    ctx_k = jnp.transpose(cache_k[:, 0], (0, 2, 1, 3)).astype(BF16)
    ctx_vt = jnp.transpose(cache_v[:, 0], (0, 2, 3, 1)).astype(BF16)
    att_s = _attention_sample(q, k, vt, ctx_k, ctx_vt)
    uc = _conv_module(u, conv_w[0], conv_b[0], conv_ln_g[0], conv_ln_b[0])
    x1 = _outproj_a(att_p, att_s, uc, xp, xs, mods[0], w_out_a[0].astype(BF16), *ln(0, 0))
    x2 = _ffn(x1, 0, NT, mods[0], w_ff1[0].astype(BF16), w_ff2[0].astype(BF16), *ln(0, 1))

    wm = w_in_m[0]
    src = jnp.array([4 * M_INNER + g * M_HEADS + h
                     for h in range(M_HEADS) for g in (0, 2, 1, 3)], jnp.int32)
    dst = jnp.array([h * GATE_STRIDE + t for h in range(M_HEADS) for t in range(4)], jnp.int32)
    wg = jnp.zeros((D_MODEL, LANES), F32).at[:, dst].set(wm[:, src]).astype(BF16)
    bg = jnp.zeros((1, LANES), F32).at[0, dst].set(b_gate_m[0][src - 4 * M_INNER])
    qkvo, gates_t, vt_m = _inproj_m(x2, mods[1], wm[:, :4 * M_INNER].astype(BF16), wg, bg)
    gain = mh_gain[0].reshape(1, M_INNER)
    hg_p, st_c, st_n, st_m = _mlstm(qkvo, vt_m, gates_t, gain, BATCH, SEQ, MLSTM_SEQS, 0, emit_state=True)
    init = (state_c[:, 0], state_n[:, 0].reshape(DEC_BATCH, 2, M_HEADS, 1, M_DK),
            state_m[:, 0].reshape(DEC_BATCH, 2, M_HEADS, 1, 1))
    (hg_s,) = _mlstm(qkvo, vt_m, gates_t, gain, DEC_BATCH, DEC_SEQ, 1, NP // DEC_SEQ, init=init)
    x3 = _outproj_m(hg_p, hg_s, x2, mods[1], w_out_m[0].astype(BF16), *ln(1, 0))
    w1 = w_ff1[1].astype(BF16)
    w2 = w_ff2[1].astype(BF16)
    y_p = _ffn(x3, 0, NP, mods[1], w1, w2, *ln(1, 1))
    y_s = _ffn(x3, NP, NS, mods[1], w1, w2, *ln(1, 1))

    new_k = kf[:NP].reshape(BATCH, 1, SEQ, ATT_KV_HEADS, HEAD_DIM)
    new_v = vf[:NP].reshape(BATCH, 1, SEQ, ATT_KV_HEADS, HEAD_DIM)
    return (y_p.reshape(BATCH, SEQ, D_MODEL), y_s.reshape(DEC_BATCH, DEC_SEQ, D_MODEL),
            new_k, new_v,
            st_c.reshape(BATCH, 1, 2, M_HEADS, M_DK, M_DV),
            st_n.reshape(BATCH, 1, 2, M_HEADS, M_DK),
            st_m.reshape(BATCH, 1, 2, M_HEADS))
```

```python
import functools

import jax
import jax.numpy as jnp
from jax import lax
from jax.experimental import pallas as pl
from jax.experimental.pallas import tpu as pltpu

F32 = jnp.float32
BF16 = jnp.bfloat16

D_MODEL = 1024
BATCH = 32
SEQ = 256
DEPTH = 2
DEC_BATCH = 4
DEC_SEQ = 4096
PAST_LEN = 512
GRID_W = 64
ATT_HEADS = 8
ATT_KV_HEADS = 2
HEAD_DIM = 64
ATT_GROUP = ATT_HEADS // ATT_KV_HEADS
ATT_W = ATT_HEADS * HEAD_DIM
KV_W = ATT_KV_HEADS * HEAD_DIM
ROPE_AXIS_DIM = HEAD_DIM // 2
ROPE_THETA = 10000.0
CONV_CH = D_MODEL // 2
CONV_K = 31
EVEN_IN = ATT_W + 2 * KV_W + 2 * CONV_CH
M_HEADS = 4
M_INNER = D_MODEL
M_DK = M_INNER // M_HEADS
M_DV = M_INNER // M_HEADS
D_FF = 4 * D_MODEL
ALPHA = (2 * DEPTH) ** 0.25
EPS = 1e-6
LOG2E = 1.4426950408889634

NP = BATCH * SEQ
NS = DEC_BATCH * DEC_SEQ
NT = NP + NS
N_COND = 8

LANES = 128
VMEM_LIMIT = 56 * 1024 * 1024

TM_PROJ = 512
TM_FFN = 1024
TF_FFN = 1024
TQ_ATT = 256
TK_ATT = 512
R_CONV = 256
HALO = 16
L_CHUNK = 256
GATE_STRIDE = 8
ONES_ROWS = 16
GATE_ROWS = 32
ATT_SPLIT = 1
MLSTM_UNROLL = 2
MLSTM_SEQS = 4
ATT_UNROLL = 4


def _cond_index(i_global, tm):
    npt = NP // tm
    tps = DEC_SEQ // tm
    return jnp.where(i_global < npt, 0, 1 + (i_global - npt) // tps)


def _layernorm(r, g, b):
    mu = jnp.mean(r, axis=-1, keepdims=True)
    d = r - mu
    var = jnp.mean(d * d, axis=-1, keepdims=True)
    return d * lax.rsqrt(var + EPS) * g + b


def _sigmoid(x):
    return 1.0 / (1.0 + jnp.exp(-x))


def _params(sem, vmem=VMEM_LIMIT, flags=None):
    return pltpu.CompilerParams(dimension_semantics=sem, vmem_limit_bytes=vmem, flags=flags)


def _adaln_kernel(cond_ref, w_ref, b_ref, o_ref):
    c = cond_ref[...]
    s = (c * _sigmoid(c)).astype(BF16)
    o_ref[...] = jnp.dot(s, w_ref[...].astype(BF16), preferred_element_type=F32) + b_ref[...]


def _adaln(cond, w_ada, b_ada):
    tn = 1536
    n = 6 * D_MODEL
    out = pl.pallas_call(
        _adaln_kernel,
        out_shape=jax.ShapeDtypeStruct((DEPTH, N_COND, n), F32),
        grid=(DEPTH, n // tn),
        in_specs=[
            pl.BlockSpec((N_COND, D_MODEL), lambda l, j: (0, 0)),
            pl.BlockSpec((None, D_MODEL, tn), lambda l, j: (l, 0, j)),
            pl.BlockSpec((None, 1, tn), lambda l, j: (l, 0, j)),
        ],
        out_specs=pl.BlockSpec((None, N_COND, tn), lambda l, j: (l, 0, j)),
        compiler_params=_params(("arbitrary", "arbitrary")),
        name="adaln",
    )(cond, w_ada, b_ada.reshape(DEPTH, 1, n))
    return out.reshape(DEPTH, N_COND, 6, D_MODEL)


def _cast_once(w_ref, w_scr):
    @pl.when(pl.program_id(0) == 0)
    def _():
        w_scr[...] = w_ref[...].astype(BF16)


def _inproj_a_kernel(xp_ref, xs_ref, mod_ref, w_ref, qg_ref, kg_ref, cos_ref, sin_ref,
                     q_ref, k_ref, vt_ref, kf_ref, vf_ref, u_ref, w_scr, *, npt):
    i = pl.program_id(0)
    _cast_once(w_ref, w_scr)
    x = jnp.where(i < npt, xp_ref[...], xs_ref[...])
    mod = mod_ref[...]
    h = (x * (1.0 + mod[1:2]) + mod[0:1]).astype(BF16)
    proj = jnp.dot(h, w_scr[...], preferred_element_type=F32)
    tm = proj.shape[0]

    ri = lax.broadcasted_iota(jnp.int32, (LANES, LANES), 0) // HEAD_DIM
    ci = lax.broadcasted_iota(jnp.int32, (LANES, LANES), 1) // HEAD_DIM
    seg = jnp.where(ri == ci, 1.0, 0.0).astype(BF16)
    lane = lax.broadcasted_iota(jnp.int32, (tm, LANES), 1)
    even = (lane % 2) == 0
    cos = cos_ref[...]
    sin = sin_ref[...]

    def norm(xc, gain):
        ss = jnp.dot((xc * xc).astype(BF16), seg, preferred_element_type=F32)
        return xc * lax.rsqrt(ss * (1.0 / HEAD_DIM) + EPS) * gain

    def rope(xn):
        partner = jnp.where(even, pltpu.roll(xn, LANES - 1, 1), pltpu.roll(xn, 1, 1))
        return xn * cos + partner * sin

    qg = qg_ref[...]
    for c in range(ATT_W // LANES):
        qr = rope(norm(proj[:, c * LANES:(c + 1) * LANES], qg))
        qs = (qr * (HEAD_DIM ** -0.5 * LOG2E)).astype(BF16)
        q_ref[2 * c] = qs[:, :HEAD_DIM]
        q_ref[2 * c + 1] = qs[:, HEAD_DIM:]

    kn = norm(proj[:, ATT_W:ATT_W + KV_W], kg_ref[...])
    kf_ref[...] = kn
    kr = rope(kn).astype(BF16)
    k_ref[0] = kr[:, :HEAD_DIM]
    k_ref[1] = kr[:, HEAD_DIM:]

    v = proj[:, ATT_W + KV_W:ATT_W + 2 * KV_W]
    vf_ref[...] = v
    vt_ref[...] = v.T.astype(BF16)

    off = ATT_W + 2 * KV_W
    a = proj[:, off:off + CONV_CH]
    gt = proj[:, off + CONV_CH:off + 2 * CONV_CH]
    u_ref[...] = a * _sigmoid(gt)


def _rope_tables(tm):
    t = jnp.arange(DEC_SEQ)
    row = (t // GRID_W).astype(F32)
    col = (t % GRID_W).astype(F32)
    freqs = ROPE_THETA ** (-jnp.arange(0, ROPE_AXIS_DIM, 2, dtype=F32) / ROPE_AXIS_DIM)
    ang = jnp.concatenate([row[:, None] * freqs, col[:, None] * freqs], axis=-1)
    pair = (jnp.arange(LANES) % HEAD_DIM) // 2
    sign = jnp.where(jnp.arange(LANES) % 2 == 0, -1.0, 1.0).astype(F32)
    cos = jnp.cos(ang)[:, pair]
    sin = jnp.sin(ang)[:, pair] * sign
    cos = jnp.concatenate([jnp.ones((tm, LANES), F32), cos], axis=0)
    sin = jnp.concatenate([jnp.zeros((tm, LANES), F32), sin], axis=0)
    return cos, sin


def _inproj_a(xp, xs, mod, w, q_gain, k_gain):
    tm = TM_PROJ
    npt = NP // tm
    tps = DEC_SEQ // tm
    nt = NT // tm
    cos, sin = _rope_tables(tm)
    qg = jnp.tile(q_gain, LANES // HEAD_DIM).reshape(1, LANES)
    kg = jnp.tile(k_gain, LANES // HEAD_DIM).reshape(1, LANES)

    def rope_idx(i):
        return (jnp.where(i < npt, 0, 1 + (i - npt) % tps), 0)

    return pl.pallas_call(
        functools.partial(_inproj_a_kernel, npt=npt),
        out_shape=[
            jax.ShapeDtypeStruct((ATT_HEADS, NT, HEAD_DIM), BF16),
            jax.ShapeDtypeStruct((ATT_KV_HEADS, NT, HEAD_DIM), BF16),
            jax.ShapeDtypeStruct((nt, KV_W, tm), BF16),
            jax.ShapeDtypeStruct((NT, KV_W), F32),
            jax.ShapeDtypeStruct((NT, KV_W), F32),
            jax.ShapeDtypeStruct((NT, CONV_CH), F32),
        ],
        grid=(nt,),
        in_specs=[
            pl.BlockSpec((tm, D_MODEL), lambda i: (jnp.minimum(i, npt - 1), 0)),
            pl.BlockSpec((tm, D_MODEL), lambda i: (jnp.maximum(i - npt, 0), 0)),
            pl.BlockSpec((None, 6, D_MODEL), lambda i: (_cond_index(i, tm), 0, 0)),
            pl.BlockSpec((None, D_MODEL, EVEN_IN), lambda i: (0, 0, 0)),
            pl.BlockSpec((1, LANES), lambda i: (0, 0)),
            pl.BlockSpec((1, LANES), lambda i: (0, 0)),
            pl.BlockSpec((tm, LANES), rope_idx),
            pl.BlockSpec((tm, LANES), rope_idx),
        ],
        out_specs=[
            pl.BlockSpec((ATT_HEADS, tm, HEAD_DIM), lambda i: (0, i, 0)),
            pl.BlockSpec((ATT_KV_HEADS, tm, HEAD_DIM), lambda i: (0, i, 0)),
            pl.BlockSpec((None, KV_W, tm), lambda i: (i, 0, 0)),
            pl.BlockSpec((tm, KV_W), lambda i: (i, 0)),
            pl.BlockSpec((tm, KV_W), lambda i: (i, 0)),
            pl.BlockSpec((tm, CONV_CH), lambda i: (i, 0)),
        ],
        scratch_shapes=[pltpu.VMEM((D_MODEL, EVEN_IN), BF16)],
        compiler_params=_params(("arbitrary",)),
        name="inproj_a",
    )(xp, xs, mod, w, qg, kg, cos, sin)


def _attn_step(qh, kc, vta, m, acc):
    s = lax.dot_general(kc, qh, (((1,), (1,)), ((), ())), preferred_element_type=F32)
    m_new = jnp.maximum(m, jnp.max(s, axis=0, keepdims=True))
    alpha = jnp.exp2(m - m_new)
    p = jnp.exp2(s - m_new).astype(BF16)
    acc = alpha * acc + jnp.dot(vta, p, preferred_element_type=F32)
    return m_new, acc


def _with_ones(vtc):
    return jnp.concatenate([vtc, jnp.ones((ONES_ROWS, vtc.shape[1]), BF16)], axis=0)


def _attn_chunk(q_ref, kc, vtc, carry):
    vta = _with_ones(vtc)
    hg = ATT_GROUP // ATT_SPLIT
    tq = q_ref.shape[1]
    out = []
    for g in range(ATT_SPLIT):
        qg = q_ref[g * hg:(g + 1) * hg].reshape(hg * tq, HEAD_DIM)
        out.append(_attn_step(qg, kc, vta, *carry[g]))
    return tuple(out)


def _attn_init(tq):
    nq = (ATT_GROUP // ATT_SPLIT) * tq
    one = (jnp.full((1, nq), -1e30, F32), jnp.zeros((HEAD_DIM + ONES_ROWS, nq), F32))
    return (one,) * ATT_SPLIT


def _attn_finish(carry, o_ref):
    tq = o_ref.shape[0]
    outs = []
    for _, acc in carry:
        o = acc[:HEAD_DIM] * (1.0 / acc[HEAD_DIM:HEAD_DIM + 1])
        outs += [o[:, h * tq:(h + 1) * tq] for h in range(ATT_GROUP // ATT_SPLIT)]
    o_ref[...] = jnp.concatenate(outs, axis=0).T.astype(BF16)


def _attn_prompt_kernel(q_ref, k_ref, vt_ref, o_ref):
    carry = _attn_chunk(q_ref, k_ref[...], vt_ref[...], _attn_init(q_ref.shape[1]))
    _attn_finish(carry, o_ref)


def _attn_sample_kernel(q_ref, k_ref, vt_ref, ck_ref, cv_ref, o_ref, *, nchunks):
    def body(j, carry):
        r0 = pl.multiple_of(j * TK_ATT, TK_ATT)
        return _attn_chunk(q_ref, k_ref[pl.ds(r0, TK_ATT), :], vt_ref[j], carry)

    carry = lax.fori_loop(0, nchunks, body, _attn_init(q_ref.shape[1]), unroll=ATT_UNROLL)
    first = pl.program_id(1) == 0
    ck = ck_ref[...]
    cv = cv_ref[...]
    ck = jnp.where(first, ck[:, :HEAD_DIM], ck[:, HEAD_DIM:]).astype(BF16)
    cvt = jnp.where(first, cv[:, :HEAD_DIM], cv[:, HEAD_DIM:]).T.astype(BF16)
    carry = _attn_chunk(q_ref, ck, cvt, carry)
    _attn_finish(carry, o_ref)


def _attention_prompt(q, k, vt):
    return pl.pallas_call(
        _attn_prompt_kernel,
        out_shape=jax.ShapeDtypeStruct((NP, ATT_W), BF16),
        grid=(BATCH, ATT_KV_HEADS),
        in_specs=[
            pl.BlockSpec((ATT_GROUP, SEQ, HEAD_DIM), lambda b, g: (g, b, 0)),
            pl.BlockSpec((None, SEQ, HEAD_DIM), lambda b, g: (g, b, 0)),
            pl.BlockSpec((None, HEAD_DIM, SEQ),
                         lambda b, g: (b // (TM_PROJ // SEQ), g, b % (TM_PROJ // SEQ))),
        ],
        out_specs=pl.BlockSpec((SEQ, ATT_GROUP * HEAD_DIM), lambda b, g: (b, g)),
        compiler_params=_params(("arbitrary", "arbitrary")),
        name="attn_prompt",
    )(q, k, vt)


def _attention_sample(q, k, vt, ctx_k, ctx_v):
    tq = TQ_ATT
    nchunks = DEC_SEQ // TK_ATT
    q_off = NP // tq
    kv_off = NP // DEC_SEQ
    return pl.pallas_call(
        functools.partial(_attn_sample_kernel, nchunks=nchunks),
        out_shape=jax.ShapeDtypeStruct((NS, ATT_W), BF16),
        grid=(DEC_BATCH, ATT_KV_HEADS, DEC_SEQ // tq),
        in_specs=[
            pl.BlockSpec((ATT_GROUP, tq, HEAD_DIM),
                         lambda b, g, i: (g, q_off + b * (DEC_SEQ // tq) + i, 0)),
            pl.BlockSpec((None, DEC_SEQ, HEAD_DIM), lambda b, g, i: (g, kv_off + b, 0)),
            pl.BlockSpec((nchunks, HEAD_DIM, TK_ATT), lambda b, g, i: (kv_off + b, g, 0)),
            pl.BlockSpec((None, PAST_LEN, KV_W), lambda b, g, i: (b, 0, 0)),
            pl.BlockSpec((None, PAST_LEN, KV_W), lambda b, g, i: (b, 0, 0)),
        ],
        out_specs=pl.BlockSpec((tq, ATT_GROUP * HEAD_DIM),
                               lambda b, g, i: (b * (DEC_SEQ // tq) + i, g)),
        compiler_params=_params(("arbitrary", "arbitrary", "arbitrary")),
        name="attn_sample",
    )(q, k, vt, ctx_k, ctx_v)


def _conv_kernel(prev_ref, cur_ref, next_ref, w_ref, b_ref, g_ref, bb_ref, o_ref, win_ref, y_ref,
                 *, npt, tps):
    r = cur_ref.shape[0]
    i = pl.program_id(0)
    j = (i - npt) % tps
    first = jnp.logical_or(i < npt, j == 0)
    last = jnp.logical_or(i < npt, j == tps - 1)
    win_ref[0:HALO, :] = jnp.where(first, 0.0, prev_ref[...])
    win_ref[HALO:HALO + r, :] = cur_ref[...]
    win_ref[HALO + r:2 * HALO + r, :] = jnp.where(last, 0.0, next_ref[...])
    base = HALO - CONV_K // 2
    sub = 8
    for c in range(CONV_CH // LANES):
        cs = slice(c * LANES, (c + 1) * LANES)
        acc = None
        for res in range(sub):
            p = None
            for k in range(CONV_K):
                if (base + k) % sub != res:
                    continue
                a = (base + k) - res
                term = win_ref[a:a + r + sub, cs] * w_ref[k:k + 1, cs]
                p = term if p is None else p + term
            if p is None:
                continue
            p = p[res:res + r]
            acc = p if acc is None else acc + p
        y_ref[:, cs] = acc + b_ref[:, cs]
    y = _layernorm(y_ref[...], g_ref[...], bb_ref[...])
    o_ref[...] = (y * _sigmoid(y)).astype(BF16)


def _conv_module(u, conv_w, conv_b, cln_g, cln_b):
    r = R_CONV
    npt = NP // r
    tps = DEC_SEQ // r
    hb = r // HALO
    nh = NT // HALO
    w = jnp.concatenate([conv_w, jnp.zeros((1, CONV_CH), F32)], axis=0)
    return pl.pallas_call(
        functools.partial(_conv_kernel, npt=npt, tps=tps),
        out_shape=jax.ShapeDtypeStruct((NT, CONV_CH), BF16),
        grid=(NT // r,),
        in_specs=[
            pl.BlockSpec((HALO, CONV_CH), lambda i: (jnp.maximum(i * hb - 1, 0), 0)),
            pl.BlockSpec((r, CONV_CH), lambda i: (i, 0)),
            pl.BlockSpec((HALO, CONV_CH), lambda i: (jnp.minimum((i + 1) * hb, nh - 1), 0)),
            pl.BlockSpec((CONV_K + 1, CONV_CH), lambda i: (0, 0)),
            pl.BlockSpec((1, CONV_CH), lambda i: (0, 0)),
            pl.BlockSpec((1, CONV_CH), lambda i: (0, 0)),
            pl.BlockSpec((1, CONV_CH), lambda i: (0, 0)),
        ],
        out_specs=pl.BlockSpec((r, CONV_CH), lambda i: (i, 0)),
        scratch_shapes=[pltpu.VMEM((r + 2 * HALO, CONV_CH), F32), pltpu.VMEM((r, CONV_CH), F32)],
        compiler_params=_params(("arbitrary",)),
        name="conv_module",
    )(u, u, u, w, conv_b.reshape(1, CONV_CH), cln_g.reshape(1, CONV_CH), cln_b.reshape(1, CONV_CH))


def _outproj_a_kernel(ap_ref, as_ref, uc_ref, xp_ref, xs_ref, mod_ref, w_ref, g_ref, b_ref, o_ref,
                      w_scr, *, npt):
    i = pl.program_id(0)
    _cast_once(w_ref, w_scr)
    att = jnp.where(i < npt, ap_ref[...], as_ref[...])
    x = jnp.where(i < npt, xp_ref[...], xs_ref[...])
    y = jnp.dot(att, w_scr[0:ATT_W, :], preferred_element_type=F32)
    y = y + jnp.dot(uc_ref[...], w_scr[ATT_W:ATT_W + CONV_CH, :], preferred_element_type=F32)
    mod = mod_ref[...]
    o_ref[...] = _layernorm(ALPHA * x + mod[2:3] * y, g_ref[...], b_ref[...])


def _outproj_a(att_p, att_s, uc, xp, xs, mod, w, g, b):
    tm = TM_PROJ
    npt = NP // tm
    first = lambda i: (jnp.minimum(i, npt - 1), 0)
    second = lambda i: (jnp.maximum(i - npt, 0), 0)
    return pl.pallas_call(
        functools.partial(_outproj_a_kernel, npt=npt),
        out_shape=jax.ShapeDtypeStruct((NT, D_MODEL), F32),
        grid=(NT // tm,),
        in_specs=[
            pl.BlockSpec((tm, ATT_W), first),
            pl.BlockSpec((tm, ATT_W), second),
            pl.BlockSpec((tm, CONV_CH), lambda i: (i, 0)),
            pl.BlockSpec((tm, D_MODEL), first),
            pl.BlockSpec((tm, D_MODEL), second),
            pl.BlockSpec((None, 6, D_MODEL), lambda i: (_cond_index(i, tm), 0, 0)),
            pl.BlockSpec((None, ATT_W + CONV_CH, D_MODEL), lambda i: (0, 0, 0)),
            pl.BlockSpec((1, D_MODEL), lambda i: (0, 0)),
            pl.BlockSpec((1, D_MODEL), lambda i: (0, 0)),
        ],
        out_specs=pl.BlockSpec((tm, D_MODEL), lambda i: (i, 0)),
        scratch_shapes=[pltpu.VMEM((ATT_W + CONV_CH, D_MODEL), BF16)],
        compiler_params=_params(("arbitrary",)),
        name="outproj_a",
    )(att_p, att_s, uc, xp, xs, mod, w, g, b)


def _outproj_m_kernel(hp_ref, hs_ref, x_ref, mod_ref, w_ref, g_ref, b_ref, o_ref, w_scr, *, npt):
    i = pl.program_id(0)
    _cast_once(w_ref, w_scr)
    hg = jnp.where(i < npt, hp_ref[...], hs_ref[...])
    y = jnp.dot(hg, w_scr[...], preferred_element_type=F32)
    mod = mod_ref[...]
    o_ref[...] = _layernorm(ALPHA * x_ref[...] + mod[2:3] * y, g_ref[...], b_ref[...])


def _outproj_m(hg_p, hg_s, x, mod, w, g, b):
    tm = TM_PROJ
    npt = NP // tm
    return pl.pallas_call(
        functools.partial(_outproj_m_kernel, npt=npt),
        out_shape=jax.ShapeDtypeStruct((NT, D_MODEL), F32),
        grid=(NT // tm,),
        in_specs=[
            pl.BlockSpec((tm, M_INNER), lambda i: (jnp.minimum(i, npt - 1), 0)),
            pl.BlockSpec((tm, M_INNER), lambda i: (jnp.maximum(i - npt, 0), 0)),
            pl.BlockSpec((tm, D_MODEL), lambda i: (i, 0)),
            pl.BlockSpec((None, 6, D_MODEL), lambda i: (_cond_index(i, tm), 0, 0)),
            pl.BlockSpec((None, M_INNER, D_MODEL), lambda i: (0, 0, 0)),
            pl.BlockSpec((1, D_MODEL), lambda i: (0, 0)),
            pl.BlockSpec((1, D_MODEL), lambda i: (0, 0)),
        ],
        out_specs=pl.BlockSpec((tm, D_MODEL), lambda i: (i, 0)),
        scratch_shapes=[pltpu.VMEM((M_INNER, D_MODEL), BF16)],
        compiler_params=_params(("arbitrary",)),
        name="outproj_m",
    )(hg_p, hg_s, x, mod, w, g, b)


def _ffn_kernel(x_ref, mod_ref, w1_ref, w2_ref, g_ref, b_ref, o_ref, h_scr, acc_scr):
    j = pl.program_id(1)

    @pl.when(j == 0)
    def _():
        mod = mod_ref[...]
        h_scr[...] = (x_ref[...] * (1.0 + mod[4:5]) + mod[3:4]).astype(BF16)
        acc_scr[...] = jnp.zeros_like(acc_scr)

    a = jnp.dot(h_scr[...], w1_ref[...].astype(BF16), preferred_element_type=F32)
    a = jnp.maximum(a, 0.0)
    acc_scr[...] += jnp.dot((a * a).astype(BF16), w2_ref[...].astype(BF16), preferred_element_type=F32)

    @pl.when(j == pl.num_programs(1) - 1)
    def _():
        mod = mod_ref[...]
        o_ref[...] = _layernorm(ALPHA * x_ref[...] + mod[5:6] * acc_scr[...], g_ref[...], b_ref[...])


def _ffn(x, row0, nrows, mod, w1, w2, layer, g, b):
    tm, tf = TM_FFN, TF_FFN
    off = row0 // tm
    return pl.pallas_call(
        _ffn_kernel,
        out_shape=jax.ShapeDtypeStruct((nrows, D_MODEL), F32),
        grid=(nrows // tm, D_FF // tf),
        in_specs=[
            pl.BlockSpec((tm, D_MODEL), lambda i, j: (i + off, 0)),
            pl.BlockSpec((None, 6, D_MODEL), lambda i, j: (_cond_index(i + off, tm), 0, 0)),
            pl.BlockSpec((None, D_MODEL, tf), lambda i, j: (layer, 0, j)),
            pl.BlockSpec((None, tf, D_MODEL), lambda i, j: (layer, j, 0)),
            pl.BlockSpec((1, D_MODEL), lambda i, j: (0, 0)),
            pl.BlockSpec((1, D_MODEL), lambda i, j: (0, 0)),
        ],
        out_specs=pl.BlockSpec((tm, D_MODEL), lambda i, j: (i, 0)),
        scratch_shapes=[pltpu.VMEM((tm, D_MODEL), BF16), pltpu.VMEM((tm, D_MODEL), F32)],
        compiler_params=_params(("arbitrary", "arbitrary")),
        name="ffn",
    )(x, mod, w1, w2, g, b)


def _inproj_m_kernel(x_ref, mod_ref, w_ref, wg_ref, bg_ref, o_ref, gt_ref, vt_ref, h_scr):
    j = pl.program_id(1)
    nslab = vt_ref.shape[0]
    L = L_CHUNK

    @pl.when(j == 0)
    def _():
        mod = mod_ref[...]
        h = (x_ref[...] * (1.0 + mod[1:2]) + mod[0:1]).astype(BF16)
        h_scr[...] = h
        gates = jnp.dot(h, wg_ref[...], preferred_element_type=F32) + bg_ref[...]
        kind = lax.broadcasted_iota(jnp.int32, gates.shape, 1) % GATE_STRIDE
        gates = jnp.where((kind == 2) | (kind == 3), _log_sigmoid(gates), gates)
        gates_t = gates.T
        for s in range(nslab):
            gt_ref[s] = gates_t[0:GATE_ROWS, s * L:(s + 1) * L]

    p = jnp.dot(h_scr[...], w_ref[...].astype(BF16), preferred_element_type=F32)
    p = p * jnp.where(j == 1, M_DK ** -0.5, 1.0)
    o_ref[...] = p.astype(BF16)

    @pl.when(j == 2)
    def _():
        v_t = p.T.astype(BF16)
        for s in range(nslab):
            vt_ref[s] = v_t[:, s * L:(s + 1) * L]


def _inproj_m(x, mod, w, wg, bg):
    tm = TM_FFN
    nslab = tm // L_CHUNK
    return pl.pallas_call(
        _inproj_m_kernel,
        out_shape=[
            jax.ShapeDtypeStruct((NT, 4 * M_INNER), BF16),
            jax.ShapeDtypeStruct((NT // L_CHUNK, GATE_ROWS, L_CHUNK), F32),
            jax.ShapeDtypeStruct((NT // L_CHUNK, M_INNER, L_CHUNK), BF16),
        ],
        grid=(NT // tm, 4),
        in_specs=[
            pl.BlockSpec((tm, D_MODEL), lambda i, j: (i, 0)),
            pl.BlockSpec((None, 6, D_MODEL), lambda i, j: (_cond_index(i, tm), 0, 0)),
            pl.BlockSpec((None, D_MODEL, M_INNER), lambda i, j: (0, 0, j)),
            pl.BlockSpec((D_MODEL, LANES), lambda i, j: (0, 0)),
            pl.BlockSpec((1, LANES), lambda i, j: (0, 0)),
        ],
        out_specs=[
            pl.BlockSpec((tm, M_INNER), lambda i, j: (i, j)),
            pl.BlockSpec((nslab, GATE_ROWS, L_CHUNK), lambda i, j: (i, 0, 0)),
            pl.BlockSpec((nslab, M_INNER, L_CHUNK), lambda i, j: (i, 0, 0)),
        ],
        scratch_shapes=[pltpu.VMEM((tm, D_MODEL), BF16)],
        compiler_params=_params(("arbitrary", "arbitrary")),
        name="inproj_m",
    )(x, mod, w, wg, bg)


def _log_sigmoid(x):
    return jnp.minimum(x, 0.0) - jnp.log1p(jnp.exp(-jnp.abs(x)))


def _split3(x):
    x1 = x.astype(BF16)
    r1 = x - x1.astype(F32)
    x2 = r1.astype(BF16)
    x3 = (r1 - x2.astype(F32)).astype(BF16)
    return x1, x2, x3


def _fold_lanes(x, op):
    acc = x[:, :LANES]
    for c in range(1, x.shape[1] // LANES):
        acc = op(acc, x[:, c * LANES:(c + 1) * LANES])
    return acc


def _row_sum(x):
    return jnp.sum(_fold_lanes(x, jnp.add), axis=1, keepdims=True)


def _row_max(x):
    return jnp.max(_fold_lanes(x, jnp.maximum), axis=1, keepdims=True)


def _lane_scan(x, op, fill, reverse):
    n = x.shape[1]
    lane = lax.broadcasted_iota(jnp.int32, x.shape, 1)
    sh = 1
    while sh < n:
        if reverse:
            x = op(x, jnp.where(lane < n - sh, pltpu.roll(x, n - sh, 1), fill))
        else:
            x = op(x, jnp.where(lane >= sh, pltpu.roll(x, sh, 1), fill))
        sh *= 2
    return x


def _mlstm_kernel(*refs, nc, nseq, has_init, emit_state):
    q_ref, k_ref, vt_ref, og_ref, gt_ref, gain_ref = refs[:6]
    pos = 6
    if has_init:
        c0_ref, n0_ref, m0_ref = refs[pos:pos + 3]
        pos += 3
    out_ref = refs[pos]
    pos += 1
    if emit_state:
        co_ref, no_ref, mo_ref = refs[pos:pos + 3]
        pos += 3
    hs_ref, ct_scr, n_scr, m_scr = refs[pos:pos + 4]

    L = L_CHUNK
    head = pl.program_id(1)
    grow = pl.multiple_of(head * GATE_STRIDE, GATE_STRIDE)
    zero_state = (not has_init) and nc == 1

    hs_ref[...] = jnp.zeros_like(hs_ref)
    if has_init:
        for d in range(2):
            ct_scr[d] = c0_ref[d].T
        n_scr[...] = n0_ref[...]
        m_scr[...] = jnp.broadcast_to(m0_ref[...], m_scr.shape)
    elif zero_state:
        pass
    else:
        ct_scr[...] = jnp.zeros_like(ct_scr)
        n_scr[...] = jnp.zeros_like(n_scr)
        m_scr[...] = jnp.zeros_like(m_scr)

    tpos = lax.broadcasted_iota(jnp.int32, (L, L), 0)
    spos = lax.broadcasted_iota(jnp.int32, (L, L), 1)

    def prep(j):
        rows = pl.ds(j * L, L) if isinstance(j, int) else pl.ds(pl.multiple_of(j * L, L), L)
        qc = q_ref[rows, :]
        kc = k_ref[rows, :]
        vtc = vt_ref[j]
        ght = gt_ref[j, pl.ds(grow, GATE_STRIDE), :]
        parts = jnp.concatenate([t.astype(F32) for t in _split3(ght)]
                                + [jnp.zeros((GATE_STRIDE, L), F32)], axis=0).astype(BF16)
        upper = jnp.where(tpos <= spos, 1.0, 0.0).astype(BF16)
        r = jnp.dot(parts, upper, preferred_element_type=F32)
        pre = r[0:GATE_STRIDE] + r[GATE_STRIDE:2 * GATE_STRIDE] + r[2 * GATE_STRIDE:3 * GATE_STRIDE]
        s = lax.dot_general(kc, qc, (((1,), (1,)), ((), ())), preferred_element_type=F32)
        return qc, kc, vtc, ght, pre, s

    def chunk(d, j, sd, prepared):
        qc, kc, vtc, ght, pre, s = prepared
        rev = d == 1
        keep = (tpos >= spos) if rev else (tpos <= spos)
        if rev:
            tot = pre[3:4, L - 1:L]
            a_row = tot - pre[3:4, :] + ght[3:4, :]
        else:
            a_row = pre[2:3, :]
            tot = a_row[:, L - 1:L]
        b_row = ght[d:d + 1, :] - a_row
        m = m_scr[sd][:, 0:1] if not zero_state else jnp.zeros((1, 1), F32)

        b_col = jnp.broadcast_to(b_row, (8, L)).T[:, 0:1]
        bm = jnp.where(keep, b_col, -jnp.inf)
        mm = jnp.maximum(m, jnp.max(bm, axis=0, keepdims=True))
        w = jnp.exp(bm - mm)
        qk = s * w
        num = jnp.dot(vtc, qk.astype(BF16), preferred_element_type=F32)
        den = jnp.sum(qk, axis=0, keepdims=True)
        if not zero_state:
            ct = ct_scr[sd]
            n = n_scr[sd]
            nrows = jnp.concatenate([t.astype(F32) for t in _split3(n)]
                                    + [jnp.zeros((ONES_ROWS - 3, M_DK), F32)], axis=0)
            cn = jnp.concatenate([ct, nrows], axis=0).astype(BF16)
            inter = lax.dot_general(cn, qc, (((1,), (1,)), ((), ())), preferred_element_type=F32)
            s_inter = jnp.exp(m - mm)
            num = num + s_inter * inter[:M_DV]
            qn = inter[M_DV:M_DV + 1] + inter[M_DV + 1:M_DV + 2] + inter[M_DV + 2:M_DV + 3]
            den = den + s_inter * qn
        h = num * (1.0 / jnp.maximum(jnp.abs(den), jnp.exp(-(a_row + mm))))
        hs_ref[j] = hs_ref[j] + h

        g_row = tot + b_row
        m_new = jnp.maximum(tot + m, jnp.max(g_row, axis=1, keepdims=True))
        ws = jnp.exp(g_row - m_new)
        wrows = jnp.concatenate([t.astype(F32) for t in _split3(ws)]
                                + [jnp.zeros((ONES_ROWS - 3, L), F32)], axis=0).astype(BF16)
        lhs = jnp.concatenate([(vtc.astype(F32) * ws).astype(BF16), wrows], axis=0)
        upd_all = jnp.dot(lhs, kc, preferred_element_type=F32)
        upd = upd_all[:M_DV]
        nsum = upd_all[M_DV:M_DV + 1] + upd_all[M_DV + 1:M_DV + 2] + upd_all[M_DV + 2:M_DV + 3]
        if zero_state:
            ct_scr[sd] = upd
            n_scr[sd] = nsum
        else:
            decay = jnp.exp(tot + m - m_new)
            ct_scr[sd] = decay * ct + upd
            n_scr[sd] = decay * n + nsum
        m_scr[sd] = jnp.broadcast_to(m_new, (1, LANES))

    if nc == 1:
        for bb in range(nseq):
            prepared = prep(bb)
            chunk(0, bb, 2 * bb, prepared)
            chunk(1, bb, 2 * bb + 1, prepared)
    else:
        def body(j, carry):
            chunk(0, j, 0, prep(j))
            chunk(1, nc - 1 - j, 1, prep(nc - 1 - j))
            return carry
        lax.fori_loop(0, nc, body, 0, unroll=MLSTM_UNROLL)

    gain = gain_ref[...]

    def finish(j, carry):
        rows = pl.ds(j * L, L) if isinstance(j, int) else pl.ds(pl.multiple_of(j * L, L), L)
        hsum = hs_ref[j].T
        hn = hsum * lax.rsqrt(_row_sum(hsum * hsum) * (1.0 / M_DV) + EPS) * gain
        out_ref[rows, :] = (_sigmoid(og_ref[rows, :].astype(F32)) * hn).astype(BF16)
        return carry

    if nc == 1:
        for bb in range(nseq):
            finish(bb, 0)
    else:
        lax.fori_loop(0, nc, finish, 0)

    if emit_state:
        for bb in range(nseq):
            for d in range(2):
                co_ref[bb, d] = ct_scr[2 * bb + d].T
                no_ref[bb, d] = n_scr[2 * bb + d]
                mo_ref[bb, d] = m_scr[2 * bb + d][:, 0:1]


def _mlstm(qkvo, vt, gates_t, gain, nb, seq, nseq, row_blk0, init=None, emit_state=False):
    assert nseq == 1 or seq == L_CHUNK
    s = nseq * seq
    nc = seq // L_CHUNK
    nslab = s // L_CHUNK
    hq = M_INNER // M_DK
    in_specs = [
        pl.BlockSpec((s, M_DK), lambda b, h: (row_blk0 + b, h)),
        pl.BlockSpec((s, M_DK), lambda b, h: (row_blk0 + b, hq + h)),
        pl.BlockSpec((nslab, M_DV, L_CHUNK), lambda b, h: (row_blk0 + b, h, 0)),
        pl.BlockSpec((s, M_DV), lambda b, h: (row_blk0 + b, 3 * hq + h)),
        pl.BlockSpec((nslab, GATE_ROWS, L_CHUNK), lambda b, h: (row_blk0 + b, 0, 0)),
        pl.BlockSpec((1, M_DV), lambda b, h: (0, h)),
    ]
    args = [qkvo, qkvo, vt, qkvo, gates_t, gain]
    state_specs = [
        pl.BlockSpec((nseq, 2, None, M_DK, M_DV), lambda b, h: (b, 0, h, 0, 0)),
        pl.BlockSpec((nseq, 2, None, 1, M_DK), lambda b, h: (b, 0, h, 0, 0)),
        pl.BlockSpec((nseq, 2, None, 1, 1), lambda b, h: (b, 0, h, 0, 0)),
    ]
    if init is not None:
        assert nseq == 1
        in_specs += [
            pl.BlockSpec((None, 2, None, M_DK, M_DV), lambda b, h: (b, 0, h, 0, 0)),
            pl.BlockSpec((None, 2, None, 1, M_DK), lambda b, h: (b, 0, h, 0, 0)),
            pl.BlockSpec((None, 2, None, 1, 1), lambda b, h: (b, 0, h, 0, 0)),
        ]
        args += list(init)
    out_shape = [jax.ShapeDtypeStruct((nb * seq, M_INNER), BF16)]
    out_specs = [pl.BlockSpec((s, M_DV), lambda b, h: (b, h))]
    if emit_state:
        out_shape += [
            jax.ShapeDtypeStruct((nb, 2, M_HEADS, M_DK, M_DV), F32),
            jax.ShapeDtypeStruct((nb, 2, M_HEADS, 1, M_DK), F32),
            jax.ShapeDtypeStruct((nb, 2, M_HEADS, 1, 1), F32),
        ]
        out_specs += state_specs
    return pl.pallas_call(
        functools.partial(_mlstm_kernel, nc=nc, nseq=nseq, has_init=init is not None,
                          emit_state=emit_state),
        out_shape=out_shape,
        grid=(nb // nseq, M_HEADS),
        in_specs=in_specs,
        out_specs=out_specs,
        scratch_shapes=[
            pltpu.VMEM((nslab, M_DV, L_CHUNK), F32),
            pltpu.VMEM((2 * nseq, M_DV, M_DK), F32),
            pltpu.VMEM((2 * nseq, 1, M_DK), F32),
            pltpu.VMEM((2 * nseq, 1, LANES), F32),
        ],
        compiler_params=_params(("arbitrary", "arbitrary")),
        name="mlstm_state" if emit_state else "mlstm",
    )(*args)


@jax.jit
def kernel(x_prompt, x_sample, cache_k, cache_v, state_c, state_n, state_m, c, c_ctx, w_ada, b_ada,
           ln_g, ln_b, w_ff1, w_ff2, w_in_a, q_gain, k_gain, conv_w, conv_b, conv_ln_g, conv_ln_b,
           w_out_a, w_in_m, b_gate_m, mh_gain, w_out_m):
    xp = x_prompt.reshape(NP, D_MODEL)
    xs = x_sample.reshape(NS, D_MODEL)

    cond = jnp.concatenate(
        [c_ctx[None, :], c, jnp.zeros((N_COND - 1 - DEC_BATCH, D_MODEL), F32)], axis=0)
    mods = _adaln(cond, w_ada, b_ada)

    ln = lambda l, s: (ln_g[l, s].reshape(1, D_MODEL), ln_b[l, s].reshape(1, D_MODEL))

    q, k, vt, kf, vf, u = _inproj_a(xp, xs, mods[0], w_in_a, q_gain[0], k_gain[0])
    att_p = _attention_prompt(q, k, vt)
    ctx_k = cache_k[:, 0].reshape(DEC_BATCH, PAST_LEN, KV_W)
    ctx_v = cache_v[:, 0].reshape(DEC_BATCH, PAST_LEN, KV_W)
    att_s = _attention_sample(q, k, vt, ctx_k, ctx_v)
    uc = _conv_module(u, conv_w[0], conv_b[0], conv_ln_g[0], conv_ln_b[0])
    x1 = _outproj_a(att_p, att_s, uc, xp, xs, mods[0], w_out_a, *ln(0, 0))
    x2 = _ffn(x1, 0, NT, mods[0], w_ff1, w_ff2, 0, *ln(0, 1))

    wm = w_in_m[0]
    src = jnp.array([4 * M_INNER + g * M_HEADS + h
                     for h in range(M_HEADS) for g in (0, 2, 1, 3)], jnp.int32)
    dst = jnp.array([h * GATE_STRIDE + t for h in range(M_HEADS) for t in range(4)], jnp.int32)
    wg = jnp.zeros((D_MODEL, LANES), F32).at[:, dst].set(wm[:, src]).astype(BF16)
    bg = jnp.zeros((1, LANES), F32).at[0, dst].set(b_gate_m[0][src - 4 * M_INNER])
    qkvo, gates_t, vt_m = _inproj_m(x2, mods[1], w_in_m, wg, bg)
    gain = mh_gain[0].reshape(1, M_INNER)
    hg_p, st_c, st_n, st_m = _mlstm(qkvo, vt_m, gates_t, gain, BATCH, SEQ, MLSTM_SEQS, 0, emit_state=True)
    init = (state_c[:, 0], state_n[:, 0].reshape(DEC_BATCH, 2, M_HEADS, 1, M_DK),
            state_m[:, 0].reshape(DEC_BATCH, 2, M_HEADS, 1, 1))
    (hg_s,) = _mlstm(qkvo, vt_m, gates_t, gain, DEC_BATCH, DEC_SEQ, 1, NP // DEC_SEQ, init=init)
    x3 = _outproj_m(hg_p, hg_s, x2, mods[1], w_out_m, *ln(1, 0))
    y_p = _ffn(x3, 0, NP, mods[1], w_ff1, w_ff2, 1, *ln(1, 1))
    y_s = _ffn(x3, NP, NS, mods[1], w_ff1, w_ff2, 1, *ln(1, 1))

    new_k = kf[:NP].reshape(BATCH, 1, SEQ, ATT_KV_HEADS, HEAD_DIM)
    new_v = vf[:NP].reshape(BATCH, 1, SEQ, ATT_KV_HEADS, HEAD_DIM)
    return (y_p.reshape(BATCH, SEQ, D_MODEL), y_s.reshape(DEC_BATCH, DEC_SEQ, D_MODEL),
            new_k, new_v,
            st_c.reshape(BATCH, 1, 2, M_HEADS, M_DK, M_DV),
            st_n.reshape(BATCH, 1, 2, M_HEADS, M_DK),
            st_m.reshape(BATCH, 1, 2, M_HEADS))
```

```python
import functools

import jax
import jax.numpy as jnp
from jax import lax
from jax.experimental import pallas as pl
from jax.experimental.pallas import tpu as pltpu

F32 = jnp.float32
BF16 = jnp.bfloat16

D_MODEL = 1024
BATCH = 32
SEQ = 256
DEPTH = 2
DEC_BATCH = 4
DEC_SEQ = 4096
PAST_LEN = 512
GRID_W = 64
ATT_HEADS = 8
ATT_KV_HEADS = 2
HEAD_DIM = 64
ATT_GROUP = ATT_HEADS // ATT_KV_HEADS
ATT_W = ATT_HEADS * HEAD_DIM
KV_W = ATT_KV_HEADS * HEAD_DIM
ROPE_AXIS_DIM = HEAD_DIM // 2
ROPE_THETA = 10000.0
CONV_CH = D_MODEL // 2
CONV_K = 31
EVEN_IN = ATT_W + 2 * KV_W + 2 * CONV_CH
M_HEADS = 4
M_INNER = D_MODEL
M_DK = M_INNER // M_HEADS
M_DV = M_INNER // M_HEADS
D_FF = 4 * D_MODEL
ALPHA = (2 * DEPTH) ** 0.25
EPS = 1e-6
LOG2E = 1.4426950408889634

NP = BATCH * SEQ
NS = DEC_BATCH * DEC_SEQ
NT = NP + NS
N_COND = 8

LANES = 128
VMEM_LIMIT = 56 * 1024 * 1024

TM_PROJ = 512
TF_FFN = 1024
TQ_ATT = 256
TK_ATT = 512
R_CONV = 256
HALO = 16
L_CHUNK = 256
GATE_STRIDE = 8
ONES_ROWS = 16
GATE_ROWS = 32
ATT_SPLIT = 1
MLSTM_SEQS = 4
ATT_UNROLL = 4


def _cond_index(i_global, tm):
    npt = NP // tm
    tps = DEC_SEQ // tm
    return jnp.where(i_global < npt, 0, 1 + (i_global - npt) // tps)


def _layernorm(r, g, b):
    mu = jnp.mean(r, axis=-1, keepdims=True)
    d = r - mu
    var = jnp.mean(d * d, axis=-1, keepdims=True)
    return d * lax.rsqrt(var + EPS) * g + b


def _sigmoid(x):
    return 1.0 / (1.0 + jnp.exp(-x))


def _params(sem, vmem=VMEM_LIMIT, flags=None):
    return pltpu.CompilerParams(dimension_semantics=sem, vmem_limit_bytes=vmem, flags=flags)


def _adaln_kernel(cond_ref, w_ref, b_ref, o_ref):
    c = cond_ref[...]
    s = (c * _sigmoid(c)).astype(BF16)
    o_ref[...] = jnp.dot(s, w_ref[...].astype(BF16), preferred_element_type=F32) + b_ref[...]


def _adaln(cond, w_ada, b_ada):
    tn = 1536
    n = 6 * D_MODEL
    out = pl.pallas_call(
        _adaln_kernel,
        out_shape=jax.ShapeDtypeStruct((DEPTH, N_COND, n), F32),
        grid=(DEPTH, n // tn),
        in_specs=[
            pl.BlockSpec((N_COND, D_MODEL), lambda l, j: (0, 0)),
            pl.BlockSpec((None, D_MODEL, tn), lambda l, j: (l, 0, j)),
            pl.BlockSpec((None, 1, tn), lambda l, j: (l, 0, j)),
        ],
        out_specs=pl.BlockSpec((None, N_COND, tn), lambda l, j: (l, 0, j)),
        compiler_params=_params(("arbitrary", "arbitrary")),
        name="adaln",
    )(cond, w_ada, b_ada.reshape(DEPTH, 1, n))
    return out.reshape(DEPTH, N_COND, 6, D_MODEL)


def _cast_once(w_ref, w_scr):
    @pl.when(pl.program_id(0) == 0)
    def _():
        w_scr[...] = w_ref[...].astype(BF16)


def _inproj_a_kernel(xp_ref, xs_ref, mod_ref, w_ref, qg_ref, kg_ref, cos_ref, sin_ref,
                     q_ref, k_ref, vt_ref, kf_ref, vf_ref, u_ref, w_scr, *, npt):
    i = pl.program_id(0)
    _cast_once(w_ref, w_scr)
    x = jnp.where(i < npt, xp_ref[...], xs_ref[...])
    mod = mod_ref[...]
    h = (x * (1.0 + mod[1:2]) + mod[0:1]).astype(BF16)
    proj = jnp.dot(h, w_scr[...], preferred_element_type=F32)
    tm = proj.shape[0]

    ri = lax.broadcasted_iota(jnp.int32, (LANES, LANES), 0) // HEAD_DIM
    ci = lax.broadcasted_iota(jnp.int32, (LANES, LANES), 1) // HEAD_DIM
    seg = jnp.where(ri == ci, 1.0, 0.0).astype(BF16)
    lane = lax.broadcasted_iota(jnp.int32, (tm, LANES), 1)
    even = (lane % 2) == 0
    cos = cos_ref[...]
    sin = sin_ref[...]

    def norm(xc, gain):
        ss = jnp.dot((xc * xc).astype(BF16), seg, preferred_element_type=F32)
        return xc * lax.rsqrt(ss * (1.0 / HEAD_DIM) + EPS) * gain

    def rope(xn):
        partner = jnp.where(even, pltpu.roll(xn, LANES - 1, 1), pltpu.roll(xn, 1, 1))
        return xn * cos + partner * sin

    qg = qg_ref[...]
    for c in range(ATT_W // LANES):
        qr = rope(norm(proj[:, c * LANES:(c + 1) * LANES], qg))
        qs = (qr * (HEAD_DIM ** -0.5 * LOG2E)).astype(BF16)
        q_ref[2 * c] = qs[:, :HEAD_DIM]
        q_ref[2 * c + 1] = qs[:, HEAD_DIM:]

    kn = norm(proj[:, ATT_W:ATT_W + KV_W], kg_ref[...])
    kf_ref[...] = kn
    kr = rope(kn).astype(BF16)
    k_ref[0] = kr[:, :HEAD_DIM]
    k_ref[1] = kr[:, HEAD_DIM:]

    v = proj[:, ATT_W + KV_W:ATT_W + 2 * KV_W]
    vf_ref[...] = v
    vt_ref[...] = v.T.astype(BF16)

    off = ATT_W + 2 * KV_W
    a = proj[:, off:off + CONV_CH]
    gt = proj[:, off + CONV_CH:off + 2 * CONV_CH]
    u_ref[...] = a * _sigmoid(gt)


def _rope_tables(tm):
    t = jnp.arange(DEC_SEQ)
    row = (t // GRID_W).astype(F32)
    col = (t % GRID_W).astype(F32)
    freqs = ROPE_THETA ** (-jnp.arange(0, ROPE_AXIS_DIM, 2, dtype=F32) / ROPE_AXIS_DIM)
    ang = jnp.concatenate([row[:, None] * freqs, col[:, None] * freqs], axis=-1)
    pair = (jnp.arange(LANES) % HEAD_DIM) // 2
    sign = jnp.where(jnp.arange(LANES) % 2 == 0, -1.0, 1.0).astype(F32)
    cos = jnp.cos(ang)[:, pair]
    sin = jnp.sin(ang)[:, pair] * sign
    cos = jnp.concatenate([jnp.ones((tm, LANES), F32), cos], axis=0)
    sin = jnp.concatenate([jnp.zeros((tm, LANES), F32), sin], axis=0)
    return cos, sin


def _inproj_a(xp, xs, mod, w, q_gain, k_gain):
    tm = TM_PROJ
    npt = NP // tm
    tps = DEC_SEQ // tm
    nt = NT // tm
    cos, sin = _rope_tables(tm)
    qg = jnp.tile(q_gain, LANES // HEAD_DIM).reshape(1, LANES)
    kg = jnp.tile(k_gain, LANES // HEAD_DIM).reshape(1, LANES)

    def rope_idx(i):
        return (jnp.where(i < npt, 0, 1 + (i - npt) % tps), 0)

    return pl.pallas_call(
        functools.partial(_inproj_a_kernel, npt=npt),
        out_shape=[
            jax.ShapeDtypeStruct((ATT_HEADS, NT, HEAD_DIM), BF16),
            jax.ShapeDtypeStruct((ATT_KV_HEADS, NT, HEAD_DIM), BF16),
            jax.ShapeDtypeStruct((nt, KV_W, tm), BF16),
            jax.ShapeDtypeStruct((NT, KV_W), F32),
            jax.ShapeDtypeStruct((NT, KV_W), F32),
            jax.ShapeDtypeStruct((NT, CONV_CH), F32),
        ],
        grid=(nt,),
        in_specs=[
            pl.BlockSpec((tm, D_MODEL), lambda i: (jnp.minimum(i, npt - 1), 0)),
            pl.BlockSpec((tm, D_MODEL), lambda i: (jnp.maximum(i - npt, 0), 0)),
            pl.BlockSpec((None, 6, D_MODEL), lambda i: (_cond_index(i, tm), 0, 0)),
            pl.BlockSpec((None, D_MODEL, EVEN_IN), lambda i: (0, 0, 0)),
            pl.BlockSpec((1, LANES), lambda i: (0, 0)),
            pl.BlockSpec((1, LANES), lambda i: (0, 0)),
            pl.BlockSpec((tm, LANES), rope_idx),
            pl.BlockSpec((tm, LANES), rope_idx),
        ],
        out_specs=[
            pl.BlockSpec((ATT_HEADS, tm, HEAD_DIM), lambda i: (0, i, 0)),
            pl.BlockSpec((ATT_KV_HEADS, tm, HEAD_DIM), lambda i: (0, i, 0)),
            pl.BlockSpec((None, KV_W, tm), lambda i: (i, 0, 0)),
            pl.BlockSpec((tm, KV_W), lambda i: (i, 0)),
            pl.BlockSpec((tm, KV_W), lambda i: (i, 0)),
            pl.BlockSpec((tm, CONV_CH), lambda i: (i, 0)),
        ],
        scratch_shapes=[pltpu.VMEM((D_MODEL, EVEN_IN), BF16)],
        compiler_params=_params(("arbitrary",)),
        name="inproj_a",
    )(xp, xs, mod, w, qg, kg, cos, sin)


def _attn_step(qh, kc, vta, m, acc):
    s = lax.dot_general(kc, qh, (((1,), (1,)), ((), ())), preferred_element_type=F32)
    m_new = jnp.maximum(m, jnp.max(s, axis=0, keepdims=True))
    alpha = jnp.exp2(m - m_new)
    p = jnp.exp2(s - m_new).astype(BF16)
    acc = alpha * acc + jnp.dot(vta, p, preferred_element_type=F32)
    return m_new, acc


def _with_ones(vtc):
    return jnp.concatenate([vtc, jnp.ones((ONES_ROWS, vtc.shape[1]), BF16)], axis=0)


def _attn_chunk(q_ref, kc, vtc, carry):
    vta = _with_ones(vtc)
    hg = ATT_GROUP // ATT_SPLIT
    tq = q_ref.shape[1]
    out = []
    for g in range(ATT_SPLIT):
        qg = q_ref[g * hg:(g + 1) * hg].reshape(hg * tq, HEAD_DIM)
        out.append(_attn_step(qg, kc, vta, *carry[g]))
    return tuple(out)


def _attn_init(tq):
    nq = (ATT_GROUP // ATT_SPLIT) * tq
    one = (jnp.full((1, nq), -1e30, F32), jnp.zeros((HEAD_DIM + ONES_ROWS, nq), F32))
    return (one,) * ATT_SPLIT


def _attn_finish(carry, o_ref):
    tq = o_ref.shape[0]
    outs = []
    for _, acc in carry:
        o = acc[:HEAD_DIM] * (1.0 / acc[HEAD_DIM:HEAD_DIM + 1])
        outs += [o[:, h * tq:(h + 1) * tq] for h in range(ATT_GROUP // ATT_SPLIT)]
    o_ref[...] = jnp.concatenate(outs, axis=0).T.astype(BF16)


def _attn_prompt_kernel(q_ref, k_ref, vt_ref, o_ref):
    carry = _attn_chunk(q_ref, k_ref[...], vt_ref[...], _attn_init(q_ref.shape[1]))
    _attn_finish(carry, o_ref)


def _attn_sample_kernel(q_ref, k_ref, vt_ref, ck_ref, cv_ref, o_ref, *, nchunks):
    def body(j, carry):
        r0 = pl.multiple_of(j * TK_ATT, TK_ATT)
        return _attn_chunk(q_ref, k_ref[pl.ds(r0, TK_ATT), :], vt_ref[j], carry)

    carry = lax.fori_loop(0, nchunks, body, _attn_init(q_ref.shape[1]), unroll=ATT_UNROLL)
    first = pl.program_id(1) == 0
    ck = ck_ref[...]
    cv = cv_ref[...]
    ck = jnp.where(first, ck[:, :HEAD_DIM], ck[:, HEAD_DIM:]).astype(BF16)
    cvt = jnp.where(first, cv[:, :HEAD_DIM], cv[:, HEAD_DIM:]).T.astype(BF16)
    carry = _attn_chunk(q_ref, ck, cvt, carry)
    _attn_finish(carry, o_ref)


def _attention_prompt(q, k, vt):
    return pl.pallas_call(
        _attn_prompt_kernel,
        out_shape=jax.ShapeDtypeStruct((NP, ATT_W), BF16),
        grid=(BATCH, ATT_KV_HEADS),
        in_specs=[
            pl.BlockSpec((ATT_GROUP, SEQ, HEAD_DIM), lambda b, g: (g, b, 0)),
            pl.BlockSpec((None, SEQ, HEAD_DIM), lambda b, g: (g, b, 0)),
            pl.BlockSpec((None, HEAD_DIM, SEQ),
                         lambda b, g: (b // (TM_PROJ // SEQ), g, b % (TM_PROJ // SEQ))),
        ],
        out_specs=pl.BlockSpec((SEQ, ATT_GROUP * HEAD_DIM), lambda b, g: (b, g)),
        compiler_params=_params(("arbitrary", "arbitrary")),
        name="attn_prompt",
    )(q, k, vt)


def _attention_sample(q, k, vt, ctx_k, ctx_v):
    tq = TQ_ATT
    nchunks = DEC_SEQ // TK_ATT
    q_off = NP // tq
    kv_off = NP // DEC_SEQ
    return pl.pallas_call(
        functools.partial(_attn_sample_kernel, nchunks=nchunks),
        out_shape=jax.ShapeDtypeStruct((NS, ATT_W), BF16),
        grid=(DEC_BATCH, ATT_KV_HEADS, DEC_SEQ // tq),
        in_specs=[
            pl.BlockSpec((ATT_GROUP, tq, HEAD_DIM),
                         lambda b, g, i: (g, q_off + b * (DEC_SEQ // tq) + i, 0)),
            pl.BlockSpec((None, DEC_SEQ, HEAD_DIM), lambda b, g, i: (g, kv_off + b, 0)),
            pl.BlockSpec((nchunks, HEAD_DIM, TK_ATT), lambda b, g, i: (kv_off + b, g, 0)),
            pl.BlockSpec((None, PAST_LEN, KV_W), lambda b, g, i: (b, 0, 0)),
            pl.BlockSpec((None, PAST_LEN, KV_W), lambda b, g, i: (b, 0, 0)),
        ],
        out_specs=pl.BlockSpec((tq, ATT_GROUP * HEAD_DIM),
                               lambda b, g, i: (b * (DEC_SEQ // tq) + i, g)),
        compiler_params=_params(("arbitrary", "arbitrary", "arbitrary")),
        name="attn_sample",
    )(q, k, vt, ctx_k, ctx_v)


def _conv_kernel(prev_ref, cur_ref, next_ref, w_ref, b_ref, g_ref, bb_ref, o_ref, win_ref, y_ref,
                 *, npt, tps):
    r = cur_ref.shape[0]
    i = pl.program_id(0)
    j = (i - npt) % tps
    first = jnp.logical_or(i < npt, j == 0)
    last = jnp.logical_or(i < npt, j == tps - 1)
    win_ref[0:HALO, :] = jnp.where(first, 0.0, prev_ref[...])
    win_ref[HALO:HALO + r, :] = cur_ref[...]
    win_ref[HALO + r:2 * HALO + r, :] = jnp.where(last, 0.0, next_ref[...])
    base = HALO - CONV_K // 2
    sub = 8
    for c in range(CONV_CH // LANES):
        cs = slice(c * LANES, (c + 1) * LANES)
        acc = None
        for res in range(sub):
            p = None
            for k in range(CONV_K):
                if (base + k) % sub != res:
                    continue
                a = (base + k) - res
                term = win_ref[a:a + r + sub, cs] * w_ref[k:k + 1, cs]
                p = term if p is None else p + term
            if p is None:
                continue
            p = p[res:res + r]
            acc = p if acc is None else acc + p
        y_ref[:, cs] = acc + b_ref[:, cs]
    y = _layernorm(y_ref[...], g_ref[...], bb_ref[...])
    o_ref[...] = (y * _sigmoid(y)).astype(BF16)


def _conv_module(u, conv_w, conv_b, cln_g, cln_b):
    r = R_CONV
    npt = NP // r
    tps = DEC_SEQ // r
    hb = r // HALO
    nh = NT // HALO
    w = jnp.concatenate([conv_w, jnp.zeros((1, CONV_CH), F32)], axis=0)
    return pl.pallas_call(
        functools.partial(_conv_kernel, npt=npt, tps=tps),
        out_shape=jax.ShapeDtypeStruct((NT, CONV_CH), BF16),
        grid=(NT // r,),
        in_specs=[
            pl.BlockSpec((HALO, CONV_CH), lambda i: (jnp.maximum(i * hb - 1, 0), 0)),
            pl.BlockSpec((r, CONV_CH), lambda i: (i, 0)),
            pl.BlockSpec((HALO, CONV_CH), lambda i: (jnp.minimum((i + 1) * hb, nh - 1), 0)),
            pl.BlockSpec((CONV_K + 1, CONV_CH), lambda i: (0, 0)),
            pl.BlockSpec((1, CONV_CH), lambda i: (0, 0)),
            pl.BlockSpec((1, CONV_CH), lambda i: (0, 0)),
            pl.BlockSpec((1, CONV_CH), lambda i: (0, 0)),
        ],
        out_specs=pl.BlockSpec((r, CONV_CH), lambda i: (i, 0)),
        scratch_shapes=[pltpu.VMEM((r + 2 * HALO, CONV_CH), F32), pltpu.VMEM((r, CONV_CH), F32)],
        compiler_params=_params(("arbitrary",)),
        name="conv_module",
    )(u, u, u, w, conv_b.reshape(1, CONV_CH), cln_g.reshape(1, CONV_CH), cln_b.reshape(1, CONV_CH))


def _outproj_a_kernel(ap_ref, as_ref, uc_ref, xp_ref, xs_ref, mod_ref, w_ref, g_ref, b_ref, o_ref,
                      w_scr, *, npt):
    i = pl.program_id(0)
    _cast_once(w_ref, w_scr)
    att = jnp.where(i < npt, ap_ref[...], as_ref[...])
    x = jnp.where(i < npt, xp_ref[...], xs_ref[...])
    y = jnp.dot(att, w_scr[0:ATT_W, :], preferred_element_type=F32)
    y = y + jnp.dot(uc_ref[...], w_scr[ATT_W:ATT_W + CONV_CH, :], preferred_element_type=F32)
    mod = mod_ref[...]
    o_ref[...] = _layernorm(ALPHA * x + mod[2:3] * y, g_ref[...], b_ref[...])


def _outproj_a(att_p, att_s, uc, xp, xs, mod, w, g, b):
    tm = TM_PROJ
    npt = NP // tm
    first = lambda i: (jnp.minimum(i, npt - 1), 0)
    second = lambda i: (jnp.maximum(i - npt, 0), 0)
    return pl.pallas_call(
        functools.partial(_outproj_a_kernel, npt=npt),
        out_shape=jax.ShapeDtypeStruct((NT, D_MODEL), F32),
        grid=(NT // tm,),
        in_specs=[
            pl.BlockSpec((tm, ATT_W), first),
            pl.BlockSpec((tm, ATT_W), second),
            pl.BlockSpec((tm, CONV_CH), lambda i: (i, 0)),
            pl.BlockSpec((tm, D_MODEL), first),
            pl.BlockSpec((tm, D_MODEL), second),
            pl.BlockSpec((None, 6, D_MODEL), lambda i: (_cond_index(i, tm), 0, 0)),
            pl.BlockSpec((None, ATT_W + CONV_CH, D_MODEL), lambda i: (0, 0, 0)),
            pl.BlockSpec((1, D_MODEL), lambda i: (0, 0)),
            pl.BlockSpec((1, D_MODEL), lambda i: (0, 0)),
        ],
        out_specs=pl.BlockSpec((tm, D_MODEL), lambda i: (i, 0)),
        scratch_shapes=[pltpu.VMEM((ATT_W + CONV_CH, D_MODEL), BF16)],
        compiler_params=_params(("arbitrary",)),
        name="outproj_a",
    )(att_p, att_s, uc, xp, xs, mod, w, g, b)


def _outproj_m_kernel(hp_ref, lo_ref, hi_ref, x_ref, mod_ref, w_ref, g_ref, b_ref, o_ref, w_scr,
                      *, npt, tps):
    i = pl.program_id(0)
    _cast_once(w_ref, w_scr)
    upper = ((i - npt) % tps) >= tps // 2
    hg = jnp.where(i < npt, hp_ref[...], jnp.where(upper, hi_ref[...], lo_ref[...]))
    y = jnp.dot(hg, w_scr[...], preferred_element_type=F32)
    mod = mod_ref[...]
    o_ref[...] = _layernorm(ALPHA * x_ref[...] + mod[2:3] * y, g_ref[...], b_ref[...])


def _outproj_m(hg_p, hg_lo, hg_hi, x, mod, w, g, b):
    tm = TM_PROJ
    npt = NP // tm
    tps = DEC_SEQ // tm
    hps = tps // 2

    def half_idx(i, upper):
        s = jnp.maximum(i - npt, 0)
        t = s % tps
        t = jnp.maximum(t - hps, 0) if upper else jnp.minimum(t, hps - 1)
        return ((s // tps) * hps + t, 0)

    return pl.pallas_call(
        functools.partial(_outproj_m_kernel, npt=npt, tps=tps),
        out_shape=jax.ShapeDtypeStruct((NT, D_MODEL), F32),
        grid=(NT // tm,),
        in_specs=[
            pl.BlockSpec((tm, M_INNER), lambda i: (jnp.minimum(i, npt - 1), 0)),
            pl.BlockSpec((tm, M_INNER), lambda i: half_idx(i, False)),
            pl.BlockSpec((tm, M_INNER), lambda i: half_idx(i, True)),
            pl.BlockSpec((tm, D_MODEL), lambda i: (i, 0)),
            pl.BlockSpec((None, 6, D_MODEL), lambda i: (_cond_index(i, tm), 0, 0)),
            pl.BlockSpec((None, M_INNER, D_MODEL), lambda i: (0, 0, 0)),
            pl.BlockSpec((1, D_MODEL), lambda i: (0, 0)),
            pl.BlockSpec((1, D_MODEL), lambda i: (0, 0)),
        ],
        out_specs=pl.BlockSpec((tm, D_MODEL), lambda i: (i, 0)),
        scratch_shapes=[pltpu.VMEM((M_INNER, D_MODEL), BF16)],
        compiler_params=_params(("arbitrary",)),
        name="outproj_m",
    )(hg_p, hg_lo, hg_hi, x, mod, w, g, b)


def _ffn_kernel(x_ref, mod_ref, w1_ref, w2_ref, g_ref, b_ref, o_ref, w1_scr, w2_scr, *, ncast):
    s = pl.program_id(0)

    @pl.when(s < ncast)
    def _():
        w1_scr[s] = w1_ref[...].astype(BF16)
        w2_scr[s] = w2_ref[...].astype(BF16)

    @pl.when(s >= ncast)
    def _():
        mod = mod_ref[...]
        half = x_ref.shape[0] // 2
        for r in range(2):
            rows = slice(r * half, (r + 1) * half)
            x = x_ref[rows, :]
            h = (x * (1.0 + mod[4:5]) + mod[3:4]).astype(BF16)
            acc = None
            for c in range(ncast):
                a = jnp.maximum(jnp.dot(h, w1_scr[c], preferred_element_type=F32), 0.0)
                y = jnp.dot((a * a).astype(BF16), w2_scr[c], preferred_element_type=F32)
                acc = y if acc is None else acc + y
            o_ref[rows, :] = _layernorm(ALPHA * x + mod[5:6] * acc, g_ref[...], b_ref[...])


def _ffn(x, row0, nrows, mod, w1, w2, layer, g, b):
    tm, tf = TM_PROJ, TF_FFN
    off = row0 // tm
    ncast = D_FF // tf
    row = lambda s: jnp.maximum(s - ncast, 0)
    blk = lambda s: jnp.minimum(s, ncast - 1)
    return pl.pallas_call(
        functools.partial(_ffn_kernel, ncast=ncast),
        out_shape=jax.ShapeDtypeStruct((nrows, D_MODEL), F32),
        grid=(ncast + nrows // tm,),
        in_specs=[
            pl.BlockSpec((tm, D_MODEL), lambda s: (row(s) + off, 0)),
            pl.BlockSpec((None, 6, D_MODEL), lambda s: (_cond_index(row(s) + off, tm), 0, 0)),
            pl.BlockSpec((None, D_MODEL, tf), lambda s: (layer, 0, blk(s))),
            pl.BlockSpec((None, tf, D_MODEL), lambda s: (layer, blk(s), 0)),
            pl.BlockSpec((1, D_MODEL), lambda s: (0, 0)),
            pl.BlockSpec((1, D_MODEL), lambda s: (0, 0)),
        ],
        out_specs=pl.BlockSpec((tm, D_MODEL), lambda s: (row(s), 0)),
        scratch_shapes=[pltpu.VMEM((ncast, D_MODEL, tf), BF16), pltpu.VMEM((ncast, tf, D_MODEL), BF16)],
        compiler_params=_params(("arbitrary",)),
        name="ffn",
    )(x, mod, w1, w2, g, b)


def _inproj_m_kernel(x_ref, mod_ref, w_ref, wg_ref, bg_ref, o_ref, gt_ref, vt_ref):
    nslab = vt_ref.shape[0]
    L = L_CHUNK
    mod = mod_ref[...]
    h = (x_ref[...] * (1.0 + mod[1:2]) + mod[0:1]).astype(BF16)

    gates = jnp.dot(h, wg_ref[...], preferred_element_type=F32) + bg_ref[...]
    kind = lax.broadcasted_iota(jnp.int32, gates.shape, 1) % GATE_STRIDE
    gates = jnp.where((kind == 2) | (kind == 3), _log_sigmoid(gates), gates)
    gates_t = gates.T
    for s in range(nslab):
        gt_ref[s] = gates_t[0:GATE_ROWS, s * L:(s + 1) * L]

    for j in range(4):
        cols = slice(j * M_INNER, (j + 1) * M_INNER)
        p = jnp.dot(h, w_ref[:, cols], preferred_element_type=F32)
        if j == 1:
            p = p * (M_DK ** -0.5)
        o_ref[:, cols] = p.astype(BF16)
        if j == 2:
            v_t = p.T.astype(BF16)
            for s in range(nslab):
                vt_ref[s] = v_t[:, s * L:(s + 1) * L]


def _inproj_m(x, mod, w, wg, bg):
    tm = TM_PROJ
    nslab = tm // L_CHUNK
    return pl.pallas_call(
        _inproj_m_kernel,
        out_shape=[
            jax.ShapeDtypeStruct((NT, 4 * M_INNER), BF16),
            jax.ShapeDtypeStruct((NT // L_CHUNK, GATE_ROWS, L_CHUNK), F32),
            jax.ShapeDtypeStruct((NT // L_CHUNK, M_INNER, L_CHUNK), BF16),
        ],
        grid=(NT // tm,),
        in_specs=[
            pl.BlockSpec((tm, D_MODEL), lambda i: (i, 0)),
            pl.BlockSpec((None, 6, D_MODEL), lambda i: (_cond_index(i, tm), 0, 0)),
            pl.BlockSpec((D_MODEL, 4 * M_INNER), lambda i: (0, 0)),
            pl.BlockSpec((D_MODEL, LANES), lambda i: (0, 0)),
            pl.BlockSpec((1, LANES), lambda i: (0, 0)),
        ],
        out_specs=[
            pl.BlockSpec((tm, 4 * M_INNER), lambda i: (i, 0)),
            pl.BlockSpec((nslab, GATE_ROWS, L_CHUNK), lambda i: (i, 0, 0)),
            pl.BlockSpec((nslab, M_INNER, L_CHUNK), lambda i: (i, 0, 0)),
        ],
        compiler_params=_params(("arbitrary",)),
        name="inproj_m",
    )(x, mod, w, wg, bg)


def _log_sigmoid(x):
    return jnp.minimum(x, 0.0) - jnp.log1p(jnp.exp(-jnp.abs(x)))


def _split3(x):
    x1 = x.astype(BF16)
    r1 = x - x1.astype(F32)
    x2 = r1.astype(BF16)
    x3 = (r1 - x2.astype(F32)).astype(BF16)
    return x1, x2, x3


def _mlstm_prefix(gt):
    L = gt.shape[1]
    parts = jnp.concatenate([t.astype(F32) for t in _split3(gt)]
                            + [jnp.zeros((GATE_ROWS, L), F32)], axis=0).astype(BF16)
    upper = jnp.where(lax.broadcasted_iota(jnp.int32, (L, L), 0)
                      <= lax.broadcasted_iota(jnp.int32, (L, L), 1), 1.0, 0.0).astype(BF16)
    r = jnp.dot(parts, upper, preferred_element_type=F32)
    return r[0:GATE_ROWS] + r[GATE_ROWS:2 * GATE_ROWS] + r[2 * GATE_ROWS:3 * GATE_ROWS]


def _rows3(x, n):
    g = x.shape[0]
    return jnp.concatenate([t.astype(F32) for t in _split3(x)]
                           + [jnp.zeros((g, ONES_ROWS - 3, n), F32)], axis=1).astype(BF16)


def _sum3(x, r0):
    return x[:, r0:r0 + 1] + x[:, r0 + 1:r0 + 2] + x[:, r0 + 2:r0 + 3]


def _mlstm_group(rev, q, k, vt, gt, pre, state):
    G, L = q.shape[0], q.shape[1]
    d = 1 if rev else 0
    tpos = lax.broadcasted_iota(jnp.int32, (L, L), 0)
    spos = lax.broadcasted_iota(jnp.int32, (L, L), 1)
    keep = (tpos >= spos) if rev else (tpos <= spos)
    if rev:
        tot = pre[:, 3:4, L - 1:L]
        a_row = tot - pre[:, 3:4, :] + gt[:, 3:4, :]
    else:
        a_row = pre[:, 2:3, :]
        tot = a_row[:, :, L - 1:L]
    b_row = gt[:, d:d + 1, :] - a_row
    m = jnp.zeros((G, 1, 1), F32) if state is None else state[0]

    gp = -(-G // 8) * 8
    rows = [b_row[p] for p in range(G)] + [jnp.zeros((gp - G, L), F32)] * (1 if gp > G else 0)
    bt = jnp.concatenate(rows, axis=0).T
    bm = jnp.stack([jnp.where(keep, bt[:, p:p + 1], -jnp.inf) for p in range(G)])
    mm = jnp.maximum(m, jnp.max(bm, axis=1, keepdims=True))
    w = jnp.exp(bm - mm)
    s = jnp.einsum('gsd,gtd->gst', k, q, preferred_element_type=F32)
    qk = s * w
    num = jnp.einsum('gvs,gst->gvt', vt, qk.astype(BF16), preferred_element_type=F32)
    den = jnp.sum(qk, axis=1, keepdims=True)
    if state is not None:
        _, ct, n = state
        cn = jnp.concatenate([ct.astype(BF16), _rows3(n, n.shape[2])], axis=1)
        inter = jnp.einsum('gcd,gtd->gct', cn, q, preferred_element_type=F32)
        s_inter = jnp.exp(m - mm)
        num = num + s_inter * inter[:, :M_DV]
        den = den + s_inter * _sum3(inter, M_DV)
    h = num * (1.0 / jnp.maximum(jnp.abs(den), jnp.exp(-(a_row + mm))))

    g_row = tot + b_row
    m_new = jnp.maximum(tot + m, jnp.max(g_row, axis=2, keepdims=True))
    ws = jnp.exp(g_row - m_new)
    lhs = jnp.concatenate([(vt.astype(F32) * ws).astype(BF16), _rows3(ws, L)], axis=1)
    upd = jnp.einsum('gcs,gsd->gcd', lhs, k, preferred_element_type=F32)
    ct_new = upd[:, :M_DV]
    n_new = _sum3(upd, M_DV)
    if state is not None:
        decay = jnp.exp(tot + m - m_new)
        ct_new = decay * ct + ct_new
        n_new = decay * n + n_new
    return h, (m_new, ct_new, n_new)


def _heads(x):
    w = x.shape[1] // M_HEADS
    return jnp.stack([x[:, h * w:(h + 1) * w] for h in range(M_HEADS)])


def _mlstm_finish(ht, o, gain):
    ms = jnp.mean(ht * ht, axis=1, keepdims=True)
    hn = ht * lax.rsqrt(ms + EPS)
    hn = jnp.concatenate([hn[h].T for h in range(M_HEADS)], axis=1)
    return (_sigmoid(o.astype(F32)) * hn * gain).astype(BF16)


def _mlstm_sample_kernel(qf_ref, kf_ref, vtf_ref, gtf_ref, of_ref, qb_ref, kb_ref, vtb_ref, gtb_ref,
                         ob_ref, gain_ref, c0_ref, n0_ref, m0_ref, hi_ref, lo_ref,
                         hs_ref, ct_scr, n_scr, m_scr, *, nc):
    j = pl.program_id(1)
    jb = nc - 1 - j
    H = M_HEADS
    L = L_CHUNK

    @pl.when(j == 0)
    def _():
        for d in range(2):
            for h in range(H):
                ct_scr[d * H + h] = c0_ref[d, h].T
                n_scr[d * H + h] = n0_ref[d, h]
                m_scr[d * H + h] = jnp.broadcast_to(m0_ref[d, h], (1, LANES))

    def run(rev, q_ref, k_ref, vt_ref, gt_ref, lo):
        gt = gt_ref[...]
        pre = _mlstm_prefix(gt).reshape(H, GATE_STRIDE, L)
        state = (m_scr[lo:lo + H][:, :, 0:1], ct_scr[lo:lo + H], n_scr[lo:lo + H])
        h, (m_new, ct_new, n_new) = _mlstm_group(
            rev, _heads(q_ref[...]), _heads(k_ref[...]), vt_ref[...].reshape(H, M_DV, L),
            gt.reshape(H, GATE_STRIDE, L), pre, state)
        ct_scr[lo:lo + H] = ct_new
        n_scr[lo:lo + H] = n_new
        m_scr[lo:lo + H] = jnp.broadcast_to(m_new, (H, 1, LANES))
        return h

    h_f = run(False, qf_ref, kf_ref, vtf_ref, gtf_ref, 0)
    h_b = run(True, qb_ref, kb_ref, vtb_ref, gtb_ref, H)

    @pl.when(j < nc // 2)
    def _():
        hs_ref[j] = h_f
        hs_ref[jb] = h_b

    @pl.when(j >= nc // 2)
    def _():
        gain = gain_ref[...]
        hi_ref[...] = _mlstm_finish(hs_ref[j] + h_f, of_ref[...], gain)
        lo_ref[...] = _mlstm_finish(hs_ref[jb] + h_b, ob_ref[...], gain)


def _mlstm_sample(qkvo, vt, gates_t, gain, init):
    nc = DEC_SEQ // L_CHUNK
    half = nc // 2
    c0 = NP // L_CHUNK
    L = L_CHUNK
    fwd = lambda b, j: c0 + b * nc + j
    bwd = lambda b, j: c0 + b * nc + nc - 1 - j

    def specs(f):
        return [
            pl.BlockSpec((L, M_INNER), lambda b, j: (f(b, j), 0)),
            pl.BlockSpec((L, M_INNER), lambda b, j: (f(b, j), 1)),
            pl.BlockSpec((None, M_INNER, L), lambda b, j: (f(b, j), 0, 0)),
            pl.BlockSpec((None, GATE_ROWS, L), lambda b, j: (f(b, j), 0, 0)),
            pl.BlockSpec((L, M_INNER), lambda b, j: (f(b, j), 3)),
        ]

    state_specs = [
        pl.BlockSpec((None, 2, M_HEADS, M_DK, M_DV), lambda b, j: (b, 0, 0, 0, 0)),
        pl.BlockSpec((None, 2, M_HEADS, 1, M_DK), lambda b, j: (b, 0, 0, 0, 0)),
        pl.BlockSpec((None, 2, M_HEADS, 1, 1), lambda b, j: (b, 0, 0, 0, 0)),
    ]
    out_rows = DEC_BATCH * DEC_SEQ // 2
    return pl.pallas_call(
        functools.partial(_mlstm_sample_kernel, nc=nc),
        out_shape=[jax.ShapeDtypeStruct((out_rows, M_INNER), BF16)] * 2,
        grid=(DEC_BATCH, nc),
        in_specs=specs(fwd) + specs(bwd) + [pl.BlockSpec((1, M_INNER), lambda b, j: (0, 0))] + state_specs,
        out_specs=[
            pl.BlockSpec((L, M_INNER), lambda b, j: (b * half + jnp.maximum(j, half) - half, 0)),
            pl.BlockSpec((L, M_INNER), lambda b, j: (b * half + jnp.minimum(nc - 1 - j, half - 1), 0)),
        ],
        scratch_shapes=[
            pltpu.VMEM((nc, M_HEADS, M_DV, L), F32),
            pltpu.VMEM((2 * M_HEADS, M_DV, M_DK), F32),
            pltpu.VMEM((2 * M_HEADS, 1, M_DK), F32),
            pltpu.VMEM((2 * M_HEADS, 1, LANES), F32),
        ],
        compiler_params=_params(("arbitrary", "arbitrary")),
        name="mlstm",
    )(qkvo, qkvo, vt, gates_t, qkvo, qkvo, qkvo, vt, gates_t, qkvo, gain, *init)


def _mlstm_prompt_kernel(q_ref, k_ref, vt_ref, gt_ref, o_ref, gain_ref, out_ref, co_ref, no_ref, mo_ref,
                         *, nseq):
    H = M_HEADS
    L = L_CHUNK
    G = nseq * H
    q = jnp.concatenate([_heads(q_ref[b * L:(b + 1) * L, :]) for b in range(nseq)], axis=0)
    k = jnp.concatenate([_heads(k_ref[b * L:(b + 1) * L, :]) for b in range(nseq)], axis=0)
    vt = vt_ref[...].reshape(G, M_DV, L)
    gt3 = jnp.concatenate([gt_ref[b] for b in range(nseq)], axis=0).reshape(G, GATE_STRIDE, L)
    pre3 = jnp.concatenate([_mlstm_prefix(gt_ref[b]) for b in range(nseq)],
                           axis=0).reshape(G, GATE_STRIDE, L)
    gain = gain_ref[...]
    hs = None
    for d in range(2):
        h, (m_new, ct_new, n_new) = _mlstm_group(d == 1, q, k, vt, gt3, pre3, None)
        hs = h if hs is None else hs + h
        for b in range(nseq):
            for hh in range(H):
                co_ref[b, d, hh] = ct_new[b * H + hh].T
                no_ref[b, d, hh] = n_new[b * H + hh]
                mo_ref[b, d, hh] = m_new[b * H + hh]
    for b in range(nseq):
        out_ref[b * L:(b + 1) * L, :] = _mlstm_finish(hs[b * H:(b + 1) * H], o_ref[b * L:(b + 1) * L, :], gain)


def _mlstm_prompt(qkvo, vt, gates_t, gain):
    nseq = MLSTM_SEQS
    L = L_CHUNK
    rows = nseq * L
    state_specs = [
        pl.BlockSpec((nseq, 2, M_HEADS, M_DK, M_DV), lambda i: (i, 0, 0, 0, 0)),
        pl.BlockSpec((nseq, 2, M_HEADS, 1, M_DK), lambda i: (i, 0, 0, 0, 0)),
        pl.BlockSpec((nseq, 2, M_HEADS, 1, 1), lambda i: (i, 0, 0, 0, 0)),
    ]
    return pl.pallas_call(
        functools.partial(_mlstm_prompt_kernel, nseq=nseq),
        out_shape=[
            jax.ShapeDtypeStruct((NP, M_INNER), BF16),
            jax.ShapeDtypeStruct((BATCH, 2, M_HEADS, M_DK, M_DV), F32),
            jax.ShapeDtypeStruct((BATCH, 2, M_HEADS, 1, M_DK), F32),
            jax.ShapeDtypeStruct((BATCH, 2, M_HEADS, 1, 1), F32),
        ],
        grid=(BATCH // nseq,),
        in_specs=[
            pl.BlockSpec((rows, M_INNER), lambda i: (i, 0)),
            pl.BlockSpec((rows, M_INNER), lambda i: (i, 1)),
            pl.BlockSpec((nseq, M_INNER, L), lambda i: (i, 0, 0)),
            pl.BlockSpec((nseq, GATE_ROWS, L), lambda i: (i, 0, 0)),
            pl.BlockSpec((rows, M_INNER), lambda i: (i, 3)),
            pl.BlockSpec((1, M_INNER), lambda i: (0, 0)),
        ],
        out_specs=[pl.BlockSpec((rows, M_INNER), lambda i: (i, 0))] + state_specs,
        compiler_params=_params(("arbitrary",)),
        name="mlstm_state",
    )(qkvo, qkvo, vt, gates_t, qkvo, gain)


@jax.jit
def kernel(x_prompt, x_sample, cache_k, cache_v, state_c, state_n, state_m, c, c_ctx, w_ada, b_ada,
           ln_g, ln_b, w_ff1, w_ff2, w_in_a, q_gain, k_gain, conv_w, conv_b, conv_ln_g, conv_ln_b,
           w_out_a, w_in_m, b_gate_m, mh_gain, w_out_m):
    xp = x_prompt.reshape(NP, D_MODEL)
    xs = x_sample.reshape(NS, D_MODEL)

    cond = jnp.concatenate(
        [c_ctx[None, :], c, jnp.zeros((N_COND - 1 - DEC_BATCH, D_MODEL), F32)], axis=0)
    mods = _adaln(cond, w_ada, b_ada)

    ln = lambda l, s: (ln_g[l, s].reshape(1, D_MODEL), ln_b[l, s].reshape(1, D_MODEL))

    q, k, vt, kf, vf, u = _inproj_a(xp, xs, mods[0], w_in_a, q_gain[0], k_gain[0])
    att_p = _attention_prompt(q, k, vt)
    ctx_k = cache_k[:, 0].reshape(DEC_BATCH, PAST_LEN, KV_W)
    ctx_v = cache_v[:, 0].reshape(DEC_BATCH, PAST_LEN, KV_W)
    att_s = _attention_sample(q, k, vt, ctx_k, ctx_v)
    uc = _conv_module(u, conv_w[0], conv_b[0], conv_ln_g[0], conv_ln_b[0])
    x1 = _outproj_a(att_p, att_s, uc, xp, xs, mods[0], w_out_a, *ln(0, 0))
    x2 = _ffn(x1, 0, NT, mods[0], w_ff1, w_ff2, 0, *ln(0, 1))

    wm = w_in_m[0]
    src = jnp.array([4 * M_INNER + g * M_HEADS + h
                     for h in range(M_HEADS) for g in (0, 2, 1, 3)], jnp.int32)
    dst = jnp.array([h * GATE_STRIDE + t for h in range(M_HEADS) for t in range(4)], jnp.int32)
    wg = jnp.zeros((D_MODEL, LANES), F32).at[:, dst].set(wm[:, src]).astype(BF16)
    bg = jnp.zeros((1, LANES), F32).at[0, dst].set(b_gate_m[0][src - 4 * M_INNER])
    qkvo, gates_t, vt_m = _inproj_m(x2, mods[1], wm[:, :4 * M_INNER].astype(BF16), wg, bg)
    gain = mh_gain[0].reshape(1, M_INNER)
    hg_p, st_c, st_n, st_m = _mlstm_prompt(qkvo, vt_m, gates_t, gain)
    init = (state_c[:, 0], state_n[:, 0].reshape(DEC_BATCH, 2, M_HEADS, 1, M_DK),
            state_m[:, 0].reshape(DEC_BATCH, 2, M_HEADS, 1, 1))
    hg_hi, hg_lo = _mlstm_sample(qkvo, vt_m, gates_t, gain, init)
    x3 = _outproj_m(hg_p, hg_lo, hg_hi, x2, mods[1], w_out_m, *ln(1, 0))
    y_p = _ffn(x3, 0, NP, mods[1], w_ff1, w_ff2, 1, *ln(1, 1))
    y_s = _ffn(x3, NP, NS, mods[1], w_ff1, w_ff2, 1, *ln(1, 1))

    new_k = kf[:NP].reshape(BATCH, 1, SEQ, ATT_KV_HEADS, HEAD_DIM)
    new_v = vf[:NP].reshape(BATCH, 1, SEQ, ATT_KV_HEADS, HEAD_DIM)
    return (y_p.reshape(BATCH, SEQ, D_MODEL), y_s.reshape(DEC_BATCH, DEC_SEQ, D_MODEL),
            new_k, new_v,
            st_c.reshape(BATCH, 1, 2, M_HEADS, M_DK, M_DV),
            st_n.reshape(BATCH, 1, 2, M_HEADS, M_DK),
            st_m.reshape(BATCH, 1, 2, M_HEADS))
```

```python
import functools

import jax
import jax.numpy as jnp
from jax import lax
from jax.experimental import pallas as pl
from jax.experimental.pallas import tpu as pltpu

F32 = jnp.float32
BF16 = jnp.bfloat16

D_MODEL = 1024
BATCH = 32
SEQ = 256
DEPTH = 2
DEC_BATCH = 4
DEC_SEQ = 4096
PAST_LEN = 512
GRID_W = 64
ATT_HEADS = 8
ATT_KV_HEADS = 2
HEAD_DIM = 64
ATT_GROUP = ATT_HEADS // ATT_KV_HEADS
ATT_W = ATT_HEADS * HEAD_DIM
KV_W = ATT_KV_HEADS * HEAD_DIM
ROPE_AXIS_DIM = HEAD_DIM // 2
ROPE_THETA = 10000.0
CONV_CH = D_MODEL // 2
CONV_K = 31
EVEN_IN = ATT_W + 2 * KV_W + 2 * CONV_CH
M_HEADS = 4
M_INNER = D_MODEL
M_DK = M_INNER // M_HEADS
M_DV = M_INNER // M_HEADS
D_FF = 4 * D_MODEL
ALPHA = (2 * DEPTH) ** 0.25
EPS = 1e-6
LOG2E = 1.4426950408889634

NP = BATCH * SEQ
NS = DEC_BATCH * DEC_SEQ
NT = NP + NS
N_COND = 8

LANES = 128
VMEM_LIMIT = 56 * 1024 * 1024

TM_PROJ = 512
TF_FFN = 1024
TQ_ATT = 256
TK_ATT = 512
R_CONV = 256
HALO = 16
L_CHUNK = 256
GATE_STRIDE = 8
ONES_ROWS = 16
GATE_ROWS = 32
MLSTM_SEQS = 4
OUTPROJ_SUBTILES = 4
INPROJ_SUBTILES = 4
ATT_UNROLL = 9


def _cond_index(i_global, tm):
    npt = NP // tm
    tps = DEC_SEQ // tm
    return jnp.where(i_global < npt, 0, 1 + (i_global - npt) // tps)


def _layernorm(r, g, b):
    mu = jnp.mean(r, axis=-1, keepdims=True)
    d = r - mu
    var = jnp.mean(d * d, axis=-1, keepdims=True)
    return d * lax.rsqrt(var + EPS) * g + b


def _sigmoid(x):
    return 1.0 / (1.0 + jnp.exp(-x))


def _params(sem, vmem=VMEM_LIMIT, flags=None):
    return pltpu.CompilerParams(dimension_semantics=sem, vmem_limit_bytes=vmem, flags=flags)


def _adaln_kernel(cond_ref, w_ref, b_ref, o_ref):
    c = cond_ref[...]
    s = (c * _sigmoid(c)).astype(BF16)
    o_ref[...] = jnp.dot(s, w_ref[...].astype(BF16), preferred_element_type=F32) + b_ref[...]


def _adaln(cond, w_ada, b_ada):
    tn = 1536
    n = 6 * D_MODEL
    out = pl.pallas_call(
        _adaln_kernel,
        out_shape=jax.ShapeDtypeStruct((DEPTH, N_COND, n), F32),
        grid=(DEPTH, n // tn),
        in_specs=[
            pl.BlockSpec((N_COND, D_MODEL), lambda l, j: (0, 0)),
            pl.BlockSpec((None, D_MODEL, tn), lambda l, j: (l, 0, j)),
            pl.BlockSpec((None, 1, tn), lambda l, j: (l, 0, j)),
        ],
        out_specs=pl.BlockSpec((None, N_COND, tn), lambda l, j: (l, 0, j)),
        compiler_params=_params(("arbitrary", "arbitrary")),
        name="adaln",
    )(cond, w_ada, b_ada.reshape(DEPTH, 1, n))
    return out.reshape(DEPTH, N_COND, 6, D_MODEL)


def _cast_once(w_ref, w_scr):
    @pl.when(pl.program_id(0) == 0)
    def _():
        w_scr[...] = w_ref[...].astype(BF16)


def _inproj_a_kernel(xp_ref, xs_ref, mod_ref, w_ref, qg_ref, kg_ref, cos_ref, sin_ref,
                     q_ref, k_ref, v_ref, kf_ref, vf_ref, u_ref, w_scr, *, npt):
    i = pl.program_id(0)
    _cast_once(w_ref, w_scr)
    mod = mod_ref[...]
    tm = u_ref.shape[0]
    sub = tm // INPROJ_SUBTILES

    ri = lax.broadcasted_iota(jnp.int32, (LANES, LANES), 0) // HEAD_DIM
    ci = lax.broadcasted_iota(jnp.int32, (LANES, LANES), 1) // HEAD_DIM
    seg = jnp.where(ri == ci, 1.0, 0.0).astype(BF16)
    lane = lax.broadcasted_iota(jnp.int32, (sub, LANES), 1)
    even = (lane % 2) == 0
    qg = qg_ref[...]
    kg = kg_ref[...]

    def norm(xc, gain):
        ss = jnp.dot((xc * xc).astype(BF16), seg, preferred_element_type=F32)
        return xc * lax.rsqrt(ss * (1.0 / HEAD_DIM) + EPS) * gain

    for r in range(INPROJ_SUBTILES):
        rows = slice(r * sub, (r + 1) * sub)
        x = jnp.where(i < npt, xp_ref[rows, :], xs_ref[rows, :])
        h = (x * (1.0 + mod[1:2]) + mod[0:1]).astype(BF16)
        proj = jnp.dot(h, w_scr[...], preferred_element_type=F32)
        cos = cos_ref[rows, :]
        sin = sin_ref[rows, :]

        def rope(xn):
            partner = jnp.where(even, pltpu.roll(xn, LANES - 1, 1), pltpu.roll(xn, 1, 1))
            return xn * cos + partner * sin

        for c in range(ATT_W // LANES):
            qr = rope(norm(proj[:, c * LANES:(c + 1) * LANES], qg))
            qs = (qr * (HEAD_DIM ** -0.5 * LOG2E)).astype(BF16)
            q_ref[2 * c, rows, :] = qs[:, :HEAD_DIM]
            q_ref[2 * c + 1, rows, :] = qs[:, HEAD_DIM:]

        kn = norm(proj[:, ATT_W:ATT_W + KV_W], kg)
        kf_ref[rows, :] = kn
        kr = rope(kn).astype(BF16)
        k_ref[0, rows, :] = kr[:, :HEAD_DIM]
        k_ref[1, rows, :] = kr[:, HEAD_DIM:]

        v = proj[:, ATT_W + KV_W:ATT_W + 2 * KV_W]
        vf_ref[rows, :] = v
        vb = v.astype(BF16)
        v_ref[0, rows, :] = vb[:, :HEAD_DIM]
        v_ref[1, rows, :] = vb[:, HEAD_DIM:]

        off = ATT_W + 2 * KV_W
        a = proj[:, off:off + CONV_CH]
        gt = proj[:, off + CONV_CH:off + 2 * CONV_CH]
        u_ref[rows, :] = a * _sigmoid(gt)


def _rope_tables(tm):
    t = jnp.arange(DEC_SEQ)
    row = (t // GRID_W).astype(F32)
    col = (t % GRID_W).astype(F32)
    freqs = ROPE_THETA ** (-jnp.arange(0, ROPE_AXIS_DIM, 2, dtype=F32) / ROPE_AXIS_DIM)
    ang = jnp.concatenate([row[:, None] * freqs, col[:, None] * freqs], axis=-1)
    pair = (jnp.arange(LANES) % HEAD_DIM) // 2
    sign = jnp.where(jnp.arange(LANES) % 2 == 0, -1.0, 1.0).astype(F32)
    cos = jnp.cos(ang)[:, pair]
    sin = jnp.sin(ang)[:, pair] * sign
    cos = jnp.concatenate([jnp.ones((tm, LANES), F32), cos], axis=0)
    sin = jnp.concatenate([jnp.zeros((tm, LANES), F32), sin], axis=0)
    return cos, sin


def _inproj_a(xp, xs, mod, w, q_gain, k_gain):
    tm = TM_PROJ
    npt = NP // tm
    tps = DEC_SEQ // tm
    nt = NT // tm
    cos, sin = _rope_tables(tm)
    qg = jnp.tile(q_gain, LANES // HEAD_DIM).reshape(1, LANES)
    kg = jnp.tile(k_gain, LANES // HEAD_DIM).reshape(1, LANES)

    def rope_idx(i):
        return (jnp.where(i < npt, 0, 1 + (i - npt) % tps), 0)

    return pl.pallas_call(
        functools.partial(_inproj_a_kernel, npt=npt),
        out_shape=[
            jax.ShapeDtypeStruct((ATT_HEADS, NT, HEAD_DIM), BF16),
            jax.ShapeDtypeStruct((ATT_KV_HEADS, NT, HEAD_DIM), BF16),
            jax.ShapeDtypeStruct((ATT_KV_HEADS, NT, HEAD_DIM), BF16),
            jax.ShapeDtypeStruct((NT, KV_W), F32),
            jax.ShapeDtypeStruct((NT, KV_W), F32),
            jax.ShapeDtypeStruct((NT, CONV_CH), F32),
        ],
        grid=(nt,),
        in_specs=[
            pl.BlockSpec((tm, D_MODEL), lambda i: (jnp.minimum(i, npt - 1), 0)),
            pl.BlockSpec((tm, D_MODEL), lambda i: (jnp.maximum(i - npt, 0), 0)),
            pl.BlockSpec((None, 6, D_MODEL), lambda i: (_cond_index(i, tm), 0, 0)),
            pl.BlockSpec((None, D_MODEL, EVEN_IN), lambda i: (0, 0, 0)),
            pl.BlockSpec((1, LANES), lambda i: (0, 0)),
            pl.BlockSpec((1, LANES), lambda i: (0, 0)),
            pl.BlockSpec((tm, LANES), rope_idx),
            pl.BlockSpec((tm, LANES), rope_idx),
        ],
        out_specs=[
            pl.BlockSpec((ATT_HEADS, tm, HEAD_DIM), lambda i: (0, i, 0)),
            pl.BlockSpec((ATT_KV_HEADS, tm, HEAD_DIM), lambda i: (0, i, 0)),
            pl.BlockSpec((ATT_KV_HEADS, tm, HEAD_DIM), lambda i: (0, i, 0)),
            pl.BlockSpec((tm, KV_W), lambda i: (i, 0)),
            pl.BlockSpec((tm, KV_W), lambda i: (i, 0)),
            pl.BlockSpec((tm, CONV_CH), lambda i: (i, 0)),
        ],
        scratch_shapes=[pltpu.VMEM((D_MODEL, EVEN_IN), BF16)],
        compiler_params=_params(("arbitrary",)),
        name="inproj_a",
    )(xp, xs, mod, w, qg, kg, cos, sin)


def _attn_chunk(qs, kc, vc, m, acc):
    s = lax.dot_general(qs, kc, (((1,), (1,)), ((), ())), preferred_element_type=F32)
    fold = s[:, :LANES]
    for c in range(1, s.shape[1] // LANES):
        fold = jnp.maximum(fold, s[:, c * LANES:(c + 1) * LANES])
    m_new = jnp.maximum(m, jnp.max(fold, axis=1, keepdims=True))
    alpha = jnp.exp2(m - m_new)
    p = jnp.exp2(s - m_new).astype(BF16)
    va = jnp.concatenate([vc, jnp.ones((vc.shape[0], LANES - HEAD_DIM), BF16)], axis=1)
    acc = alpha * acc + jnp.dot(p, va, preferred_element_type=F32)
    return m_new, acc


def _attn_init(nq):
    return jnp.full((nq, 1), -1e30, F32), jnp.zeros((nq, LANES), F32)


def _attn_finish(acc, o_ref):
    tq = o_ref.shape[0]
    o = acc[:, :HEAD_DIM] * (1.0 / acc[:, HEAD_DIM:HEAD_DIM + 1])
    o_ref[...] = jnp.concatenate([o[h * tq:(h + 1) * tq] for h in range(ATT_GROUP)],
                                 axis=1).astype(BF16)


def _stack_heads(q_ref):
    return q_ref[...].reshape(ATT_GROUP * q_ref.shape[1], HEAD_DIM)


def _attn_prompt_kernel(q_ref, k_ref, v_ref, o_ref):
    qs = _stack_heads(q_ref)
    _, acc = _attn_chunk(qs, k_ref[...], v_ref[...], *_attn_init(qs.shape[0]))
    _attn_finish(acc, o_ref)


def _attn_sample_kernel(q_ref, k_ref, v_ref, ck_ref, cv_ref, o_ref, k_scr, v_scr, *, nchunks):
    @pl.when(pl.program_id(2) == 0)
    def _():
        own = k_ref.shape[0]
        k_scr[0:own, :] = k_ref[...]
        v_scr[0:own, :] = v_ref[...]
        first = pl.program_id(1) == 0
        ck = ck_ref[...]
        cv = cv_ref[...]
        k_scr[own:, :] = jnp.where(first, ck[:, :HEAD_DIM], ck[:, HEAD_DIM:]).astype(BF16)
        v_scr[own:, :] = jnp.where(first, cv[:, :HEAD_DIM], cv[:, HEAD_DIM:]).astype(BF16)

    qs = _stack_heads(q_ref)

    def body(j, carry):
        rows = pl.ds(pl.multiple_of(j * TK_ATT, TK_ATT), TK_ATT)
        return _attn_chunk(qs, k_scr[rows, :], v_scr[rows, :], *carry)

    _, acc = lax.fori_loop(0, nchunks, body, _attn_init(qs.shape[0]), unroll=ATT_UNROLL)
    _attn_finish(acc, o_ref)


def _attention_prompt(q, k, v):
    return pl.pallas_call(
        _attn_prompt_kernel,
        out_shape=jax.ShapeDtypeStruct((NP, ATT_W), BF16),
        grid=(BATCH, ATT_KV_HEADS),
        in_specs=[
            pl.BlockSpec((ATT_GROUP, SEQ, HEAD_DIM), lambda b, g: (g, b, 0)),
            pl.BlockSpec((None, SEQ, HEAD_DIM), lambda b, g: (g, b, 0)),
            pl.BlockSpec((None, SEQ, HEAD_DIM), lambda b, g: (g, b, 0)),
        ],
        out_specs=pl.BlockSpec((SEQ, ATT_GROUP * HEAD_DIM), lambda b, g: (b, g)),
        compiler_params=_params(("arbitrary", "arbitrary")),
        name="attn_prompt",
    )(q, k, v)


def _attention_sample(q, k, v, ctx_k, ctx_v):
    tq = TQ_ATT
    nkeys = DEC_SEQ + PAST_LEN
    nchunks = nkeys // TK_ATT
    q_off = NP // tq
    kv_off = NP // DEC_SEQ
    return pl.pallas_call(
        functools.partial(_attn_sample_kernel, nchunks=nchunks),
        out_shape=jax.ShapeDtypeStruct((NS, ATT_W), BF16),
        grid=(DEC_BATCH, ATT_KV_HEADS, DEC_SEQ // tq),
        in_specs=[
            pl.BlockSpec((ATT_GROUP, tq, HEAD_DIM),
                         lambda b, g, i: (g, q_off + b * (DEC_SEQ // tq) + i, 0)),
            pl.BlockSpec((None, DEC_SEQ, HEAD_DIM), lambda b, g, i: (g, kv_off + b, 0)),
            pl.BlockSpec((None, DEC_SEQ, HEAD_DIM), lambda b, g, i: (g, kv_off + b, 0)),
            pl.BlockSpec((None, PAST_LEN, KV_W), lambda b, g, i: (b, 0, 0)),
            pl.BlockSpec((None, PAST_LEN, KV_W), lambda b, g, i: (b, 0, 0)),
        ],
        out_specs=pl.BlockSpec((tq, ATT_GROUP * HEAD_DIM),
                               lambda b, g, i: (b * (DEC_SEQ // tq) + i, g)),
        scratch_shapes=[pltpu.VMEM((nkeys, HEAD_DIM), BF16), pltpu.VMEM((nkeys, HEAD_DIM), BF16)],
        compiler_params=_params(("arbitrary", "arbitrary", "arbitrary")),
        name="attn_sample",
    )(q, k, v, ctx_k, ctx_v)


def _conv_kernel(prev_ref, cur_ref, next_ref, w_ref, b_ref, g_ref, bb_ref, o_ref, win_ref, y_ref,
                 *, npt, tps):
    r = cur_ref.shape[0]
    i = pl.program_id(0)
    j = (i - npt) % tps
    first = jnp.logical_or(i < npt, j == 0)
    last = jnp.logical_or(i < npt, j == tps - 1)
    win_ref[0:HALO, :] = jnp.where(first, 0.0, prev_ref[...])
    win_ref[HALO:HALO + r, :] = cur_ref[...]
    win_ref[HALO + r:2 * HALO + r, :] = jnp.where(last, 0.0, next_ref[...])
    base = HALO - CONV_K // 2
    sub = 8
    for c in range(CONV_CH // LANES):
        cs = slice(c * LANES, (c + 1) * LANES)
        acc = None
        for res in range(sub):
            p = None
            for k in range(CONV_K):
                if (base + k) % sub != res:
                    continue
                a = (base + k) - res
                term = win_ref[a:a + r + sub, cs] * w_ref[k:k + 1, cs]
                p = term if p is None else p + term
            if p is None:
                continue
            p = p[res:res + r]
            acc = p if acc is None else acc + p
        y_ref[:, cs] = acc + b_ref[:, cs]
    y = _layernorm(y_ref[...], g_ref[...], bb_ref[...])
    o_ref[...] = (y * _sigmoid(y)).astype(BF16)


def _conv_module(u, conv_w, conv_b, cln_g, cln_b):
    r = R_CONV
    npt = NP // r
    tps = DEC_SEQ // r
    hb = r // HALO
    nh = NT // HALO
    w = jnp.concatenate([conv_w, jnp.zeros((1, CONV_CH), F32)], axis=0)
    return pl.pallas_call(
        functools.partial(_conv_kernel, npt=npt, tps=tps),
        out_shape=jax.ShapeDtypeStruct((NT, CONV_CH), BF16),
        grid=(NT // r,),
        in_specs=[
            pl.BlockSpec((HALO, CONV_CH), lambda i: (jnp.maximum(i * hb - 1, 0), 0)),
            pl.BlockSpec((r, CONV_CH), lambda i: (i, 0)),
            pl.BlockSpec((HALO, CONV_CH), lambda i: (jnp.minimum((i + 1) * hb, nh - 1), 0)),
            pl.BlockSpec((CONV_K + 1, CONV_CH), lambda i: (0, 0)),
            pl.BlockSpec((1, CONV_CH), lambda i: (0, 0)),
            pl.BlockSpec((1, CONV_CH), lambda i: (0, 0)),
            pl.BlockSpec((1, CONV_CH), lambda i: (0, 0)),
        ],
        out_specs=pl.BlockSpec((r, CONV_CH), lambda i: (i, 0)),
        scratch_shapes=[pltpu.VMEM((r + 2 * HALO, CONV_CH), F32), pltpu.VMEM((r, CONV_CH), F32)],
        compiler_params=_params(("arbitrary",)),
        name="conv_module",
    )(u, u, u, w, conv_b.reshape(1, CONV_CH), cln_g.reshape(1, CONV_CH), cln_b.reshape(1, CONV_CH))


def _outproj_a_kernel(ap_ref, as_ref, uc_ref, xp_ref, xs_ref, mod_ref, w_ref, g_ref, b_ref, o_ref,
                      w_scr, *, npt):
    i = pl.program_id(0)
    _cast_once(w_ref, w_scr)
    mod = mod_ref[...]
    sub = o_ref.shape[0] // OUTPROJ_SUBTILES
    for r in range(OUTPROJ_SUBTILES):
        rows = slice(r * sub, (r + 1) * sub)
        att = jnp.where(i < npt, ap_ref[rows, :], as_ref[rows, :])
        x = jnp.where(i < npt, xp_ref[rows, :], xs_ref[rows, :])
        y = jnp.dot(att, w_scr[0:ATT_W, :], preferred_element_type=F32)
        y = y + jnp.dot(uc_ref[rows, :], w_scr[ATT_W:ATT_W + CONV_CH, :], preferred_element_type=F32)
        o_ref[rows, :] = _layernorm(ALPHA * x + mod[2:3] * y, g_ref[...], b_ref[...])


def _outproj_a(att_p, att_s, uc, xp, xs, mod, w, g, b):
    tm = TM_PROJ
    npt = NP // tm
    first = lambda i: (jnp.minimum(i, npt - 1), 0)
    second = lambda i: (jnp.maximum(i - npt, 0), 0)
    return pl.pallas_call(
        functools.partial(_outproj_a_kernel, npt=npt),
        out_shape=jax.ShapeDtypeStruct((NT, D_MODEL), F32),
        grid=(NT // tm,),
        in_specs=[
            pl.BlockSpec((tm, ATT_W), first),
            pl.BlockSpec((tm, ATT_W), second),
            pl.BlockSpec((tm, CONV_CH), lambda i: (i, 0)),
            pl.BlockSpec((tm, D_MODEL), first),
            pl.BlockSpec((tm, D_MODEL), second),
            pl.BlockSpec((None, 6, D_MODEL), lambda i: (_cond_index(i, tm), 0, 0)),
            pl.BlockSpec((None, ATT_W + CONV_CH, D_MODEL), lambda i: (0, 0, 0)),
            pl.BlockSpec((1, D_MODEL), lambda i: (0, 0)),
            pl.BlockSpec((1, D_MODEL), lambda i: (0, 0)),
        ],
        out_specs=pl.BlockSpec((tm, D_MODEL), lambda i: (i, 0)),
        scratch_shapes=[pltpu.VMEM((ATT_W + CONV_CH, D_MODEL), BF16)],
        compiler_params=_params(("arbitrary",)),
        name="outproj_a",
    )(att_p, att_s, uc, xp, xs, mod, w, g, b)


def _outproj_m_kernel(hp_ref, lo_ref, hi_ref, x_ref, mod_ref, w_ref, g_ref, b_ref, o_ref, w_scr,
                      *, npt, tps):
    i = pl.program_id(0)
    _cast_once(w_ref, w_scr)
    upper = ((i - npt) % tps) >= tps // 2
    mod = mod_ref[...]
    sub = o_ref.shape[0] // OUTPROJ_SUBTILES
    for r in range(OUTPROJ_SUBTILES):
        rows = slice(r * sub, (r + 1) * sub)
        hg = jnp.where(i < npt, hp_ref[rows, :], jnp.where(upper, hi_ref[rows, :], lo_ref[rows, :]))
        y = jnp.dot(hg, w_scr[...], preferred_element_type=F32)
        o_ref[rows, :] = _layernorm(ALPHA * x_ref[rows, :] + mod[2:3] * y, g_ref[...], b_ref[...])


def _outproj_m(hg_p, hg_lo, hg_hi, x, mod, w, g, b):
    tm = TM_PROJ
    npt = NP // tm
    tps = DEC_SEQ // tm
    hps = tps // 2

    def half_idx(i, upper):
        s = jnp.maximum(i - npt, 0)
        t = s % tps
        t = jnp.maximum(t - hps, 0) if upper else jnp.minimum(t, hps - 1)
        return ((s // tps) * hps + t, 0)

    return pl.pallas_call(
        functools.partial(_outproj_m_kernel, npt=npt, tps=tps),
        out_shape=jax.ShapeDtypeStruct((NT, D_MODEL), F32),
        grid=(NT // tm,),
        in_specs=[
            pl.BlockSpec((tm, M_INNER), lambda i: (jnp.minimum(i, npt - 1), 0)),
            pl.BlockSpec((tm, M_INNER), lambda i: half_idx(i, False)),
            pl.BlockSpec((tm, M_INNER), lambda i: half_idx(i, True)),
            pl.BlockSpec((tm, D_MODEL), lambda i: (i, 0)),
            pl.BlockSpec((None, 6, D_MODEL), lambda i: (_cond_index(i, tm), 0, 0)),
            pl.BlockSpec((None, M_INNER, D_MODEL), lambda i: (0, 0, 0)),
            pl.BlockSpec((1, D_MODEL), lambda i: (0, 0)),
            pl.BlockSpec((1, D_MODEL), lambda i: (0, 0)),
        ],
        out_specs=pl.BlockSpec((tm, D_MODEL), lambda i: (i, 0)),
        scratch_shapes=[pltpu.VMEM((M_INNER, D_MODEL), BF16)],
        compiler_params=_params(("arbitrary",)),
        name="outproj_m",
    )(hg_p, hg_lo, hg_hi, x, mod, w, g, b)


def _ffn_kernel(x_ref, mod_ref, w1_ref, w2_ref, g_ref, b_ref, o_ref, w1_scr, w2_scr, *, ncast):
    s = pl.program_id(0)

    @pl.when(s < ncast)
    def _():
        w1_scr[s] = w1_ref[...].astype(BF16)
        w2_scr[s] = w2_ref[...].astype(BF16)

    @pl.when(s >= ncast)
    def _():
        mod = mod_ref[...]
        half = x_ref.shape[0] // 2
        for r in range(2):
            rows = slice(r * half, (r + 1) * half)
            x = x_ref[rows, :]
            h = (x * (1.0 + mod[4:5]) + mod[3:4]).astype(BF16)
            acc = None
            for c in range(ncast):
                a = jnp.maximum(jnp.dot(h, w1_scr[c], preferred_element_type=F32), 0.0)
                y = jnp.dot((a * a).astype(BF16), w2_scr[c], preferred_element_type=F32)
                acc = y if acc is None else acc + y
            o_ref[rows, :] = _layernorm(ALPHA * x + mod[5:6] * acc, g_ref[...], b_ref[...])


def _ffn(x, row0, nrows, mod, w1, w2, layer, g, b):
    tm, tf = TM_PROJ, TF_FFN
    off = row0 // tm
    ncast = D_FF // tf
    row = lambda s: jnp.maximum(s - ncast, 0)
    blk = lambda s: jnp.minimum(s, ncast - 1)
    return pl.pallas_call(
        functools.partial(_ffn_kernel, ncast=ncast),
        out_shape=jax.ShapeDtypeStruct((nrows, D_MODEL), F32),
        grid=(ncast + nrows // tm,),
        in_specs=[
            pl.BlockSpec((tm, D_MODEL), lambda s: (row(s) + off, 0)),
            pl.BlockSpec((None, 6, D_MODEL), lambda s: (_cond_index(row(s) + off, tm), 0, 0)),
            pl.BlockSpec((None, D_MODEL, tf), lambda s: (layer, 0, blk(s))),
            pl.BlockSpec((None, tf, D_MODEL), lambda s: (layer, blk(s), 0)),
            pl.BlockSpec((1, D_MODEL), lambda s: (0, 0)),
            pl.BlockSpec((1, D_MODEL), lambda s: (0, 0)),
        ],
        out_specs=pl.BlockSpec((tm, D_MODEL), lambda s: (row(s), 0)),
        scratch_shapes=[pltpu.VMEM((ncast, D_MODEL, tf), BF16), pltpu.VMEM((ncast, tf, D_MODEL), BF16)],
        compiler_params=_params(("arbitrary",)),
        name="ffn",
    )(x, mod, w1, w2, g, b)


def _inproj_m_kernel(x_ref, mod_ref, w_ref, wg_ref, bg_ref, o_ref, gt_ref, vt_ref):
    nslab = vt_ref.shape[0]
    L = L_CHUNK
    mod = mod_ref[...]
    h = (x_ref[...] * (1.0 + mod[1:2]) + mod[0:1]).astype(BF16)

    gates = jnp.dot(h, wg_ref[...], preferred_element_type=F32) + bg_ref[...]
    kind = lax.broadcasted_iota(jnp.int32, gates.shape, 1) % GATE_STRIDE
    gates = jnp.where((kind == 2) | (kind == 3), _log_sigmoid(gates), gates)
    gates_t = gates.T
    for s in range(nslab):
        gt_ref[s] = gates_t[0:GATE_ROWS, s * L:(s + 1) * L]

    for j in range(4):
        cols = slice(j * M_INNER, (j + 1) * M_INNER)
        p = jnp.dot(h, w_ref[:, cols], preferred_element_type=F32)
        if j == 1:
            p = p * (M_DK ** -0.5)
        o_ref[:, cols] = p.astype(BF16)
        if j == 2:
            v_t = p.T.astype(BF16)
            for s in range(nslab):
                vt_ref[s] = v_t[:, s * L:(s + 1) * L]


def _inproj_m(x, mod, w, wg, bg):
    tm = TM_PROJ
    nslab = tm // L_CHUNK
    return pl.pallas_call(
        _inproj_m_kernel,
        out_shape=[
            jax.ShapeDtypeStruct((NT, 4 * M_INNER), BF16),
            jax.ShapeDtypeStruct((NT // L_CHUNK, GATE_ROWS, L_CHUNK), F32),
            jax.ShapeDtypeStruct((NT // L_CHUNK, M_INNER, L_CHUNK), BF16),
        ],
        grid=(NT // tm,),
        in_specs=[
            pl.BlockSpec((tm, D_MODEL), lambda i: (i, 0)),
            pl.BlockSpec((None, 6, D_MODEL), lambda i: (_cond_index(i, tm), 0, 0)),
            pl.BlockSpec((D_MODEL, 4 * M_INNER), lambda i: (0, 0)),
            pl.BlockSpec((D_MODEL, LANES), lambda i: (0, 0)),
            pl.BlockSpec((1, LANES), lambda i: (0, 0)),
        ],
        out_specs=[
            pl.BlockSpec((tm, 4 * M_INNER), lambda i: (i, 0)),
            pl.BlockSpec((nslab, GATE_ROWS, L_CHUNK), lambda i: (i, 0, 0)),
            pl.BlockSpec((nslab, M_INNER, L_CHUNK), lambda i: (i, 0, 0)),
        ],
        compiler_params=_params(("arbitrary",)),
        name="inproj_m",
    )(x, mod, w, wg, bg)


def _log_sigmoid(x):
    return jnp.minimum(x, 0.0) - jnp.log1p(jnp.exp(-jnp.abs(x)))


def _split3(x):
    x1 = x.astype(BF16)
    r1 = x - x1.astype(F32)
    x2 = r1.astype(BF16)
    x3 = (r1 - x2.astype(F32)).astype(BF16)
    return x1, x2, x3


def _mlstm_prefix(gt):
    L = gt.shape[1]
    parts = jnp.concatenate([t.astype(F32) for t in _split3(gt)]
                            + [jnp.zeros((GATE_ROWS, L), F32)], axis=0).astype(BF16)
    upper = jnp.where(lax.broadcasted_iota(jnp.int32, (L, L), 0)
                      <= lax.broadcasted_iota(jnp.int32, (L, L), 1), 1.0, 0.0).astype(BF16)
    r = jnp.dot(parts, upper, preferred_element_type=F32)
    return r[0:GATE_ROWS] + r[GATE_ROWS:2 * GATE_ROWS] + r[2 * GATE_ROWS:3 * GATE_ROWS]


def _rows3(x, n):
    g = x.shape[0]
    return jnp.concatenate([t.astype(F32) for t in _split3(x)]
                           + [jnp.zeros((g, ONES_ROWS - 3, n), F32)], axis=1).astype(BF16)


def _sum3(x, r0):
    return x[:, r0:r0 + 1] + x[:, r0 + 1:r0 + 2] + x[:, r0 + 2:r0 + 3]


def _mlstm_group(rev, q, k, vt, gt, pre, state):
    G, L = q.shape[0], q.shape[1]
    d = 1 if rev else 0
    tpos = lax.broadcasted_iota(jnp.int32, (L, L), 0)
    spos = lax.broadcasted_iota(jnp.int32, (L, L), 1)
    keep = (tpos >= spos) if rev else (tpos <= spos)
    if rev:
        tot = pre[:, 3:4, L - 1:L]
        a_row = tot - pre[:, 3:4, :] + gt[:, 3:4, :]
    else:
        a_row = pre[:, 2:3, :]
        tot = a_row[:, :, L - 1:L]
    b_row = gt[:, d:d + 1, :] - a_row
    m = jnp.zeros((G, 1, 1), F32) if state is None else state[0]

    gp = -(-G // 8) * 8
    rows = [b_row[p] for p in range(G)] + [jnp.zeros((gp - G, L), F32)] * (1 if gp > G else 0)
    bt = jnp.concatenate(rows, axis=0).T
    bm = jnp.stack([jnp.where(keep, bt[:, p:p + 1], -jnp.inf) for p in range(G)])
    mm = jnp.maximum(m, jnp.max(bm, axis=1, keepdims=True))
    w = jnp.exp(bm - mm)
    s = jnp.einsum('gsd,gtd->gst', k, q, preferred_element_type=F32)
    qk = s * w
    num = jnp.einsum('gvs,gst->gvt', vt, qk.astype(BF16), preferred_element_type=F32)
    den = jnp.sum(qk, axis=1, keepdims=True)
    if state is not None:
        _, ct, n = state
        cn = jnp.concatenate([ct.astype(BF16), _rows3(n, n.shape[2])], axis=1)
        inter = jnp.einsum('gcd,gtd->gct', cn, q, preferred_element_type=F32)
        s_inter = jnp.exp(m - mm)
        num = num + s_inter * inter[:, :M_DV]
        den = den + s_inter * _sum3(inter, M_DV)
    h = num * (1.0 / jnp.maximum(jnp.abs(den), jnp.exp(-(a_row + mm))))

    g_row = tot + b_row
    m_new = jnp.maximum(tot + m, jnp.max(g_row, axis=2, keepdims=True))
    ws = jnp.exp(g_row - m_new)
    lhs = jnp.concatenate([(vt.astype(F32) * ws).astype(BF16), _rows3(ws, L)], axis=1)
    upd = jnp.einsum('gcs,gsd->gcd', lhs, k, preferred_element_type=F32)
    ct_new = upd[:, :M_DV]
    n_new = _sum3(upd, M_DV)
    if state is not None:
        decay = jnp.exp(tot + m - m_new)
        ct_new = decay * ct + ct_new
        n_new = decay * n + n_new
    return h, (m_new, ct_new, n_new)


def _heads(x):
    w = x.shape[1] // M_HEADS
    return jnp.stack([x[:, h * w:(h + 1) * w] for h in range(M_HEADS)])


def _mlstm_finish(ht, o, gain):
    ms = jnp.mean(ht * ht, axis=1, keepdims=True)
    hn = ht * lax.rsqrt(ms + EPS)
    hn = jnp.concatenate([hn[h].T for h in range(M_HEADS)], axis=1)
    return (_sigmoid(o.astype(F32)) * hn * gain).astype(BF16)


def _mlstm_sample_kernel(qf_ref, kf_ref, vtf_ref, gtf_ref, of_ref, qb_ref, kb_ref, vtb_ref, gtb_ref,
                         ob_ref, gain_ref, c0_ref, n0_ref, m0_ref, hi_ref, lo_ref,
                         hs_ref, ct_scr, n_scr, m_scr, *, nc):
    j = pl.program_id(1)
    jb = nc - 1 - j
    H = M_HEADS
    L = L_CHUNK

    @pl.when(j == 0)
    def _():
        for d in range(2):
            for h in range(H):
                ct_scr[d * H + h] = c0_ref[d, h].T
                n_scr[d * H + h] = n0_ref[d, h]
                m_scr[d * H + h] = jnp.broadcast_to(m0_ref[d, h], (1, LANES))

    def run(rev, q_ref, k_ref, vt_ref, gt_ref, lo):
        gt = gt_ref[...]
        pre = _mlstm_prefix(gt).reshape(H, GATE_STRIDE, L)
        state = (m_scr[lo:lo + H][:, :, 0:1], ct_scr[lo:lo + H], n_scr[lo:lo + H])
        h, (m_new, ct_new, n_new) = _mlstm_group(
            rev, _heads(q_ref[...]), _heads(k_ref[...]), vt_ref[...].reshape(H, M_DV, L),
            gt.reshape(H, GATE_STRIDE, L), pre, state)
        ct_scr[lo:lo + H] = ct_new
        n_scr[lo:lo + H] = n_new
        m_scr[lo:lo + H] = jnp.broadcast_to(m_new, (H, 1, LANES))
        return h

    h_f = run(False, qf_ref, kf_ref, vtf_ref, gtf_ref, 0)
    h_b = run(True, qb_ref, kb_ref, vtb_ref, gtb_ref, H)

    @pl.when(j < nc // 2)
    def _():
        hs_ref[j] = h_f
        hs_ref[jb] = h_b

    @pl.when(j >= nc // 2)
    def _():
        gain = gain_ref[...]
        hi_ref[...] = _mlstm_finish(hs_ref[j] + h_f, of_ref[...], gain)
        lo_ref[...] = _mlstm_finish(hs_ref[jb] + h_b, ob_ref[...], gain)


def _mlstm_sample(qkvo, vt, gates_t, gain, init):
    nc = DEC_SEQ // L_CHUNK
    half = nc // 2
    c0 = NP // L_CHUNK
    L = L_CHUNK
    fwd = lambda b, j: c0 + b * nc + j
    bwd = lambda b, j: c0 + b * nc + nc - 1 - j

    def specs(f):
        return [
            pl.BlockSpec((L, M_INNER), lambda b, j: (f(b, j), 0)),
            pl.BlockSpec((L, M_INNER), lambda b, j: (f(b, j), 1)),
            pl.BlockSpec((None, M_INNER, L), lambda b, j: (f(b, j), 0, 0)),
            pl.BlockSpec((None, GATE_ROWS, L), lambda b, j: (f(b, j), 0, 0)),
            pl.BlockSpec((L, M_INNER), lambda b, j: (f(b, j), 3)),
        ]

    state_specs = [
        pl.BlockSpec((None, 2, M_HEADS, M_DK, M_DV), lambda b, j: (b, 0, 0, 0, 0)),
        pl.BlockSpec((None, 2, M_HEADS, 1, M_DK), lambda b, j: (b, 0, 0, 0, 0)),
        pl.BlockSpec((None, 2, M_HEADS, 1, 1), lambda b, j: (b, 0, 0, 0, 0)),
    ]
    out_rows = DEC_BATCH * DEC_SEQ // 2
    return pl.pallas_call(
        functools.partial(_mlstm_sample_kernel, nc=nc),
        out_shape=[jax.ShapeDtypeStruct((out_rows, M_INNER), BF16)] * 2,
        grid=(DEC_BATCH, nc),
        in_specs=specs(fwd) + specs(bwd) + [pl.BlockSpec((1, M_INNER), lambda b, j: (0, 0))] + state_specs,
        out_specs=[
            pl.BlockSpec((L, M_INNER), lambda b, j: (b * half + jnp.maximum(j, half) - half, 0)),
            pl.BlockSpec((L, M_INNER), lambda b, j: (b * half + jnp.minimum(nc - 1 - j, half - 1), 0)),
        ],
        scratch_shapes=[
            pltpu.VMEM((nc, M_HEADS, M_DV, L), F32),
            pltpu.VMEM((2 * M_HEADS, M_DV, M_DK), F32),
            pltpu.VMEM((2 * M_HEADS, 1, M_DK), F32),
            pltpu.VMEM((2 * M_HEADS, 1, LANES), F32),
        ],
        compiler_params=_params(("arbitrary", "arbitrary")),
        name="mlstm",
    )(qkvo, qkvo, vt, gates_t, qkvo, qkvo, qkvo, vt, gates_t, qkvo, gain, *init)


def _mlstm_prompt_kernel(q_ref, k_ref, vt_ref, gt_ref, o_ref, gain_ref, out_ref, co_ref, no_ref, mo_ref,
                         *, nseq):
    H = M_HEADS
    L = L_CHUNK
    G = nseq * H
    q = jnp.concatenate([_heads(q_ref[b * L:(b + 1) * L, :]) for b in range(nseq)], axis=0)
    k = jnp.concatenate([_heads(k_ref[b * L:(b + 1) * L, :]) for b in range(nseq)], axis=0)
    vt = vt_ref[...].reshape(G, M_DV, L)
    gt3 = jnp.concatenate([gt_ref[b] for b in range(nseq)], axis=0).reshape(G, GATE_STRIDE, L)
    pre3 = jnp.concatenate([_mlstm_prefix(gt_ref[b]) for b in range(nseq)],
                           axis=0).reshape(G, GATE_STRIDE, L)
    gain = gain_ref[...]
    hs = None
    for d in range(2):
        h, (m_new, ct_new, n_new) = _mlstm_group(d == 1, q, k, vt, gt3, pre3, None)
        hs = h if hs is None else hs + h
        for b in range(nseq):
            for hh in range(H):
                co_ref[b, d, hh] = ct_new[b * H + hh].T
                no_ref[b, d, hh] = n_new[b * H + hh]
                mo_ref[b, d, hh] = m_new[b * H + hh]
    for b in range(nseq):
        out_ref[b * L:(b + 1) * L, :] = _mlstm_finish(hs[b * H:(b + 1) * H], o_ref[b * L:(b + 1) * L, :], gain)


def _mlstm_prompt(qkvo, vt, gates_t, gain):
    nseq = MLSTM_SEQS
    L = L_CHUNK
    rows = nseq * L
    state_specs = [
        pl.BlockSpec((nseq, 2, M_HEADS, M_DK, M_DV), lambda i: (i, 0, 0, 0, 0)),
        pl.BlockSpec((nseq, 2, M_HEADS, 1, M_DK), lambda i: (i, 0, 0, 0, 0)),
        pl.BlockSpec((nseq, 2, M_HEADS, 1, 1), lambda i: (i, 0, 0, 0, 0)),
    ]
    return pl.pallas_call(
        functools.partial(_mlstm_prompt_kernel, nseq=nseq),
        out_shape=[
            jax.ShapeDtypeStruct((NP, M_INNER), BF16),
            jax.ShapeDtypeStruct((BATCH, 2, M_HEADS, M_DK, M_DV), F32),
            jax.ShapeDtypeStruct((BATCH, 2, M_HEADS, 1, M_DK), F32),
            jax.ShapeDtypeStruct((BATCH, 2, M_HEADS, 1, 1), F32),
        ],
        grid=(BATCH // nseq,),
        in_specs=[
            pl.BlockSpec((rows, M_INNER), lambda i: (i, 0)),
            pl.BlockSpec((rows, M_INNER), lambda i: (i, 1)),
            pl.BlockSpec((nseq, M_INNER, L), lambda i: (i, 0, 0)),
            pl.BlockSpec((nseq, GATE_ROWS, L), lambda i: (i, 0, 0)),
            pl.BlockSpec((rows, M_INNER), lambda i: (i, 3)),
            pl.BlockSpec((1, M_INNER), lambda i: (0, 0)),
        ],
        out_specs=[pl.BlockSpec((rows, M_INNER), lambda i: (i, 0))] + state_specs,
        compiler_params=_params(("arbitrary",)),
        name="mlstm_state",
    )(qkvo, qkvo, vt, gates_t, qkvo, gain)


@jax.jit
def kernel(x_prompt, x_sample, cache_k, cache_v, state_c, state_n, state_m, c, c_ctx, w_ada, b_ada,
           ln_g, ln_b, w_ff1, w_ff2, w_in_a, q_gain, k_gain, conv_w, conv_b, conv_ln_g, conv_ln_b,
           w_out_a, w_in_m, b_gate_m, mh_gain, w_out_m):
    xp = x_prompt.reshape(NP, D_MODEL)
    xs = x_sample.reshape(NS, D_MODEL)

    cond = jnp.concatenate(
        [c_ctx[None, :], c, jnp.zeros((N_COND - 1 - DEC_BATCH, D_MODEL), F32)], axis=0)
    mods = _adaln(cond, w_ada, b_ada)

    ln = lambda l, s: (ln_g[l, s].reshape(1, D_MODEL), ln_b[l, s].reshape(1, D_MODEL))

    q, k, v, kf, vf, u = _inproj_a(xp, xs, mods[0], w_in_a, q_gain[0], k_gain[0])
    att_p = _attention_prompt(q, k, v)
    ctx_k = cache_k[:, 0].reshape(DEC_BATCH, PAST_LEN, KV_W)
    ctx_v = cache_v[:, 0].reshape(DEC_BATCH, PAST_LEN, KV_W)
    att_s = _attention_sample(q, k, v, ctx_k, ctx_v)
    uc = _conv_module(u, conv_w[0], conv_b[0], conv_ln_g[0], conv_ln_b[0])
    x1 = _outproj_a(att_p, att_s, uc, xp, xs, mods[0], w_out_a, *ln(0, 0))
    x2 = _ffn(x1, 0, NT, mods[0], w_ff1, w_ff2, 0, *ln(0, 1))

    wm = w_in_m[0]
    src = jnp.array([4 * M_INNER + g * M_HEADS + h
                     for h in range(M_HEADS) for g in (0, 2, 1, 3)], jnp.int32)
    dst = jnp.array([h * GATE_STRIDE + t for h in range(M_HEADS) for t in range(4)], jnp.int32)
    wg = jnp.zeros((D_MODEL, LANES), F32).at[:, dst].set(wm[:, src]).astype(BF16)
    bg = jnp.zeros((1, LANES), F32).at[0, dst].set(b_gate_m[0][src - 4 * M_INNER])
    qkvo, gates_t, vt_m = _inproj_m(x2, mods[1], wm[:, :4 * M_INNER].astype(BF16), wg, bg)
    gain = mh_gain[0].reshape(1, M_INNER)
    hg_p, st_c, st_n, st_m = _mlstm_prompt(qkvo, vt_m, gates_t, gain)
    init = (state_c[:, 0], state_n[:, 0].reshape(DEC_BATCH, 2, M_HEADS, 1, M_DK),
            state_m[:, 0].reshape(DEC_BATCH, 2, M_HEADS, 1, 1))
    hg_hi, hg_lo = _mlstm_sample(qkvo, vt_m, gates_t, gain, init)
    x3 = _outproj_m(hg_p, hg_lo, hg_hi, x2, mods[1], w_out_m, *ln(1, 0))
    y_p = _ffn(x3, 0, NP, mods[1], w_ff1, w_ff2, 1, *ln(1, 1))
    y_s = _ffn(x3, NP, NS, mods[1], w_ff1, w_ff2, 1, *ln(1, 1))

    new_k = kf[:NP].reshape(BATCH, 1, SEQ, ATT_KV_HEADS, HEAD_DIM)
    new_v = vf[:NP].reshape(BATCH, 1, SEQ, ATT_KV_HEADS, HEAD_DIM)
    return (y_p.reshape(BATCH, SEQ, D_MODEL), y_s.reshape(DEC_BATCH, DEC_SEQ, D_MODEL),
            new_k, new_v,
            st_c.reshape(BATCH, 1, 2, M_HEADS, M_DK, M_DV),
            st_n.reshape(BATCH, 1, 2, M_HEADS, M_DK),
            st_m.reshape(BATCH, 1, 2, M_HEADS))
```

```python
import functools

import jax
import jax.numpy as jnp
from jax import lax
from jax.experimental import pallas as pl
from jax.experimental.pallas import tpu as pltpu

F32 = jnp.float32
BF16 = jnp.bfloat16

D_MODEL = 1024
BATCH = 32
SEQ = 256
DEPTH = 2
DEC_BATCH = 4
DEC_SEQ = 4096
PAST_LEN = 512
GRID_W = 64
ATT_HEADS = 8
ATT_KV_HEADS = 2
HEAD_DIM = 64
ATT_GROUP = ATT_HEADS // ATT_KV_HEADS
ATT_W = ATT_HEADS * HEAD_DIM
KV_W = ATT_KV_HEADS * HEAD_DIM
ROPE_AXIS_DIM = HEAD_DIM // 2
ROPE_THETA = 10000.0
CONV_CH = D_MODEL // 2
CONV_K = 31
EVEN_IN = ATT_W + 2 * KV_W + 2 * CONV_CH
M_HEADS = 4
M_INNER = D_MODEL
M_DK = M_INNER // M_HEADS
M_DV = M_INNER // M_HEADS
D_FF = 4 * D_MODEL
ALPHA = (2 * DEPTH) ** 0.25
EPS = 1e-6
LOG2E = 1.4426950408889634

NP = BATCH * SEQ
NS = DEC_BATCH * DEC_SEQ
NT = NP + NS
N_COND = 8

LANES = 128
VMEM_LIMIT = 56 * 1024 * 1024

TM_PROJ = 512
TF_FFN = 1024
TQ_ATT = 256
TK_ATT = 512
R_CONV = 256
HALO = 16
L_CHUNK = 256
GATE_STRIDE = 8
ONES_ROWS = 16
GATE_ROWS = 32
MLSTM_SEQS = 4
TAIL_CAST_STEPS = 16
INPROJ_SUBTILES = 4
ATT_UNROLL = 9


def _cond_index(i_global, tm):
    npt = NP // tm
    tps = DEC_SEQ // tm
    return jnp.where(i_global < npt, 0, 1 + (i_global - npt) // tps)


def _layernorm(r, g, b):
    mu = jnp.mean(r, axis=-1, keepdims=True)
    d = r - mu
    var = jnp.mean(d * d, axis=-1, keepdims=True)
    return d * lax.rsqrt(var + EPS) * g + b


def _sigmoid(x):
    return 1.0 / (1.0 + jnp.exp(-x))


def _params(sem, vmem=VMEM_LIMIT, flags=None):
    return pltpu.CompilerParams(dimension_semantics=sem, vmem_limit_bytes=vmem, flags=flags)


def _adaln_kernel(cond_ref, w_ref, b_ref, o_ref):
    c = cond_ref[...]
    s = (c * _sigmoid(c)).astype(BF16)
    o_ref[...] = jnp.dot(s, w_ref[...].astype(BF16), preferred_element_type=F32) + b_ref[...]


def _adaln(cond, w_ada, b_ada):
    tn = 1536
    n = 6 * D_MODEL
    out = pl.pallas_call(
        _adaln_kernel,
        out_shape=jax.ShapeDtypeStruct((DEPTH, N_COND, n), F32),
        grid=(DEPTH, n // tn),
        in_specs=[
            pl.BlockSpec((N_COND, D_MODEL), lambda l, j: (0, 0)),
            pl.BlockSpec((None, D_MODEL, tn), lambda l, j: (l, 0, j)),
            pl.BlockSpec((None, 1, tn), lambda l, j: (l, 0, j)),
        ],
        out_specs=pl.BlockSpec((None, N_COND, tn), lambda l, j: (l, 0, j)),
        compiler_params=_params(("arbitrary", "arbitrary")),
        name="adaln",
    )(cond, w_ada, b_ada.reshape(DEPTH, 1, n))
    return out.reshape(DEPTH, N_COND, 6, D_MODEL)


def _cast_once(w_ref, w_scr):
    @pl.when(pl.program_id(0) == 0)
    def _():
        w_scr[...] = w_ref[...].astype(BF16)


def _inproj_a_kernel(xp_ref, xs_ref, mod_ref, w_ref, qg_ref, kg_ref, cos_ref, sin_ref,
                     q_ref, k_ref, v_ref, kf_ref, vf_ref, u_ref, w_scr, *, npt):
    i = pl.program_id(0)
    _cast_once(w_ref, w_scr)
    mod = mod_ref[...]
    tm = u_ref.shape[0]
    sub = tm // INPROJ_SUBTILES

    ri = lax.broadcasted_iota(jnp.int32, (LANES, LANES), 0) // HEAD_DIM
    ci = lax.broadcasted_iota(jnp.int32, (LANES, LANES), 1) // HEAD_DIM
    seg = jnp.where(ri == ci, 1.0, 0.0).astype(BF16)
    lane = lax.broadcasted_iota(jnp.int32, (sub, LANES), 1)
    even = (lane % 2) == 0
    qg = qg_ref[...]
    kg = kg_ref[...]

    def norm(xc, gain):
        ss = jnp.dot((xc * xc).astype(BF16), seg, preferred_element_type=F32)
        return xc * lax.rsqrt(ss * (1.0 / HEAD_DIM) + EPS) * gain

    for r in range(INPROJ_SUBTILES):
        rows = slice(r * sub, (r + 1) * sub)
        x = jnp.where(i < npt, xp_ref[rows, :], xs_ref[rows, :])
        h = (x * (1.0 + mod[1:2]) + mod[0:1]).astype(BF16)
        proj = jnp.dot(h, w_scr[...], preferred_element_type=F32)
        cos = cos_ref[rows, :]
        sin = sin_ref[rows, :]

        def rope(xn):
            partner = jnp.where(even, pltpu.roll(xn, LANES - 1, 1), pltpu.roll(xn, 1, 1))
            return xn * cos + partner * sin

        for c in range(ATT_W // LANES):
            qr = rope(norm(proj[:, c * LANES:(c + 1) * LANES], qg))
            qs = (qr * (HEAD_DIM ** -0.5 * LOG2E)).astype(BF16)
            q_ref[2 * c, rows, :] = qs[:, :HEAD_DIM]
            q_ref[2 * c + 1, rows, :] = qs[:, HEAD_DIM:]

        kn = norm(proj[:, ATT_W:ATT_W + KV_W], kg)
        kf_ref[rows, :] = kn
        kr = rope(kn).astype(BF16)
        k_ref[0, rows, :] = kr[:, :HEAD_DIM]
        k_ref[1, rows, :] = kr[:, HEAD_DIM:]

        v = proj[:, ATT_W + KV_W:ATT_W + 2 * KV_W]
        vf_ref[rows, :] = v
        vb = v.astype(BF16)
        v_ref[0, rows, :] = vb[:, :HEAD_DIM]
        v_ref[1, rows, :] = vb[:, HEAD_DIM:]

        off = ATT_W + 2 * KV_W
        a = proj[:, off:off + CONV_CH]
        gt = proj[:, off + CONV_CH:off + 2 * CONV_CH]
        u_ref[rows, :] = a * _sigmoid(gt)


def _rope_tables(tm):
    t = jnp.arange(DEC_SEQ)
    row = (t // GRID_W).astype(F32)
    col = (t % GRID_W).astype(F32)
    freqs = ROPE_THETA ** (-jnp.arange(0, ROPE_AXIS_DIM, 2, dtype=F32) / ROPE_AXIS_DIM)
    ang = jnp.concatenate([row[:, None] * freqs, col[:, None] * freqs], axis=-1)
    pair = (jnp.arange(LANES) % HEAD_DIM) // 2
    sign = jnp.where(jnp.arange(LANES) % 2 == 0, -1.0, 1.0).astype(F32)
    cos = jnp.cos(ang)[:, pair]
    sin = jnp.sin(ang)[:, pair] * sign
    cos = jnp.concatenate([jnp.ones((tm, LANES), F32), cos], axis=0)
    sin = jnp.concatenate([jnp.zeros((tm, LANES), F32), sin], axis=0)
    return cos, sin


def _inproj_a(xp, xs, mod, w, q_gain, k_gain):
    tm = TM_PROJ
    npt = NP // tm
    tps = DEC_SEQ // tm
    nt = NT // tm
    cos, sin = _rope_tables(tm)
    qg = jnp.tile(q_gain, LANES // HEAD_DIM).reshape(1, LANES)
    kg = jnp.tile(k_gain, LANES // HEAD_DIM).reshape(1, LANES)

    def rope_idx(i):
        return (jnp.where(i < npt, 0, 1 + (i - npt) % tps), 0)

    return pl.pallas_call(
        functools.partial(_inproj_a_kernel, npt=npt),
        out_shape=[
            jax.ShapeDtypeStruct((ATT_HEADS, NT, HEAD_DIM), BF16),
            jax.ShapeDtypeStruct((ATT_KV_HEADS, NT, HEAD_DIM), BF16),
            jax.ShapeDtypeStruct((ATT_KV_HEADS, NT, HEAD_DIM), BF16),
            jax.ShapeDtypeStruct((NT, KV_W), F32),
            jax.ShapeDtypeStruct((NT, KV_W), F32),
            jax.ShapeDtypeStruct((NT, CONV_CH), F32),
        ],
        grid=(nt,),
        in_specs=[
            pl.BlockSpec((tm, D_MODEL), lambda i: (jnp.minimum(i, npt - 1), 0)),
            pl.BlockSpec((tm, D_MODEL), lambda i: (jnp.maximum(i - npt, 0), 0)),
            pl.BlockSpec((None, 6, D_MODEL), lambda i: (_cond_index(i, tm), 0, 0)),
            pl.BlockSpec((None, D_MODEL, EVEN_IN), lambda i: (0, 0, 0)),
            pl.BlockSpec((1, LANES), lambda i: (0, 0)),
            pl.BlockSpec((1, LANES), lambda i: (0, 0)),
            pl.BlockSpec((tm, LANES), rope_idx),
            pl.BlockSpec((tm, LANES), rope_idx),
        ],
        out_specs=[
            pl.BlockSpec((ATT_HEADS, tm, HEAD_DIM), lambda i: (0, i, 0)),
            pl.BlockSpec((ATT_KV_HEADS, tm, HEAD_DIM), lambda i: (0, i, 0)),
            pl.BlockSpec((ATT_KV_HEADS, tm, HEAD_DIM), lambda i: (0, i, 0)),
            pl.BlockSpec((tm, KV_W), lambda i: (i, 0)),
            pl.BlockSpec((tm, KV_W), lambda i: (i, 0)),
            pl.BlockSpec((tm, CONV_CH), lambda i: (i, 0)),
        ],
        scratch_shapes=[pltpu.VMEM((D_MODEL, EVEN_IN), BF16)],
        compiler_params=_params(("arbitrary",)),
        name="inproj_a",
    )(xp, xs, mod, w, qg, kg, cos, sin)


def _attn_chunk(qs, kc, vc, m, acc):
    s = lax.dot_general(qs, kc, (((1,), (1,)), ((), ())), preferred_element_type=F32)
    fold = s[:, :LANES]
    for c in range(1, s.shape[1] // LANES):
        fold = jnp.maximum(fold, s[:, c * LANES:(c + 1) * LANES])
    m_new = jnp.maximum(m, jnp.max(fold, axis=1, keepdims=True))
    alpha = jnp.exp2(m - m_new)
    p = jnp.exp2(s - m_new).astype(BF16)
    va = jnp.concatenate([vc, jnp.ones((vc.shape[0], LANES - HEAD_DIM), BF16)], axis=1)
    acc = alpha * acc + jnp.dot(p, va, preferred_element_type=F32)
    return m_new, acc


def _attn_init(nq):
    return jnp.full((nq, 1), -1e30, F32), jnp.zeros((nq, LANES), F32)


def _attn_finish(acc, o_ref):
    tq = o_ref.shape[0]
    o = acc[:, :HEAD_DIM] * (1.0 / acc[:, HEAD_DIM:HEAD_DIM + 1])
    o_ref[...] = jnp.concatenate([o[h * tq:(h + 1) * tq] for h in range(ATT_GROUP)],
                                 axis=1).astype(BF16)


def _stack_heads(q_ref):
    return q_ref[...].reshape(ATT_GROUP * q_ref.shape[1], HEAD_DIM)


def _attn_prompt_kernel(q_ref, k_ref, v_ref, o_ref):
    qs = _stack_heads(q_ref)
    _, acc = _attn_chunk(qs, k_ref[...], v_ref[...], *_attn_init(qs.shape[0]))
    _attn_finish(acc, o_ref)


def _attn_sample_kernel(q_ref, k_ref, v_ref, ck_ref, cv_ref, o_ref, k_scr, v_scr, *, nchunks):
    @pl.when(pl.program_id(2) == 0)
    def _():
        own = k_ref.shape[0]
        k_scr[0:own, :] = k_ref[...]
        v_scr[0:own, :] = v_ref[...]
        first = pl.program_id(1) == 0
        ck = ck_ref[...]
        cv = cv_ref[...]
        k_scr[own:, :] = jnp.where(first, ck[:, :HEAD_DIM], ck[:, HEAD_DIM:]).astype(BF16)
        v_scr[own:, :] = jnp.where(first, cv[:, :HEAD_DIM], cv[:, HEAD_DIM:]).astype(BF16)

    qs = _stack_heads(q_ref)

    def body(j, carry):
        rows = pl.ds(pl.multiple_of(j * TK_ATT, TK_ATT), TK_ATT)
        return _attn_chunk(qs, k_scr[rows, :], v_scr[rows, :], *carry)

    _, acc = lax.fori_loop(0, nchunks, body, _attn_init(qs.shape[0]), unroll=ATT_UNROLL)
    _attn_finish(acc, o_ref)


def _attention_prompt(q, k, v):
    return pl.pallas_call(
        _attn_prompt_kernel,
        out_shape=jax.ShapeDtypeStruct((NP, ATT_W), BF16),
        grid=(BATCH, ATT_KV_HEADS),
        in_specs=[
            pl.BlockSpec((ATT_GROUP, SEQ, HEAD_DIM), lambda b, g: (g, b, 0)),
            pl.BlockSpec((None, SEQ, HEAD_DIM), lambda b, g: (g, b, 0)),
            pl.BlockSpec((None, SEQ, HEAD_DIM), lambda b, g: (g, b, 0)),
        ],
        out_specs=pl.BlockSpec((SEQ, ATT_GROUP * HEAD_DIM), lambda b, g: (b, g)),
        compiler_params=_params(("arbitrary", "arbitrary")),
        name="attn_prompt",
    )(q, k, v)


def _attention_sample(q, k, v, ctx_k, ctx_v):
    tq = TQ_ATT
    nkeys = DEC_SEQ + PAST_LEN
    nchunks = nkeys // TK_ATT
    q_off = NP // tq
    kv_off = NP // DEC_SEQ
    return pl.pallas_call(
        functools.partial(_attn_sample_kernel, nchunks=nchunks),
        out_shape=jax.ShapeDtypeStruct((NS, ATT_W), BF16),
        grid=(DEC_BATCH, ATT_KV_HEADS, DEC_SEQ // tq),
        in_specs=[
            pl.BlockSpec((ATT_GROUP, tq, HEAD_DIM),
                         lambda b, g, i: (g, q_off + b * (DEC_SEQ // tq) + i, 0)),
            pl.BlockSpec((None, DEC_SEQ, HEAD_DIM), lambda b, g, i: (g, kv_off + b, 0)),
            pl.BlockSpec((None, DEC_SEQ, HEAD_DIM), lambda b, g, i: (g, kv_off + b, 0)),
            pl.BlockSpec((None, PAST_LEN, KV_W), lambda b, g, i: (b, 0, 0)),
            pl.BlockSpec((None, PAST_LEN, KV_W), lambda b, g, i: (b, 0, 0)),
        ],
        out_specs=pl.BlockSpec((tq, ATT_GROUP * HEAD_DIM),
                               lambda b, g, i: (b * (DEC_SEQ // tq) + i, g)),
        scratch_shapes=[pltpu.VMEM((nkeys, HEAD_DIM), BF16), pltpu.VMEM((nkeys, HEAD_DIM), BF16)],
        compiler_params=_params(("arbitrary", "arbitrary", "arbitrary")),
        name="attn_sample",
    )(q, k, v, ctx_k, ctx_v)


def _conv_kernel(prev_ref, cur_ref, next_ref, w_ref, b_ref, g_ref, bb_ref, o_ref, win_ref, y_ref,
                 *, npt, tps):
    r = cur_ref.shape[0]
    i = pl.program_id(0)
    j = (i - npt) % tps
    first = jnp.logical_or(i < npt, j == 0)
    last = jnp.logical_or(i < npt, j == tps - 1)
    win_ref[0:HALO, :] = jnp.where(first, 0.0, prev_ref[...])
    win_ref[HALO:HALO + r, :] = cur_ref[...]
    win_ref[HALO + r:2 * HALO + r, :] = jnp.where(last, 0.0, next_ref[...])
    base = HALO - CONV_K // 2
    sub = 8
    for c in range(CONV_CH // LANES):
        cs = slice(c * LANES, (c + 1) * LANES)
        acc = None
        for res in range(sub):
            p = None
            for k in range(CONV_K):
                if (base + k) % sub != res:
                    continue
                a = (base + k) - res
                term = win_ref[a:a + r + sub, cs] * w_ref[k:k + 1, cs]
                p = term if p is None else p + term
            if p is None:
                continue
            p = p[res:res + r]
            acc = p if acc is None else acc + p
        y_ref[:, cs] = acc + b_ref[:, cs]
    y = _layernorm(y_ref[...], g_ref[...], bb_ref[...])
    o_ref[...] = (y * _sigmoid(y)).astype(BF16)


def _conv_module(u, conv_w, conv_b, cln_g, cln_b):
    r = R_CONV
    npt = NP // r
    tps = DEC_SEQ // r
    hb = r // HALO
    nh = NT // HALO
    w = jnp.concatenate([conv_w, jnp.zeros((1, CONV_CH), F32)], axis=0)
    return pl.pallas_call(
        functools.partial(_conv_kernel, npt=npt, tps=tps),
        out_shape=jax.ShapeDtypeStruct((NT, CONV_CH), BF16),
        grid=(NT // r,),
        in_specs=[
            pl.BlockSpec((HALO, CONV_CH), lambda i: (jnp.maximum(i * hb - 1, 0), 0)),
            pl.BlockSpec((r, CONV_CH), lambda i: (i, 0)),
            pl.BlockSpec((HALO, CONV_CH), lambda i: (jnp.minimum((i + 1) * hb, nh - 1), 0)),
            pl.BlockSpec((CONV_K + 1, CONV_CH), lambda i: (0, 0)),
            pl.BlockSpec((1, CONV_CH), lambda i: (0, 0)),
            pl.BlockSpec((1, CONV_CH), lambda i: (0, 0)),
            pl.BlockSpec((1, CONV_CH), lambda i: (0, 0)),
        ],
        out_specs=pl.BlockSpec((r, CONV_CH), lambda i: (i, 0)),
        scratch_shapes=[pltpu.VMEM((r + 2 * HALO, CONV_CH), F32), pltpu.VMEM((r, CONV_CH), F32)],
        compiler_params=_params(("arbitrary",)),
        name="conv_module",
    )(u, u, u, w, conv_b.reshape(1, CONV_CH), cln_g.reshape(1, CONV_CH), cln_b.reshape(1, CONV_CH))


def _tail_kernel(*refs, mode, ncast, npt, tps, off):
    nsrc = 5 if mode == "a" else 4
    src = refs[:nsrc]
    mod_ref, wo_ref, w1_ref, w2_ref, g0_ref, b0_ref, g1_ref, b1_ref, o_ref, wo_scr, w1_scr, w2_scr = refs[nsrc:]
    s = pl.program_id(0)
    per = ncast // w1_scr.shape[0]
    wcol = w1_ref.shape[1]

    @pl.when(s < ncast)
    def _():
        wo_scr[s] = wo_ref[...].astype(BF16)
        w2_scr[s] = w2_ref[...].astype(BF16)

    for q in range(per):
        @pl.when(jnp.logical_and(s < ncast, s % per == q))
        def _():
            w1_scr[s // per, :, q * wcol:(q + 1) * wcol] = w1_ref[...].astype(BF16)

    @pl.when(s >= ncast)
    def _():
        tile = s - ncast + off
        ctx = tile < npt
        mod = mod_ref[...]
        w_out = wo_scr[...].reshape(D_MODEL, D_MODEL)
        nchunk = w1_scr.shape[0]
        rows_per = w2_scr.shape[0] // nchunk
        half = o_ref.shape[0] // 2
        for r in range(2):
            rows = slice(r * half, (r + 1) * half)
            if mode == "a":
                ap_ref, as_ref, uc_ref, xp_ref, xs_ref = src
                att = jnp.where(ctx, ap_ref[rows, :], as_ref[rows, :])
                x = jnp.where(ctx, xp_ref[rows, :], xs_ref[rows, :])
                y = jnp.dot(att, w_out[0:ATT_W, :], preferred_element_type=F32)
                y = y + jnp.dot(uc_ref[rows, :], w_out[ATT_W:, :], preferred_element_type=F32)
            else:
                hp_ref, lo_ref, hi_ref, x_ref = src
                upper = ((tile - npt) % tps) >= tps // 2
                hg = jnp.where(ctx, hp_ref[rows, :], jnp.where(upper, hi_ref[rows, :], lo_ref[rows, :]))
                x = x_ref[rows, :]
                y = jnp.dot(hg, w_out, preferred_element_type=F32)
            x1 = _layernorm(ALPHA * x + mod[2:3] * y, g0_ref[...], b0_ref[...])
            h = (x1 * (1.0 + mod[4:5]) + mod[3:4]).astype(BF16)
            acc = None
            for c in range(nchunk):
                a = jnp.maximum(jnp.dot(h, w1_scr[c], preferred_element_type=F32), 0.0)
                w2 = w2_scr[c * rows_per:(c + 1) * rows_per].reshape(TF_FFN, D_MODEL)
                t = jnp.dot((a * a).astype(BF16), w2, preferred_element_type=F32)
                acc = t if acc is None else acc + t
            o_ref[rows, :] = _layernorm(ALPHA * x1 + mod[5:6] * acc, g1_ref[...], b1_ref[...])


def _block_tail(mode, src, row0, nrows, mod, w_out, w1, w2, layer, ln0, ln1):
    tm = TM_PROJ
    ncast = TAIL_CAST_STEPS
    npt = NP // tm
    tps = DEC_SEQ // tm
    hps = tps // 2
    off = row0 // tm
    tile = lambda s: jnp.maximum(s - ncast, 0) + off
    blk = lambda s: jnp.minimum(s, ncast - 1)
    first = lambda s: (jnp.minimum(tile(s), npt - 1), 0)
    second = lambda s: (jnp.maximum(tile(s) - npt, 0), 0)

    def half_idx(s, upper):
        t0 = jnp.maximum(tile(s) - npt, 0)
        t = t0 % tps
        t = jnp.maximum(t - hps, 0) if upper else jnp.minimum(t, hps - 1)
        return ((t0 // tps) * hps + t, 0)

    if mode == "a":
        src_specs = [
            pl.BlockSpec((tm, ATT_W), first),
            pl.BlockSpec((tm, ATT_W), second),
            pl.BlockSpec((tm, CONV_CH), lambda s: (tile(s), 0)),
            pl.BlockSpec((tm, D_MODEL), first),
            pl.BlockSpec((tm, D_MODEL), second),
        ]
    else:
        src_specs = [
            pl.BlockSpec((tm, M_INNER), first),
            pl.BlockSpec((tm, M_INNER), lambda s: half_idx(s, False)),
            pl.BlockSpec((tm, M_INNER), lambda s: half_idx(s, True)),
            pl.BlockSpec((tm, D_MODEL), lambda s: (tile(s), 0)),
        ]
    nchunk = D_FF // TF_FFN
    vec = pl.BlockSpec((1, D_MODEL), lambda s: (0, 0))
    return pl.pallas_call(
        functools.partial(_tail_kernel, mode=mode, ncast=ncast, npt=npt, tps=tps, off=off),
        out_shape=jax.ShapeDtypeStruct((nrows, D_MODEL), F32),
        grid=(ncast + nrows // tm,),
        in_specs=src_specs + [
            pl.BlockSpec((None, 6, D_MODEL), lambda s: (_cond_index(tile(s), tm), 0, 0)),
            pl.BlockSpec((None, D_MODEL // ncast, D_MODEL), lambda s: (0, blk(s), 0)),
            pl.BlockSpec((None, D_MODEL, D_FF // ncast), lambda s: (layer, 0, blk(s))),
            pl.BlockSpec((None, D_FF // ncast, D_MODEL), lambda s: (layer, blk(s), 0)),
            vec, vec, vec, vec,
        ],
        out_specs=pl.BlockSpec((tm, D_MODEL), lambda s: (jnp.maximum(s - ncast, 0), 0)),
        scratch_shapes=[
            pltpu.VMEM((ncast, D_MODEL // ncast, D_MODEL), BF16),
            pltpu.VMEM((nchunk, D_MODEL, TF_FFN), BF16),
            pltpu.VMEM((ncast, D_FF // ncast, D_MODEL), BF16),
        ],
        compiler_params=_params(("arbitrary",)),
        name="tail_" + mode,
    )(*src, mod, w_out, w1, w2, *ln0, *ln1)


def _inproj_m_kernel(x_ref, mod_ref, w_ref, wg_ref, bg_ref, o_ref, gt_ref, vt_ref):
    nslab = vt_ref.shape[0]
    L = L_CHUNK
    mod = mod_ref[...]
    h = (x_ref[...] * (1.0 + mod[1:2]) + mod[0:1]).astype(BF16)

    gates = jnp.dot(h, wg_ref[...], preferred_element_type=F32) + bg_ref[...]
    kind = lax.broadcasted_iota(jnp.int32, gates.shape, 1) % GATE_STRIDE
    gates = jnp.where((kind == 2) | (kind == 3), _log_sigmoid(gates), gates)
    gates_t = gates.T
    for s in range(nslab):
        gt_ref[s] = gates_t[0:GATE_ROWS, s * L:(s + 1) * L]

    for j in range(4):
        cols = slice(j * M_INNER, (j + 1) * M_INNER)
        p = jnp.dot(h, w_ref[:, cols], preferred_element_type=F32)
        if j == 1:
            p = p * (M_DK ** -0.5)
        o_ref[:, cols] = p.astype(BF16)
        if j == 2:
            v_t = p.T.astype(BF16)
            for s in range(nslab):
                vt_ref[s] = v_t[:, s * L:(s + 1) * L]


def _inproj_m(x, mod, w, wg, bg):
    tm = TM_PROJ
    nslab = tm // L_CHUNK
    return pl.pallas_call(
        _inproj_m_kernel,
        out_shape=[
            jax.ShapeDtypeStruct((NT, 4 * M_INNER), BF16),
            jax.ShapeDtypeStruct((NT // L_CHUNK, GATE_ROWS, L_CHUNK), F32),
            jax.ShapeDtypeStruct((NT // L_CHUNK, M_INNER, L_CHUNK), BF16),
        ],
        grid=(NT // tm,),
        in_specs=[
            pl.BlockSpec((tm, D_MODEL), lambda i: (i, 0)),
            pl.BlockSpec((None, 6, D_MODEL), lambda i: (_cond_index(i, tm), 0, 0)),
            pl.BlockSpec((D_MODEL, 4 * M_INNER), lambda i: (0, 0)),
            pl.BlockSpec((D_MODEL, LANES), lambda i: (0, 0)),
            pl.BlockSpec((1, LANES), lambda i: (0, 0)),
        ],
        out_specs=[
            pl.BlockSpec((tm, 4 * M_INNER), lambda i: (i, 0)),
            pl.BlockSpec((nslab, GATE_ROWS, L_CHUNK), lambda i: (i, 0, 0)),
            pl.BlockSpec((nslab, M_INNER, L_CHUNK), lambda i: (i, 0, 0)),
        ],
        compiler_params=_params(("arbitrary",)),
        name="inproj_m",
    )(x, mod, w, wg, bg)


def _log_sigmoid(x):
    return jnp.minimum(x, 0.0) - jnp.log1p(jnp.exp(-jnp.abs(x)))


def _split3(x):
    x1 = x.astype(BF16)
    r1 = x - x1.astype(F32)
    x2 = r1.astype(BF16)
    x3 = (r1 - x2.astype(F32)).astype(BF16)
    return x1, x2, x3


def _mlstm_prefix(gt):
    L = gt.shape[1]
    parts = jnp.concatenate([t.astype(F32) for t in _split3(gt)]
                            + [jnp.zeros((GATE_ROWS, L), F32)], axis=0).astype(BF16)
    upper = jnp.where(lax.broadcasted_iota(jnp.int32, (L, L), 0)
                      <= lax.broadcasted_iota(jnp.int32, (L, L), 1), 1.0, 0.0).astype(BF16)
    r = jnp.dot(parts, upper, preferred_element_type=F32)
    return r[0:GATE_ROWS] + r[GATE_ROWS:2 * GATE_ROWS] + r[2 * GATE_ROWS:3 * GATE_ROWS]


def _rows3(x, n):
    g = x.shape[0]
    return jnp.concatenate([t.astype(F32) for t in _split3(x)]
                           + [jnp.zeros((g, ONES_ROWS - 3, n), F32)], axis=1).astype(BF16)


def _sum3(x, r0):
    return x[:, r0:r0 + 1] + x[:, r0 + 1:r0 + 2] + x[:, r0 + 2:r0 + 3]


def _mlstm_group(rev, q, k, vt, gt, pre, state):
    G, L = q.shape[0], q.shape[1]
    d = 1 if rev else 0
    tpos = lax.broadcasted_iota(jnp.int32, (L, L), 0)
    spos = lax.broadcasted_iota(jnp.int32, (L, L), 1)
    keep = (tpos >= spos) if rev else (tpos <= spos)
    if rev:
        tot = pre[:, 3:4, L - 1:L]
        a_row = tot - pre[:, 3:4, :] + gt[:, 3:4, :]
    else:
        a_row = pre[:, 2:3, :]
        tot = a_row[:, :, L - 1:L]
    b_row = gt[:, d:d + 1, :] - a_row
    m = jnp.zeros((G, 1, 1), F32) if state is None else state[0]

    gp = -(-G // 8) * 8
    rows = [b_row[p] for p in range(G)] + [jnp.zeros((gp - G, L), F32)] * (1 if gp > G else 0)
    bt = jnp.concatenate(rows, axis=0).T
    bm = jnp.stack([jnp.where(keep, bt[:, p:p + 1], -jnp.inf) for p in range(G)])
    mm = jnp.maximum(m, jnp.max(bm, axis=1, keepdims=True))
    w = jnp.exp(bm - mm)
    s = jnp.einsum('gsd,gtd->gst', k, q, preferred_element_type=F32)
    qk = s * w
    num = jnp.einsum('gvs,gst->gvt', vt, qk.astype(BF16), preferred_element_type=F32)
    den = jnp.sum(qk, axis=1, keepdims=True)
    if state is not None:
        _, ct, n = state
        cn = jnp.concatenate([ct.astype(BF16), _rows3(n, n.shape[2])], axis=1)
        inter = jnp.einsum('gcd,gtd->gct', cn, q, preferred_element_type=F32)
        s_inter = jnp.exp(m - mm)
        num = num + s_inter * inter[:, :M_DV]
        den = den + s_inter * _sum3(inter, M_DV)
    h = num * (1.0 / jnp.maximum(jnp.abs(den), jnp.exp(-(a_row + mm))))

    g_row = tot + b_row
    m_new = jnp.maximum(tot + m, jnp.max(g_row, axis=2, keepdims=True))
    ws = jnp.exp(g_row - m_new)
    lhs = jnp.concatenate([(vt.astype(F32) * ws).astype(BF16), _rows3(ws, L)], axis=1)
    upd = jnp.einsum('gcs,gsd->gcd', lhs, k, preferred_element_type=F32)
    ct_new = upd[:, :M_DV]
    n_new = _sum3(upd, M_DV)
    if state is not None:
        decay = jnp.exp(tot + m - m_new)
        ct_new = decay * ct + ct_new
        n_new = decay * n + n_new
    return h, (m_new, ct_new, n_new)


def _heads(x):
    w = x.shape[1] // M_HEADS
    return jnp.stack([x[:, h * w:(h + 1) * w] for h in range(M_HEADS)])


def _mlstm_finish(ht, o, gain):
    ms = jnp.mean(ht * ht, axis=1, keepdims=True)
    hn = ht * lax.rsqrt(ms + EPS)
    hn = jnp.concatenate([hn[h].T for h in range(M_HEADS)], axis=1)
    return (_sigmoid(o.astype(F32)) * hn * gain).astype(BF16)


def _mlstm_sample_kernel(qf_ref, kf_ref, vtf_ref, gtf_ref, of_ref, qb_ref, kb_ref, vtb_ref, gtb_ref,
                         ob_ref, gain_ref, c0_ref, n0_ref, m0_ref, hi_ref, lo_ref,
                         hs_ref, ct_scr, n_scr, m_scr, *, nc):
    j = pl.program_id(1)
    jb = nc - 1 - j
    H = M_HEADS
    L = L_CHUNK

    @pl.when(j == 0)
    def _():
        for d in range(2):
            for h in range(H):
                ct_scr[d * H + h] = c0_ref[d, h].T
                n_scr[d * H + h] = n0_ref[d, h]
                m_scr[d * H + h] = jnp.broadcast_to(m0_ref[d, h], (1, LANES))

    def run(rev, q_ref, k_ref, vt_ref, gt_ref, lo):
        gt = gt_ref[...]
        pre = _mlstm_prefix(gt).reshape(H, GATE_STRIDE, L)
        state = (m_scr[lo:lo + H][:, :, 0:1], ct_scr[lo:lo + H], n_scr[lo:lo + H])
        h, (m_new, ct_new, n_new) = _mlstm_group(
            rev, _heads(q_ref[...]), _heads(k_ref[...]), vt_ref[...].reshape(H, M_DV, L),
            gt.reshape(H, GATE_STRIDE, L), pre, state)
        ct_scr[lo:lo + H] = ct_new
        n_scr[lo:lo + H] = n_new
        m_scr[lo:lo + H] = jnp.broadcast_to(m_new, (H, 1, LANES))
        return h

    h_f = run(False, qf_ref, kf_ref, vtf_ref, gtf_ref, 0)
    h_b = run(True, qb_ref, kb_ref, vtb_ref, gtb_ref, H)

    @pl.when(j < nc // 2)
    def _():
        hs_ref[j] = h_f
        hs_ref[jb] = h_b

    @pl.when(j >= nc // 2)
    def _():
        gain = gain_ref[...]
        hi_ref[...] = _mlstm_finish(hs_ref[j] + h_f, of_ref[...], gain)
        lo_ref[...] = _mlstm_finish(hs_ref[jb] + h_b, ob_ref[...], gain)


def _mlstm_sample(qkvo, vt, gates_t, gain, init):
    nc = DEC_SEQ // L_CHUNK
    half = nc // 2
    c0 = NP // L_CHUNK
    L = L_CHUNK
    fwd = lambda b, j: c0 + b * nc + j
    bwd = lambda b, j: c0 + b * nc + nc - 1 - j

    def specs(f):
        return [
            pl.BlockSpec((L, M_INNER), lambda b, j: (f(b, j), 0)),
            pl.BlockSpec((L, M_INNER), lambda b, j: (f(b, j), 1)),
            pl.BlockSpec((None, M_INNER, L), lambda b, j: (f(b, j), 0, 0)),
            pl.BlockSpec((None, GATE_ROWS, L), lambda b, j: (f(b, j), 0, 0)),
            pl.BlockSpec((L, M_INNER), lambda b, j: (f(b, j), 3)),
        ]

    state_specs = [
        pl.BlockSpec((None, 2, M_HEADS, M_DK, M_DV), lambda b, j: (b, 0, 0, 0, 0)),
        pl.BlockSpec((None, 2, M_HEADS, 1, M_DK), lambda b, j: (b, 0, 0, 0, 0)),
        pl.BlockSpec((None, 2, M_HEADS, 1, 1), lambda b, j: (b, 0, 0, 0, 0)),
    ]
    out_rows = DEC_BATCH * DEC_SEQ // 2
    return pl.pallas_call(
        functools.partial(_mlstm_sample_kernel, nc=nc),
        out_shape=[jax.ShapeDtypeStruct((out_rows, M_INNER), BF16)] * 2,
        grid=(DEC_BATCH, nc),
        in_specs=specs(fwd) + specs(bwd) + [pl.BlockSpec((1, M_INNER), lambda b, j: (0, 0))] + state_specs,
        out_specs=[
            pl.BlockSpec((L, M_INNER), lambda b, j: (b * half + jnp.maximum(j, half) - half, 0)),
            pl.BlockSpec((L, M_INNER), lambda b, j: (b * half + jnp.minimum(nc - 1 - j, half - 1), 0)),
        ],
        scratch_shapes=[
            pltpu.VMEM((nc, M_HEADS, M_DV, L), F32),
            pltpu.VMEM((2 * M_HEADS, M_DV, M_DK), F32),
            pltpu.VMEM((2 * M_HEADS, 1, M_DK), F32),
            pltpu.VMEM((2 * M_HEADS, 1, LANES), F32),
        ],
        compiler_params=_params(("arbitrary", "arbitrary")),
        name="mlstm",
    )(qkvo, qkvo, vt, gates_t, qkvo, qkvo, qkvo, vt, gates_t, qkvo, gain, *init)


def _mlstm_prompt_kernel(q_ref, k_ref, vt_ref, gt_ref, o_ref, gain_ref, out_ref, co_ref, no_ref, mo_ref,
                         *, nseq):
    H = M_HEADS
    L = L_CHUNK
    G = nseq * H
    q = jnp.concatenate([_heads(q_ref[b * L:(b + 1) * L, :]) for b in range(nseq)], axis=0)
    k = jnp.concatenate([_heads(k_ref[b * L:(b + 1) * L, :]) for b in range(nseq)], axis=0)
    vt = vt_ref[...].reshape(G, M_DV, L)
    gt3 = jnp.concatenate([gt_ref[b] for b in range(nseq)], axis=0).reshape(G, GATE_STRIDE, L)
    pre3 = jnp.concatenate([_mlstm_prefix(gt_ref[b]) for b in range(nseq)],
                           axis=0).reshape(G, GATE_STRIDE, L)
    gain = gain_ref[...]
    hs = None
    for d in range(2):
        h, (m_new, ct_new, n_new) = _mlstm_group(d == 1, q, k, vt, gt3, pre3, None)
        hs = h if hs is None else hs + h
        for b in range(nseq):
            for hh in range(H):
                co_ref[b, d, hh] = ct_new[b * H + hh].T
                no_ref[b, d, hh] = n_new[b * H + hh]
                mo_ref[b, d, hh] = m_new[b * H + hh]
    for b in range(nseq):
        out_ref[b * L:(b + 1) * L, :] = _mlstm_finish(hs[b * H:(b + 1) * H], o_ref[b * L:(b + 1) * L, :], gain)


def _mlstm_prompt(qkvo, vt, gates_t, gain):
    nseq = MLSTM_SEQS
    L = L_CHUNK
    rows = nseq * L
    state_specs = [
        pl.BlockSpec((nseq, 2, M_HEADS, M_DK, M_DV), lambda i: (i, 0, 0, 0, 0)),
        pl.BlockSpec((nseq, 2, M_HEADS, 1, M_DK), lambda i: (i, 0, 0, 0, 0)),
        pl.BlockSpec((nseq, 2, M_HEADS, 1, 1), lambda i: (i, 0, 0, 0, 0)),
    ]
    return pl.pallas_call(
        functools.partial(_mlstm_prompt_kernel, nseq=nseq),
        out_shape=[
            jax.ShapeDtypeStruct((NP, M_INNER), BF16),
            jax.ShapeDtypeStruct((BATCH, 2, M_HEADS, M_DK, M_DV), F32),
            jax.ShapeDtypeStruct((BATCH, 2, M_HEADS, 1, M_DK), F32),
            jax.ShapeDtypeStruct((BATCH, 2, M_HEADS, 1, 1), F32),
        ],
        grid=(BATCH // nseq,),
        in_specs=[
            pl.BlockSpec((rows, M_INNER), lambda i: (i, 0)),
            pl.BlockSpec((rows, M_INNER), lambda i: (i, 1)),
            pl.BlockSpec((nseq, M_INNER, L), lambda i: (i, 0, 0)),
            pl.BlockSpec((nseq, GATE_ROWS, L), lambda i: (i, 0, 0)),
            pl.BlockSpec((rows, M_INNER), lambda i: (i, 3)),
            pl.BlockSpec((1, M_INNER), lambda i: (0, 0)),
        ],
        out_specs=[pl.BlockSpec((rows, M_INNER), lambda i: (i, 0))] + state_specs,
        compiler_params=_params(("arbitrary",)),
        name="mlstm_state",
    )(qkvo, qkvo, vt, gates_t, qkvo, gain)


@jax.jit
def kernel(x_prompt, x_sample, cache_k, cache_v, state_c, state_n, state_m, c, c_ctx, w_ada, b_ada,
           ln_g, ln_b, w_ff1, w_ff2, w_in_a, q_gain, k_gain, conv_w, conv_b, conv_ln_g, conv_ln_b,
           w_out_a, w_in_m, b_gate_m, mh_gain, w_out_m):
    xp = x_prompt.reshape(NP, D_MODEL)
    xs = x_sample.reshape(NS, D_MODEL)

    cond = jnp.concatenate(
        [c_ctx[None, :], c, jnp.zeros((N_COND - 1 - DEC_BATCH, D_MODEL), F32)], axis=0)
    mods = _adaln(cond, w_ada, b_ada)

    ln = lambda l, s: (ln_g[l, s].reshape(1, D_MODEL), ln_b[l, s].reshape(1, D_MODEL))

    q, k, v, kf, vf, u = _inproj_a(xp, xs, mods[0], w_in_a, q_gain[0], k_gain[0])
    att_p = _attention_prompt(q, k, v)
    ctx_k = cache_k[:, 0].reshape(DEC_BATCH, PAST_LEN, KV_W)
    ctx_v = cache_v[:, 0].reshape(DEC_BATCH, PAST_LEN, KV_W)
    att_s = _attention_sample(q, k, v, ctx_k, ctx_v)
    uc = _conv_module(u, conv_w[0], conv_b[0], conv_ln_g[0], conv_ln_b[0])
    x2 = _block_tail("a", (att_p, att_s, uc, xp, xs), 0, NT, mods[0], w_out_a, w_ff1, w_ff2, 0,
                     ln(0, 0), ln(0, 1))

    wm = w_in_m[0]
    src = jnp.array([4 * M_INNER + g * M_HEADS + h
                     for h in range(M_HEADS) for g in (0, 2, 1, 3)], jnp.int32)
    dst = jnp.array([h * GATE_STRIDE + t for h in range(M_HEADS) for t in range(4)], jnp.int32)
    wg = jnp.zeros((D_MODEL, LANES), F32).at[:, dst].set(wm[:, src]).astype(BF16)
    bg = jnp.zeros((1, LANES), F32).at[0, dst].set(b_gate_m[0][src - 4 * M_INNER])
    qkvo, gates_t, vt_m = _inproj_m(x2, mods[1], wm[:, :4 * M_INNER].astype(BF16), wg, bg)
    gain = mh_gain[0].reshape(1, M_INNER)
    hg_p, st_c, st_n, st_m = _mlstm_prompt(qkvo, vt_m, gates_t, gain)
    init = (state_c[:, 0], state_n[:, 0].reshape(DEC_BATCH, 2, M_HEADS, 1, M_DK),
            state_m[:, 0].reshape(DEC_BATCH, 2, M_HEADS, 1, 1))
    hg_hi, hg_lo = _mlstm_sample(qkvo, vt_m, gates_t, gain, init)
    tail_src = (hg_p, hg_lo, hg_hi, x2)
    y_p = _block_tail("m", tail_src, 0, NP, mods[1], w_out_m, w_ff1, w_ff2, 1, ln(1, 0), ln(1, 1))
    y_s = _block_tail("m", tail_src, NP, NS, mods[1], w_out_m, w_ff1, w_ff2, 1, ln(1, 0), ln(1, 1))

    new_k = kf[:NP].reshape(BATCH, 1, SEQ, ATT_KV_HEADS, HEAD_DIM)
    new_v = vf[:NP].reshape(BATCH, 1, SEQ, ATT_KV_HEADS, HEAD_DIM)
    return (y_p.reshape(BATCH, SEQ, D_MODEL), y_s.reshape(DEC_BATCH, DEC_SEQ, D_MODEL),
            new_k, new_v,
            st_c.reshape(BATCH, 1, 2, M_HEADS, M_DK, M_DV),
            st_n.reshape(BATCH, 1, 2, M_HEADS, M_DK),
            st_m.reshape(BATCH, 1, 2, M_HEADS))
```

```python
import functools

import jax
import jax.numpy as jnp
from jax import lax
from jax.experimental import pallas as pl
from jax.experimental.pallas import tpu as pltpu

F32 = jnp.float32
BF16 = jnp.bfloat16

D_MODEL = 1024
BATCH = 32
SEQ = 256
DEPTH = 2
DEC_BATCH = 4
DEC_SEQ = 4096
PAST_LEN = 512
GRID_W = 64
ATT_HEADS = 8
ATT_KV_HEADS = 2
HEAD_DIM = 64
ATT_GROUP = ATT_HEADS // ATT_KV_HEADS
ATT_W = ATT_HEADS * HEAD_DIM
KV_W = ATT_KV_HEADS * HEAD_DIM
ROPE_AXIS_DIM = HEAD_DIM // 2
ROPE_THETA = 10000.0
CONV_CH = D_MODEL // 2
CONV_K = 31
EVEN_IN = ATT_W + 2 * KV_W + 2 * CONV_CH
M_HEADS = 4
M_INNER = D_MODEL
M_DK = M_INNER // M_HEADS
M_DV = M_INNER // M_HEADS
D_FF = 4 * D_MODEL
ALPHA = (2 * DEPTH) ** 0.25
EPS = 1e-6
LOG2E = 1.4426950408889634

NP = BATCH * SEQ
NS = DEC_BATCH * DEC_SEQ
NT = NP + NS
N_COND = 8

LANES = 128
VMEM_LIMIT = 56 * 1024 * 1024

TM_PROJ = 512
TF_FFN = 1024
TQ_ATT = 256
TK_ATT = 512
R_CONV = 256
HALO = 16
L_CHUNK = 256
GATE_STRIDE = 8
ONES_ROWS = 16
GATE_ROWS = 32
MLSTM_SEQS = 4
TAIL_CAST_STEPS = 16
INPROJ_SUBTILES = 4
ATT_UNROLL = 9
ATT_PROMPT_SEQS = 1


def _cond_index(i_global, tm):
    npt = NP // tm
    tps = DEC_SEQ // tm
    return jnp.where(i_global < npt, 0, 1 + (i_global - npt) // tps)


def _layernorm(r, g, b):
    mu = jnp.mean(r, axis=-1, keepdims=True)
    d = r - mu
    var = jnp.mean(d * d, axis=-1, keepdims=True)
    return d * lax.rsqrt(var + EPS) * g + b


def _sigmoid(x):
    return 1.0 / (1.0 + jnp.exp(-x))


def _params(sem, vmem=VMEM_LIMIT, flags=None):
    return pltpu.CompilerParams(dimension_semantics=sem, vmem_limit_bytes=vmem, flags=flags)


def _adaln_kernel(cond_ref, w_ref, b_ref, o_ref):
    c = cond_ref[...]
    s = (c * _sigmoid(c)).astype(BF16)
    o_ref[...] = jnp.dot(s, w_ref[...].astype(BF16), preferred_element_type=F32) + b_ref[...]


def _adaln(cond, w_ada, b_ada):
    tn = 1536
    n = 6 * D_MODEL
    out = pl.pallas_call(
        _adaln_kernel,
        out_shape=jax.ShapeDtypeStruct((DEPTH, N_COND, n), F32),
        grid=(DEPTH, n // tn),
        in_specs=[
            pl.BlockSpec((N_COND, D_MODEL), lambda l, j: (0, 0)),
            pl.BlockSpec((None, D_MODEL, tn), lambda l, j: (l, 0, j)),
            pl.BlockSpec((None, 1, tn), lambda l, j: (l, 0, j)),
        ],
        out_specs=pl.BlockSpec((None, N_COND, tn), lambda l, j: (l, 0, j)),
        compiler_params=_params(("arbitrary", "arbitrary")),
        name="adaln",
    )(cond, w_ada, b_ada.reshape(DEPTH, 1, n))
    return out.reshape(DEPTH, N_COND, 6, D_MODEL)


def _cast_once(w_ref, w_scr):
    @pl.when(pl.program_id(0) == 0)
    def _():
        w_scr[...] = w_ref[...].astype(BF16)


def _inproj_a_kernel(xp_ref, xs_ref, mod_ref, w_ref, qg_ref, kg_ref, cos_ref, sin_ref,
                     q_ref, k_ref, v_ref, kf_ref, vf_ref, u_ref, w_scr, *, npt):
    i = pl.program_id(0)
    _cast_once(w_ref, w_scr)
    mod = mod_ref[...]
    tm = u_ref.shape[0]
    sub = tm // INPROJ_SUBTILES

    ri = lax.broadcasted_iota(jnp.int32, (LANES, LANES), 0) // HEAD_DIM
    ci = lax.broadcasted_iota(jnp.int32, (LANES, LANES), 1) // HEAD_DIM
    seg = jnp.where(ri == ci, 1.0, 0.0).astype(BF16)
    lane = lax.broadcasted_iota(jnp.int32, (sub, LANES), 1)
    even = (lane % 2) == 0
    qg = qg_ref[...]
    kg = kg_ref[...]

    def norm(xc, gain):
        ss = jnp.dot((xc * xc).astype(BF16), seg, preferred_element_type=F32)
        return xc * lax.rsqrt(ss * (1.0 / HEAD_DIM) + EPS) * gain

    for r in range(INPROJ_SUBTILES):
        rows = slice(r * sub, (r + 1) * sub)
        x = jnp.where(i < npt, xp_ref[rows, :], xs_ref[rows, :])
        h = (x * (1.0 + mod[1:2]) + mod[0:1]).astype(BF16)
        proj = jnp.dot(h, w_scr[...], preferred_element_type=F32)
        cos = cos_ref[rows, :]
        sin = sin_ref[rows, :]

        def rope(xn):
            partner = jnp.where(even, pltpu.roll(xn, LANES - 1, 1), pltpu.roll(xn, 1, 1))
            return xn * cos + partner * sin

        for c in range(ATT_W // LANES):
            qr = rope(norm(proj[:, c * LANES:(c + 1) * LANES], qg))
            qs = (qr * (HEAD_DIM ** -0.5 * LOG2E)).astype(BF16)
            q_ref[2 * c, rows, :] = qs[:, :HEAD_DIM]
            q_ref[2 * c + 1, rows, :] = qs[:, HEAD_DIM:]

        kn = norm(proj[:, ATT_W:ATT_W + KV_W], kg)
        kf_ref[rows, :] = kn
        kr = rope(kn).astype(BF16)
        k_ref[0, rows, :] = kr[:, :HEAD_DIM]
        k_ref[1, rows, :] = kr[:, HEAD_DIM:]

        v = proj[:, ATT_W + KV_W:ATT_W + 2 * KV_W]
        vf_ref[rows, :] = v
        vb = v.astype(BF16)
        v_ref[0, rows, :] = vb[:, :HEAD_DIM]
        v_ref[1, rows, :] = vb[:, HEAD_DIM:]

        off = ATT_W + 2 * KV_W
        a = proj[:, off:off + CONV_CH]
        gt = proj[:, off + CONV_CH:off + 2 * CONV_CH]
        u_ref[rows, :] = a * _sigmoid(gt)


def _rope_tables(tm):
    t = jnp.arange(DEC_SEQ)
    row = (t // GRID_W).astype(F32)
    col = (t % GRID_W).astype(F32)
    freqs = ROPE_THETA ** (-jnp.arange(0, ROPE_AXIS_DIM, 2, dtype=F32) / ROPE_AXIS_DIM)
    ang = jnp.concatenate([row[:, None] * freqs, col[:, None] * freqs], axis=-1)
    pair = (jnp.arange(LANES) % HEAD_DIM) // 2
    sign = jnp.where(jnp.arange(LANES) % 2 == 0, -1.0, 1.0).astype(F32)
    cos = jnp.cos(ang)[:, pair]
    sin = jnp.sin(ang)[:, pair] * sign
    cos = jnp.concatenate([jnp.ones((tm, LANES), F32), cos], axis=0)
    sin = jnp.concatenate([jnp.zeros((tm, LANES), F32), sin], axis=0)
    return cos, sin


def _inproj_a(xp, xs, mod, w, q_gain, k_gain):
    tm = TM_PROJ
    npt = NP // tm
    tps = DEC_SEQ // tm
    nt = NT // tm
    cos, sin = _rope_tables(tm)
    qg = jnp.tile(q_gain, LANES // HEAD_DIM).reshape(1, LANES)
    kg = jnp.tile(k_gain, LANES // HEAD_DIM).reshape(1, LANES)

    def rope_idx(i):
        return (jnp.where(i < npt, 0, 1 + (i - npt) % tps), 0)

    return pl.pallas_call(
        functools.partial(_inproj_a_kernel, npt=npt),
        out_shape=[
            jax.ShapeDtypeStruct((ATT_HEADS, NT, HEAD_DIM), BF16),
            jax.ShapeDtypeStruct((ATT_KV_HEADS, NT, HEAD_DIM), BF16),
            jax.ShapeDtypeStruct((ATT_KV_HEADS, NT, HEAD_DIM), BF16),
            jax.ShapeDtypeStruct((NT, KV_W), F32),
            jax.ShapeDtypeStruct((NT, KV_W), F32),
            jax.ShapeDtypeStruct((NT, CONV_CH), F32),
        ],
        grid=(nt,),
        in_specs=[
            pl.BlockSpec((tm, D_MODEL), lambda i: (jnp.minimum(i, npt - 1), 0)),
            pl.BlockSpec((tm, D_MODEL), lambda i: (jnp.maximum(i - npt, 0), 0)),
            pl.BlockSpec((None, 6, D_MODEL), lambda i: (_cond_index(i, tm), 0, 0)),
            pl.BlockSpec((None, D_MODEL, EVEN_IN), lambda i: (0, 0, 0)),
            pl.BlockSpec((1, LANES), lambda i: (0, 0)),
            pl.BlockSpec((1, LANES), lambda i: (0, 0)),
            pl.BlockSpec((tm, LANES), rope_idx),
            pl.BlockSpec((tm, LANES), rope_idx),
        ],
        out_specs=[
            pl.BlockSpec((ATT_HEADS, tm, HEAD_DIM), lambda i: (0, i, 0)),
            pl.BlockSpec((ATT_KV_HEADS, tm, HEAD_DIM), lambda i: (0, i, 0)),
            pl.BlockSpec((ATT_KV_HEADS, tm, HEAD_DIM), lambda i: (0, i, 0)),
            pl.BlockSpec((tm, KV_W), lambda i: (i, 0)),
            pl.BlockSpec((tm, KV_W), lambda i: (i, 0)),
            pl.BlockSpec((tm, CONV_CH), lambda i: (i, 0)),
        ],
        scratch_shapes=[pltpu.VMEM((D_MODEL, EVEN_IN), BF16)],
        compiler_params=_params(("arbitrary",)),
        name="inproj_a",
    )(xp, xs, mod, w, qg, kg, cos, sin)


def _attn_chunk(qs, kc, vc, m, acc):
    s = lax.dot_general(qs, kc, (((1,), (1,)), ((), ())), preferred_element_type=F32)
    fold = s[:, :LANES]
    for c in range(1, s.shape[1] // LANES):
        fold = jnp.maximum(fold, s[:, c * LANES:(c + 1) * LANES])
    m_new = jnp.maximum(m, jnp.max(fold, axis=1, keepdims=True))
    alpha = jnp.exp2(m - m_new)
    p = jnp.exp2(s - m_new).astype(BF16)
    va = jnp.concatenate([vc, jnp.ones((vc.shape[0], LANES - HEAD_DIM), BF16)], axis=1)
    acc = alpha * acc + jnp.dot(p, va, preferred_element_type=F32)
    return m_new, acc


def _attn_init(nq):
    return jnp.full((nq, 1), -1e30, F32), jnp.zeros((nq, LANES), F32)


def _attn_finish(acc, o_ref):
    tq = o_ref.shape[0]
    o = acc[:, :HEAD_DIM] * (1.0 / acc[:, HEAD_DIM:HEAD_DIM + 1])
    o_ref[...] = jnp.concatenate([o[h * tq:(h + 1) * tq] for h in range(ATT_GROUP)],
                                 axis=1).astype(BF16)


def _stack_heads(q_ref):
    return q_ref[...].reshape(ATT_GROUP * q_ref.shape[1], HEAD_DIM)


def _attn_prompt_kernel(q_ref, k_ref, v_ref, o_ref):
    for b in range(q_ref.shape[1] // SEQ):
        rows = slice(b * SEQ, (b + 1) * SEQ)
        qs = q_ref[:, rows, :].reshape(ATT_GROUP * SEQ, HEAD_DIM)
        _, acc = _attn_chunk(qs, k_ref[rows, :], v_ref[rows, :], *_attn_init(qs.shape[0]))
        _attn_finish(acc, o_ref.at[rows, :])


def _attn_sample_kernel(q_ref, k_ref, v_ref, ck_ref, cv_ref, o_ref, k_scr, v_scr, *, nchunks):
    @pl.when(pl.program_id(2) == 0)
    def _():
        own = k_ref.shape[0]
        k_scr[0:own, :] = k_ref[...]
        v_scr[0:own, :] = v_ref[...]
        first = pl.program_id(1) == 0
        ck = ck_ref[...]
        cv = cv_ref[...]
        k_scr[own:, :] = jnp.where(first, ck[:, :HEAD_DIM], ck[:, HEAD_DIM:]).astype(BF16)
        v_scr[own:, :] = jnp.where(first, cv[:, :HEAD_DIM], cv[:, HEAD_DIM:]).astype(BF16)

    qs = _stack_heads(q_ref)

    def body(j, carry):
        rows = pl.ds(pl.multiple_of(j * TK_ATT, TK_ATT), TK_ATT)
        return _attn_chunk(qs, k_scr[rows, :], v_scr[rows, :], *carry)

    _, acc = lax.fori_loop(0, nchunks, body, _attn_init(qs.shape[0]), unroll=ATT_UNROLL)
    _attn_finish(acc, o_ref)


def _attention_prompt(q, k, v):
    return pl.pallas_call(
        _attn_prompt_kernel,
        out_shape=jax.ShapeDtypeStruct((NP, ATT_W), BF16),
        grid=(BATCH // ATT_PROMPT_SEQS, ATT_KV_HEADS),
        in_specs=[
            pl.BlockSpec((ATT_GROUP, ATT_PROMPT_SEQS * SEQ, HEAD_DIM), lambda b, g: (g, b, 0)),
            pl.BlockSpec((None, ATT_PROMPT_SEQS * SEQ, HEAD_DIM), lambda b, g: (g, b, 0)),
            pl.BlockSpec((None, ATT_PROMPT_SEQS * SEQ, HEAD_DIM), lambda b, g: (g, b, 0)),
        ],
        out_specs=pl.BlockSpec((ATT_PROMPT_SEQS * SEQ, ATT_GROUP * HEAD_DIM), lambda b, g: (b, g)),
        compiler_params=_params(("arbitrary", "arbitrary")),
        name="attn_prompt",
    )(q, k, v)


def _attention_sample(q, k, v, ctx_k, ctx_v):
    tq = TQ_ATT
    nkeys = DEC_SEQ + PAST_LEN
    nchunks = nkeys // TK_ATT
    q_off = NP // tq
    kv_off = NP // DEC_SEQ
    return pl.pallas_call(
        functools.partial(_attn_sample_kernel, nchunks=nchunks),
        out_shape=jax.ShapeDtypeStruct((NS, ATT_W), BF16),
        grid=(DEC_BATCH, ATT_KV_HEADS, DEC_SEQ // tq),
        in_specs=[
            pl.BlockSpec((ATT_GROUP, tq, HEAD_DIM),
                         lambda b, g, i: (g, q_off + b * (DEC_SEQ // tq) + i, 0)),
            pl.BlockSpec((None, DEC_SEQ, HEAD_DIM), lambda b, g, i: (g, kv_off + b, 0)),
            pl.BlockSpec((None, DEC_SEQ, HEAD_DIM), lambda b, g, i: (g, kv_off + b, 0)),
            pl.BlockSpec((None, PAST_LEN, KV_W), lambda b, g, i: (b, 0, 0)),
            pl.BlockSpec((None, PAST_LEN, KV_W), lambda b, g, i: (b, 0, 0)),
        ],
        out_specs=pl.BlockSpec((tq, ATT_GROUP * HEAD_DIM),
                               lambda b, g, i: (b * (DEC_SEQ // tq) + i, g)),
        scratch_shapes=[pltpu.VMEM((nkeys, HEAD_DIM), BF16), pltpu.VMEM((nkeys, HEAD_DIM), BF16)],
        compiler_params=_params(("arbitrary", "arbitrary", "arbitrary")),
        name="attn_sample",
    )(q, k, v, ctx_k, ctx_v)


def _conv_window(left, cur, right, w_ref, b_ref, g_ref, bb_ref, win_ref, y_ref, out_ref):
    r = cur.shape[0]
    win_ref[0:HALO, :] = left
    win_ref[HALO:HALO + r, :] = cur
    win_ref[HALO + r:2 * HALO + r, :] = right
    base = HALO - CONV_K // 2
    sub = 8
    for c in range(CONV_CH // LANES):
        cs = slice(c * LANES, (c + 1) * LANES)
        acc = None
        for res in range(sub):
            p = None
            for k in range(CONV_K):
                if (base + k) % sub != res:
                    continue
                a = (base + k) - res
                term = win_ref[a:a + r + sub, cs] * w_ref[k:k + 1, cs]
                p = term if p is None else p + term
            if p is None:
                continue
            p = p[res:res + r]
            acc = p if acc is None else acc + p
        y_ref[:, cs] = acc + b_ref[:, cs]
    y = _layernorm(y_ref[...], g_ref[...], bb_ref[...])
    out_ref[...] = (y * _sigmoid(y)).astype(BF16)


def _tail_kernel(*refs, mode, ncast, npt, tps, off):
    nsrc = 11 if mode == "a" else 4
    src = refs[:nsrc]
    (mod_ref, wo_ref, w1_ref, w2_ref, g0_ref, b0_ref, g1_ref, b1_ref, o_ref,
     wo_scr, w1_scr, w2_scr) = refs[nsrc:nsrc + 12]
    s = pl.program_id(0)
    per = ncast // w1_scr.shape[0]
    wcol = w1_ref.shape[1]

    @pl.when(s < ncast)
    def _():
        wo_scr[s] = wo_ref[...].astype(BF16)
        w2_scr[s] = w2_ref[...].astype(BF16)

    for q in range(per):
        @pl.when(jnp.logical_and(s < ncast, s % per == q))
        def _():
            w1_scr[s // per, :, q * wcol:(q + 1) * wcol] = w1_ref[...].astype(BF16)

    def conv_half(r, tile):
        ul_ref, u_ref, ur_ref, cw_ref, cb_ref, cg_ref, cbb_ref = src[2:5] + src[7:11]
        win_scr, y_scr, uc_scr = refs[nsrc + 12:]
        ctx = tile < npt
        t = (tile - npt) % tps
        half = u_ref.shape[0] // 2
        starts = jnp.logical_or(ctx, jnp.logical_and(t == 0, r == 0))
        ends = jnp.logical_or(ctx, jnp.logical_and(t == tps - 1, r == 1))
        left = ul_ref[...] if r == 0 else u_ref[half - HALO:half, :]
        right = u_ref[half:half + HALO, :] if r == 0 else ur_ref[...]
        _conv_window(jnp.where(starts, 0.0, left), u_ref[r * half:(r + 1) * half, :],
                     jnp.where(ends, 0.0, right), cw_ref, cb_ref, cg_ref, cbb_ref,
                     win_scr.at[r], y_scr.at[r], uc_scr.at[r])
        return uc_scr[r]

    @pl.when(s >= ncast)
    def _():
        tile = s - ncast + off
        ctx = tile < npt
        mod = mod_ref[...]
        w_out = wo_scr[...].reshape(D_MODEL, D_MODEL)
        nchunk = w1_scr.shape[0]
        rows_per = w2_scr.shape[0] // nchunk
        half = o_ref.shape[0] // 2
        for r in range(2):
            rows = slice(r * half, (r + 1) * half)
            if mode == "a":
                ap_ref, as_ref, xp_ref, xs_ref = src[0], src[1], src[5], src[6]
                att = jnp.where(ctx, ap_ref[rows, :], as_ref[rows, :])
                x = jnp.where(ctx, xp_ref[rows, :], xs_ref[rows, :])
                y = jnp.dot(att, w_out[0:ATT_W, :], preferred_element_type=F32)
                y = y + jnp.dot(conv_half(r, tile), w_out[ATT_W:, :], preferred_element_type=F32)
            else:
                hp_ref, lo_ref, hi_ref, x_ref = src
                upper = ((tile - npt) % tps) >= tps // 2
                hg = jnp.where(ctx, hp_ref[rows, :], jnp.where(upper, hi_ref[rows, :], lo_ref[rows, :]))
                x = x_ref[rows, :]
                y = jnp.dot(hg, w_out, preferred_element_type=F32)
            x1 = _layernorm(ALPHA * x + mod[2:3] * y, g0_ref[...], b0_ref[...])
            h = (x1 * (1.0 + mod[4:5]) + mod[3:4]).astype(BF16)
            acc = None
            for c in range(nchunk):
                a = jnp.maximum(jnp.dot(h, w1_scr[c], preferred_element_type=F32), 0.0)
                w2 = w2_scr[c * rows_per:(c + 1) * rows_per].reshape(TF_FFN, D_MODEL)
                t = jnp.dot((a * a).astype(BF16), w2, preferred_element_type=F32)
                acc = t if acc is None else acc + t
            o_ref[rows, :] = _layernorm(ALPHA * x1 + mod[5:6] * acc, g1_ref[...], b1_ref[...])


def _block_tail(mode, src, row0, nrows, mod, w_out, w1, w2, layer, ln0, ln1):
    tm = TM_PROJ
    ncast = TAIL_CAST_STEPS
    npt = NP // tm
    tps = DEC_SEQ // tm
    hps = tps // 2
    off = row0 // tm
    ntiles = nrows // tm
    tile = lambda s: jnp.maximum(s - ncast, 0) + off
    blk = lambda s: jnp.minimum(s, ncast - 1)
    first = lambda s: (jnp.minimum(tile(s), npt - 1), 0)
    second = lambda s: (jnp.maximum(tile(s) - npt, 0), 0)

    def half_idx(s, upper):
        t0 = jnp.maximum(tile(s) - npt, 0)
        t = t0 % tps
        t = jnp.maximum(t - hps, 0) if upper else jnp.minimum(t, hps - 1)
        return ((t0 // tps) * hps + t, 0)

    scratch = []
    if mode == "a":
        assert tm == 2 * R_CONV
        att_p, att_s, u, xp, xs, conv_w, conv_b, cln_g, cln_b = src
        hb = tm // HALO
        nh = NT // HALO
        cvec = pl.BlockSpec((1, CONV_CH), lambda s: (0, 0))
        src = (att_p, att_s, u, u, u, xp, xs,
               jnp.concatenate([conv_w, jnp.zeros((1, CONV_CH), F32)], axis=0),
               conv_b.reshape(1, CONV_CH), cln_g.reshape(1, CONV_CH), cln_b.reshape(1, CONV_CH))
        src_specs = [
            pl.BlockSpec((tm, ATT_W), first),
            pl.BlockSpec((tm, ATT_W), second),
            pl.BlockSpec((HALO, CONV_CH), lambda s: (jnp.maximum(tile(s) * hb - 1, 0), 0)),
            pl.BlockSpec((tm, CONV_CH), lambda s: (tile(s), 0)),
            pl.BlockSpec((HALO, CONV_CH), lambda s: (jnp.minimum((tile(s) + 1) * hb, nh - 1), 0)),
            pl.BlockSpec((tm, D_MODEL), first),
            pl.BlockSpec((tm, D_MODEL), second),
            pl.BlockSpec((CONV_K + 1, CONV_CH), lambda s: (0, 0)),
            cvec, cvec, cvec,
        ]
        scratch = [pltpu.VMEM((2, R_CONV + 2 * HALO, CONV_CH), F32), pltpu.VMEM((2, R_CONV, CONV_CH), F32),
                   pltpu.VMEM((2, R_CONV, CONV_CH), BF16)]
    else:
        src_specs = [
            pl.BlockSpec((tm, M_INNER), first),
            pl.BlockSpec((tm, M_INNER), lambda s: half_idx(s, False)),
            pl.BlockSpec((tm, M_INNER), lambda s: half_idx(s, True)),
            pl.BlockSpec((tm, D_MODEL), lambda s: (tile(s), 0)),
        ]
    nchunk = D_FF // TF_FFN
    vec = pl.BlockSpec((1, D_MODEL), lambda s: (0, 0))
    return pl.pallas_call(
        functools.partial(_tail_kernel, mode=mode, ncast=ncast, npt=npt, tps=tps, off=off),
        out_shape=jax.ShapeDtypeStruct((nrows, D_MODEL), F32),
        grid=(ncast + ntiles,),
        in_specs=src_specs + [
            pl.BlockSpec((None, 6, D_MODEL), lambda s: (_cond_index(tile(s), tm), 0, 0)),
            pl.BlockSpec((None, D_MODEL // ncast, D_MODEL), lambda s: (0, blk(s), 0)),
            pl.BlockSpec((None, D_MODEL, D_FF // ncast), lambda s: (layer, 0, blk(s))),
            pl.BlockSpec((None, D_FF // ncast, D_MODEL), lambda s: (layer, blk(s), 0)),
            vec, vec, vec, vec,
        ],
        out_specs=pl.BlockSpec((tm, D_MODEL), lambda s: (jnp.maximum(s - ncast, 0), 0)),
        scratch_shapes=[
            pltpu.VMEM((ncast, D_MODEL // ncast, D_MODEL), BF16),
            pltpu.VMEM((nchunk, D_MODEL, TF_FFN), BF16),
            pltpu.VMEM((ncast, D_FF // ncast, D_MODEL), BF16),
        ] + scratch,
        compiler_params=_params(("arbitrary",)),
        name="tail_" + mode,
    )(*src, mod, w_out, w1, w2, *ln0, *ln1)


def _inproj_m_kernel(x_ref, mod_ref, w_ref, wg_ref, bg_ref, o_ref, gt_ref, vt_ref):
    nslab = vt_ref.shape[0]
    L = L_CHUNK
    mod = mod_ref[...]
    h = (x_ref[...] * (1.0 + mod[1:2]) + mod[0:1]).astype(BF16)

    gates = jnp.dot(h, wg_ref[...], preferred_element_type=F32) + bg_ref[...]
    kind = lax.broadcasted_iota(jnp.int32, gates.shape, 1) % GATE_STRIDE
    gates = jnp.where((kind == 2) | (kind == 3), _log_sigmoid(gates), gates)
    gates_t = gates.T
    for s in range(nslab):
        gt_ref[s] = gates_t[0:GATE_ROWS, s * L:(s + 1) * L]

    for j in range(4):
        cols = slice(j * M_INNER, (j + 1) * M_INNER)
        p = jnp.dot(h, w_ref[:, cols], preferred_element_type=F32)
        if j == 1:
            p = p * (M_DK ** -0.5)
        o_ref[:, cols] = p.astype(BF16)
        if j == 2:
            v_t = p.T.astype(BF16)
            for s in range(nslab):
                vt_ref[s] = v_t[:, s * L:(s + 1) * L]


def _inproj_m(x, mod, w, wg, bg):
    tm = TM_PROJ
    nslab = tm // L_CHUNK
    return pl.pallas_call(
        _inproj_m_kernel,
        out_shape=[
            jax.ShapeDtypeStruct((NT, 4 * M_INNER), BF16),
            jax.ShapeDtypeStruct((NT // L_CHUNK, GATE_ROWS, L_CHUNK), F32),
            jax.ShapeDtypeStruct((NT // L_CHUNK, M_INNER, L_CHUNK), BF16),
        ],
        grid=(NT // tm,),
        in_specs=[
            pl.BlockSpec((tm, D_MODEL), lambda i: (i, 0)),
            pl.BlockSpec((None, 6, D_MODEL), lambda i: (_cond_index(i, tm), 0, 0)),
            pl.BlockSpec((D_MODEL, 4 * M_INNER), lambda i: (0, 0)),
            pl.BlockSpec((D_MODEL, LANES), lambda i: (0, 0)),
            pl.BlockSpec((1, LANES), lambda i: (0, 0)),
        ],
        out_specs=[
            pl.BlockSpec((tm, 4 * M_INNER), lambda i: (i, 0)),
            pl.BlockSpec((nslab, GATE_ROWS, L_CHUNK), lambda i: (i, 0, 0)),
            pl.BlockSpec((nslab, M_INNER, L_CHUNK), lambda i: (i, 0, 0)),
        ],
        compiler_params=_params(("arbitrary",)),
        name="inproj_m",
    )(x, mod, w, wg, bg)


def _log_sigmoid(x):
    return jnp.minimum(x, 0.0) - jnp.log1p(jnp.exp(-jnp.abs(x)))


def _split3(x):
    x1 = x.astype(BF16)
    r1 = x - x1.astype(F32)
    x2 = r1.astype(BF16)
    x3 = (r1 - x2.astype(F32)).astype(BF16)
    return x1, x2, x3


def _mlstm_prefix(gt):
    L = gt.shape[1]
    parts = jnp.concatenate([t.astype(F32) for t in _split3(gt)]
                            + [jnp.zeros((GATE_ROWS, L), F32)], axis=0).astype(BF16)
    upper = jnp.where(lax.broadcasted_iota(jnp.int32, (L, L), 0)
                      <= lax.broadcasted_iota(jnp.int32, (L, L), 1), 1.0, 0.0).astype(BF16)
    r = jnp.dot(parts, upper, preferred_element_type=F32)
    return r[0:GATE_ROWS] + r[GATE_ROWS:2 * GATE_ROWS] + r[2 * GATE_ROWS:3 * GATE_ROWS]


def _rows3(x, n):
    g = x.shape[0]
    return jnp.concatenate([t.astype(F32) for t in _split3(x)]
                           + [jnp.zeros((g, ONES_ROWS - 3, n), F32)], axis=1).astype(BF16)


def _sum3(x, r0):
    return x[:, r0:r0 + 1] + x[:, r0 + 1:r0 + 2] + x[:, r0 + 2:r0 + 3]


def _mlstm_group(rev, q, k, vt, gt, pre, state):
    G, L = q.shape[0], q.shape[1]
    d = 1 if rev else 0
    tpos = lax.broadcasted_iota(jnp.int32, (L, L), 0)
    spos = lax.broadcasted_iota(jnp.int32, (L, L), 1)
    keep = (tpos >= spos) if rev else (tpos <= spos)
    if rev:
        tot = pre[:, 3:4, L - 1:L]
        a_row = tot - pre[:, 3:4, :] + gt[:, 3:4, :]
    else:
        a_row = pre[:, 2:3, :]
        tot = a_row[:, :, L - 1:L]
    b_row = gt[:, d:d + 1, :] - a_row
    m = jnp.zeros((G, 1, 1), F32) if state is None else state[0]

    gp = -(-G // 8) * 8
    rows = [b_row[p] for p in range(G)] + [jnp.zeros((gp - G, L), F32)] * (1 if gp > G else 0)
    bt = jnp.concatenate(rows, axis=0).T
    bm = jnp.stack([jnp.where(keep, bt[:, p:p + 1], -jnp.inf) for p in range(G)])
    mm = jnp.maximum(m, jnp.max(bm, axis=1, keepdims=True))
    w = jnp.exp(bm - mm)
    s = jnp.einsum('gsd,gtd->gst', k, q, preferred_element_type=F32)
    qk = s * w
    num = jnp.einsum('gvs,gst->gvt', vt, qk.astype(BF16), preferred_element_type=F32)
    den = jnp.sum(qk, axis=1, keepdims=True)
    if state is not None:
        _, ct, n = state
        cn = jnp.concatenate([ct.astype(BF16), _rows3(n, n.shape[2])], axis=1)
        inter = jnp.einsum('gcd,gtd->gct', cn, q, preferred_element_type=F32)
        s_inter = jnp.exp(m - mm)
        num = num + s_inter * inter[:, :M_DV]
        den = den + s_inter * _sum3(inter, M_DV)
    h = num * (1.0 / jnp.maximum(jnp.abs(den), jnp.exp(-(a_row + mm))))

    g_row = tot + b_row
    m_new = jnp.maximum(tot + m, jnp.max(g_row, axis=2, keepdims=True))
    ws = jnp.exp(g_row - m_new)
    lhs = jnp.concatenate([(vt.astype(F32) * ws).astype(BF16), _rows3(ws, L)], axis=1)
    upd = jnp.einsum('gcs,gsd->gcd', lhs, k, preferred_element_type=F32)
    ct_new = upd[:, :M_DV]
    n_new = _sum3(upd, M_DV)
    if state is not None:
        decay = jnp.exp(tot + m - m_new)
        ct_new = decay * ct + ct_new
        n_new = decay * n + n_new
    return h, (m_new, ct_new, n_new)


def _heads(x):
    w = x.shape[1] // M_HEADS
    return jnp.stack([x[:, h * w:(h + 1) * w] for h in range(M_HEADS)])


def _mlstm_finish(ht, o, gain):
    ms = jnp.mean(ht * ht, axis=1, keepdims=True)
    hn = ht * lax.rsqrt(ms + EPS)
    hn = jnp.concatenate([hn[h].T for h in range(M_HEADS)], axis=1)
    return (_sigmoid(o.astype(F32)) * hn * gain).astype(BF16)


def _mlstm_sample_kernel(qf_ref, kf_ref, vtf_ref, gtf_ref, of_ref, qb_ref, kb_ref, vtb_ref, gtb_ref,
                         ob_ref, gain_ref, c0_ref, n0_ref, m0_ref, hi_ref, lo_ref,
                         hs_ref, ct_scr, n_scr, m_scr, *, nc):
    j = pl.program_id(1)
    jb = nc - 1 - j
    H = M_HEADS
    L = L_CHUNK

    @pl.when(j == 0)
    def _():
        for d in range(2):
            for h in range(H):
                ct_scr[d * H + h] = c0_ref[d, h].T
                n_scr[d * H + h] = n0_ref[d, h]
                m_scr[d * H + h] = jnp.broadcast_to(m0_ref[d, h], (1, LANES))

    def run(rev, q_ref, k_ref, vt_ref, gt_ref, lo):
        gt = gt_ref[...]
        pre = _mlstm_prefix(gt).reshape(H, GATE_STRIDE, L)
        state = (m_scr[lo:lo + H][:, :, 0:1], ct_scr[lo:lo + H], n_scr[lo:lo + H])
        h, (m_new, ct_new, n_new) = _mlstm_group(
            rev, _heads(q_ref[...]), _heads(k_ref[...]), vt_ref[...].reshape(H, M_DV, L),
            gt.reshape(H, GATE_STRIDE, L), pre, state)
        ct_scr[lo:lo + H] = ct_new
        n_scr[lo:lo + H] = n_new
        m_scr[lo:lo + H] = jnp.broadcast_to(m_new, (H, 1, LANES))
        return h

    h_f = run(False, qf_ref, kf_ref, vtf_ref, gtf_ref, 0)
    h_b = run(True, qb_ref, kb_ref, vtb_ref, gtb_ref, H)

    @pl.when(j < nc // 2)
    def _():
        hs_ref[j] = h_f
        hs_ref[jb] = h_b

    @pl.when(j >= nc // 2)
    def _():
        gain = gain_ref[...]
        hi_ref[...] = _mlstm_finish(hs_ref[j] + h_f, of_ref[...], gain)
        lo_ref[...] = _mlstm_finish(hs_ref[jb] + h_b, ob_ref[...], gain)


def _mlstm_sample(qkvo, vt, gates_t, gain, init):
    nc = DEC_SEQ // L_CHUNK
    half = nc // 2
    c0 = NP // L_CHUNK
    L = L_CHUNK
    fwd = lambda b, j: c0 + b * nc + j
    bwd = lambda b, j: c0 + b * nc + nc - 1 - j

    def specs(f):
        return [
            pl.BlockSpec((L, M_INNER), lambda b, j: (f(b, j), 0)),
            pl.BlockSpec((L, M_INNER), lambda b, j: (f(b, j), 1)),
            pl.BlockSpec((None, M_INNER, L), lambda b, j: (f(b, j), 0, 0)),
            pl.BlockSpec((None, GATE_ROWS, L), lambda b, j: (f(b, j), 0, 0)),
            pl.BlockSpec((L, M_INNER), lambda b, j: (f(b, j), 3)),
        ]

    state_specs = [
        pl.BlockSpec((None, 2, M_HEADS, M_DK, M_DV), lambda b, j: (b, 0, 0, 0, 0)),
        pl.BlockSpec((None, 2, M_HEADS, 1, M_DK), lambda b, j: (b, 0, 0, 0, 0)),
        pl.BlockSpec((None, 2, M_HEADS, 1, 1), lambda b, j: (b, 0, 0, 0, 0)),
    ]
    out_rows = DEC_BATCH * DEC_SEQ // 2
    return pl.pallas_call(
        functools.partial(_mlstm_sample_kernel, nc=nc),
        out_shape=[jax.ShapeDtypeStruct((out_rows, M_INNER), BF16)] * 2,
        grid=(DEC_BATCH, nc),
        in_specs=specs(fwd) + specs(bwd) + [pl.BlockSpec((1, M_INNER), lambda b, j: (0, 0))] + state_specs,
        out_specs=[
            pl.BlockSpec((L, M_INNER), lambda b, j: (b * half + jnp.maximum(j, half) - half, 0)),
            pl.BlockSpec((L, M_INNER), lambda b, j: (b * half + jnp.minimum(nc - 1 - j, half - 1), 0)),
        ],
        scratch_shapes=[
            pltpu.VMEM((nc, M_HEADS, M_DV, L), F32),
            pltpu.VMEM((2 * M_HEADS, M_DV, M_DK), F32),
            pltpu.VMEM((2 * M_HEADS, 1, M_DK), F32),
            pltpu.VMEM((2 * M_HEADS, 1, LANES), F32),
        ],
        compiler_params=_params(("arbitrary", "arbitrary")),
        name="mlstm",
    )(qkvo, qkvo, vt, gates_t, qkvo, qkvo, qkvo, vt, gates_t, qkvo, gain, *init)


def _mlstm_prompt_kernel(q_ref, k_ref, vt_ref, gt_ref, o_ref, gain_ref, out_ref, co_ref, no_ref, mo_ref,
                         *, nseq):
    H = M_HEADS
    L = L_CHUNK
    G = nseq * H
    q = jnp.concatenate([_heads(q_ref[b * L:(b + 1) * L, :]) for b in range(nseq)], axis=0)
    k = jnp.concatenate([_heads(k_ref[b * L:(b + 1) * L, :]) for b in range(nseq)], axis=0)
    vt = vt_ref[...].reshape(G, M_DV, L)
    gt3 = jnp.concatenate([gt_ref[b] for b in range(nseq)], axis=0).reshape(G, GATE_STRIDE, L)
    pre3 = jnp.concatenate([_mlstm_prefix(gt_ref[b]) for b in range(nseq)],
                           axis=0).reshape(G, GATE_STRIDE, L)
    gain = gain_ref[...]
    hs = None
    for d in range(2):
        h, (m_new, ct_new, n_new) = _mlstm_group(d == 1, q, k, vt, gt3, pre3, None)
        hs = h if hs is None else hs + h
        for b in range(nseq):
            for hh in range(H):
                co_ref[b, d, hh] = ct_new[b * H + hh].T
                no_ref[b, d, hh] = n_new[b * H + hh]
                mo_ref[b, d, hh] = m_new[b * H + hh]
    for b in range(nseq):
        out_ref[b * L:(b + 1) * L, :] = _mlstm_finish(hs[b * H:(b + 1) * H], o_ref[b * L:(b + 1) * L, :], gain)


def _mlstm_prompt(qkvo, vt, gates_t, gain):
    nseq = MLSTM_SEQS
    L = L_CHUNK
    rows = nseq * L
    state_specs = [
        pl.BlockSpec((nseq, 2, M_HEADS, M_DK, M_DV), lambda i: (i, 0, 0, 0, 0)),
        pl.BlockSpec((nseq, 2, M_HEADS, 1, M_DK), lambda i: (i, 0, 0, 0, 0)),
        pl.BlockSpec((nseq, 2, M_HEADS, 1, 1), lambda i: (i, 0, 0, 0, 0)),
    ]
    return pl.pallas_call(
        functools.partial(_mlstm_prompt_kernel, nseq=nseq),
        out_shape=[
            jax.ShapeDtypeStruct((NP, M_INNER), BF16),
            jax.ShapeDtypeStruct((BATCH, 2, M_HEADS, M_DK, M_DV), F32),
            jax.ShapeDtypeStruct((BATCH, 2, M_HEADS, 1, M_DK), F32),
            jax.ShapeDtypeStruct((BATCH, 2, M_HEADS, 1, 1), F32),
        ],
        grid=(BATCH // nseq,),
        in_specs=[
            pl.BlockSpec((rows, M_INNER), lambda i: (i, 0)),
            pl.BlockSpec((rows, M_INNER), lambda i: (i, 1)),
            pl.BlockSpec((nseq, M_INNER, L), lambda i: (i, 0, 0)),
            pl.BlockSpec((nseq, GATE_ROWS, L), lambda i: (i, 0, 0)),
            pl.BlockSpec((rows, M_INNER), lambda i: (i, 3)),
            pl.BlockSpec((1, M_INNER), lambda i: (0, 0)),
        ],
        out_specs=[pl.BlockSpec((rows, M_INNER), lambda i: (i, 0))] + state_specs,
        compiler_params=_params(("arbitrary",)),
        name="mlstm_state",
    )(qkvo, qkvo, vt, gates_t, qkvo, gain)


@jax.jit
def kernel(x_prompt, x_sample, cache_k, cache_v, state_c, state_n, state_m, c, c_ctx, w_ada, b_ada,
           ln_g, ln_b, w_ff1, w_ff2, w_in_a, q_gain, k_gain, conv_w, conv_b, conv_ln_g, conv_ln_b,
           w_out_a, w_in_m, b_gate_m, mh_gain, w_out_m):
    xp = x_prompt.reshape(NP, D_MODEL)
    xs = x_sample.reshape(NS, D_MODEL)

    cond = jnp.concatenate(
        [c_ctx[None, :], c, jnp.zeros((N_COND - 1 - DEC_BATCH, D_MODEL), F32)], axis=0)
    mods = _adaln(cond, w_ada, b_ada)

    ln = lambda l, s: (ln_g[l, s].reshape(1, D_MODEL), ln_b[l, s].reshape(1, D_MODEL))

    q, k, v, kf, vf, u = _inproj_a(xp, xs, mods[0], w_in_a, q_gain[0], k_gain[0])
    att_p = _attention_prompt(q, k, v)
    ctx_k = cache_k[:, 0].reshape(DEC_BATCH, PAST_LEN, KV_W)
    ctx_v = cache_v[:, 0].reshape(DEC_BATCH, PAST_LEN, KV_W)
    att_s = _attention_sample(q, k, v, ctx_k, ctx_v)
    tail_src = (att_p, att_s, u, xp, xs, conv_w[0], conv_b[0], conv_ln_g[0], conv_ln_b[0])
    x2 = _block_tail("a", tail_src, 0, NT, mods[0], w_out_a, w_ff1, w_ff2, 0, ln(0, 0), ln(0, 1))

    wm = w_in_m[0]
    src = jnp.array([4 * M_INNER + g * M_HEADS + h
                     for h in range(M_HEADS) for g in (0, 2, 1, 3)], jnp.int32)
    dst = jnp.array([h * GATE_STRIDE + t for h in range(M_HEADS) for t in range(4)], jnp.int32)
    wg = jnp.zeros((D_MODEL, LANES), F32).at[:, dst].set(wm[:, src]).astype(BF16)
    bg = jnp.zeros((1, LANES), F32).at[0, dst].set(b_gate_m[0][src - 4 * M_INNER])
    qkvo, gates_t, vt_m = _inproj_m(x2, mods[1], wm[:, :4 * M_INNER].astype(BF16), wg, bg)
    gain = mh_gain[0].reshape(1, M_INNER)
    hg_p, st_c, st_n, st_m = _mlstm_prompt(qkvo, vt_m, gates_t, gain)
    init = (state_c[:, 0], state_n[:, 0].reshape(DEC_BATCH, 2, M_HEADS, 1, M_DK),
            state_m[:, 0].reshape(DEC_BATCH, 2, M_HEADS, 1, 1))
    hg_hi, hg_lo = _mlstm_sample(qkvo, vt_m, gates_t, gain, init)
    tail_src = (hg_p, hg_lo, hg_hi, x2)
    y_p = _block_tail("m", tail_src, 0, NP, mods[1], w_out_m, w_ff1, w_ff2, 1, ln(1, 0), ln(1, 1))
    y_s = _block_tail("m", tail_src, NP, NS, mods[1], w_out_m, w_ff1, w_ff2, 1, ln(1, 0), ln(1, 1))

    new_k = kf[:NP].reshape(BATCH, 1, SEQ, ATT_KV_HEADS, HEAD_DIM)
    new_v = vf[:NP].reshape(BATCH, 1, SEQ, ATT_KV_HEADS, HEAD_DIM)
    return (y_p.reshape(BATCH, SEQ, D_MODEL), y_s.reshape(DEC_BATCH, DEC_SEQ, D_MODEL),
            new_k, new_v,
            st_c.reshape(BATCH, 1, 2, M_HEADS, M_DK, M_DV),
            st_n.reshape(BATCH, 1, 2, M_HEADS, M_DK),
            st_m.reshape(BATCH, 1, 2, M_HEADS))
```

```python
import functools

import jax
import jax.numpy as jnp
from jax import lax
from jax.experimental import pallas as pl
from jax.experimental.pallas import tpu as pltpu

F32 = jnp.float32
BF16 = jnp.bfloat16

D_MODEL = 1024
BATCH = 32
SEQ = 256
DEPTH = 2
DEC_BATCH = 4
DEC_SEQ = 4096
PAST_LEN = 512
GRID_W = 64
ATT_HEADS = 8
ATT_KV_HEADS = 2
HEAD_DIM = 64
ATT_GROUP = ATT_HEADS // ATT_KV_HEADS
ATT_W = ATT_HEADS * HEAD_DIM
KV_W = ATT_KV_HEADS * HEAD_DIM
ROPE_AXIS_DIM = HEAD_DIM // 2
ROPE_THETA = 10000.0
CONV_CH = D_MODEL // 2
CONV_K = 31
EVEN_IN = ATT_W + 2 * KV_W + 2 * CONV_CH
M_HEADS = 4
M_INNER = D_MODEL
M_DK = M_INNER // M_HEADS
M_DV = M_INNER // M_HEADS
D_FF = 4 * D_MODEL
ALPHA = (2 * DEPTH) ** 0.25
EPS = 1e-6
LOG2E = 1.4426950408889634

NP = BATCH * SEQ
NS = DEC_BATCH * DEC_SEQ
NT = NP + NS
N_COND = 8

LANES = 128
VMEM_LIMIT = 56 * 1024 * 1024

TM_PROJ = 512
TF_FFN = 1024
TQ_ATT = 256
TK_ATT = 512
R_CONV = 256
HALO = 16
L_CHUNK = 256
GATE_STRIDE = 8
ONES_ROWS = 16
GATE_ROWS = 32
MLSTM_SEQS = 4
TAIL_CAST_STEPS = 16
INPROJ_SUBTILES = 4
ATT_UNROLL = 9
ATT_PROMPT_SEQS = 1


def _cond_index(i_global, tm):
    npt = NP // tm
    tps = DEC_SEQ // tm
    return jnp.where(i_global < npt, 0, 1 + (i_global - npt) // tps)


def _layernorm(r, g, b):
    mu = jnp.mean(r, axis=-1, keepdims=True)
    d = r - mu
    var = jnp.mean(d * d, axis=-1, keepdims=True)
    return d * lax.rsqrt(var + EPS) * g + b


def _sigmoid(x):
    return 1.0 / (1.0 + jnp.exp(-x))


def _params(sem, vmem=VMEM_LIMIT, flags=None):
    return pltpu.CompilerParams(dimension_semantics=sem, vmem_limit_bytes=vmem, flags=flags)


def _adaln_kernel(cond_ref, w_ref, b_ref, o_ref):
    c = cond_ref[...]
    s = (c * _sigmoid(c)).astype(BF16)
    o_ref[...] = jnp.dot(s, w_ref[...].astype(BF16), preferred_element_type=F32) + b_ref[...]


def _adaln(cond, w_ada, b_ada):
    tn = 1536
    n = 6 * D_MODEL
    out = pl.pallas_call(
        _adaln_kernel,
        out_shape=jax.ShapeDtypeStruct((DEPTH, N_COND, n), F32),
        grid=(DEPTH, n // tn),
        in_specs=[
            pl.BlockSpec((N_COND, D_MODEL), lambda l, j: (0, 0)),
            pl.BlockSpec((None, D_MODEL, tn), lambda l, j: (l, 0, j)),
            pl.BlockSpec((None, 1, tn), lambda l, j: (l, 0, j)),
        ],
        out_specs=pl.BlockSpec((None, N_COND, tn), lambda l, j: (l, 0, j)),
        compiler_params=_params(("arbitrary", "arbitrary")),
        name="adaln",
    )(cond, w_ada, b_ada.reshape(DEPTH, 1, n))
    return out.reshape(DEPTH, N_COND, 6, D_MODEL)


def _cast_once(w_ref, w_scr):
    @pl.when(pl.program_id(0) == 0)
    def _():
        w_scr[...] = w_ref[...].astype(BF16)


def _inproj_a_kernel(xp_ref, xs_ref, mod_ref, w_ref, qg_ref, kg_ref, cos_ref, sin_ref,
                     q_ref, k_ref, v_ref, kf_ref, vf_ref, u_ref, w_scr, *, npt):
    i = pl.program_id(0)
    _cast_once(w_ref, w_scr)
    mod = mod_ref[...]
    tm = u_ref.shape[0]
    sub = tm // INPROJ_SUBTILES

    ri = lax.broadcasted_iota(jnp.int32, (LANES, LANES), 0) // HEAD_DIM
    ci = lax.broadcasted_iota(jnp.int32, (LANES, LANES), 1) // HEAD_DIM
    seg = jnp.where(ri == ci, 1.0, 0.0).astype(BF16)
    lane = lax.broadcasted_iota(jnp.int32, (sub, LANES), 1)
    even = (lane % 2) == 0
    qg = qg_ref[...]
    kg = kg_ref[...]

    def norm(xc, gain):
        ss = jnp.dot((xc * xc).astype(BF16), seg, preferred_element_type=F32)
        return xc * lax.rsqrt(ss * (1.0 / HEAD_DIM) + EPS) * gain

    cache_rows = []
    for r in range(INPROJ_SUBTILES):
        rows = slice(r * sub, (r + 1) * sub)
        x = jnp.where(i < npt, xp_ref[rows, :], xs_ref[rows, :])
        h = (x * (1.0 + mod[1:2]) + mod[0:1]).astype(BF16)
        proj = jnp.dot(h, w_scr[...], preferred_element_type=F32)
        cos = cos_ref[rows, :]
        sin = sin_ref[rows, :]

        def rope(xn):
            partner = jnp.where(even, pltpu.roll(xn, LANES - 1, 1), pltpu.roll(xn, 1, 1))
            return xn * cos + partner * sin

        for c in range(ATT_W // LANES):
            qr = rope(norm(proj[:, c * LANES:(c + 1) * LANES], qg))
            qs = (qr * (HEAD_DIM ** -0.5 * LOG2E)).astype(BF16)
            q_ref[2 * c, rows, :] = qs[:, :HEAD_DIM]
            q_ref[2 * c + 1, rows, :] = qs[:, HEAD_DIM:]

        kn = norm(proj[:, ATT_W:ATT_W + KV_W], kg)
        kr = rope(kn).astype(BF16)
        k_ref[0, rows, :] = kr[:, :HEAD_DIM]
        k_ref[1, rows, :] = kr[:, HEAD_DIM:]

        v = proj[:, ATT_W + KV_W:ATT_W + 2 * KV_W]
        cache_rows.append((kn, v))
        vb = v.astype(BF16)
        v_ref[0, rows, :] = vb[:, :HEAD_DIM]
        v_ref[1, rows, :] = vb[:, HEAD_DIM:]

        off = ATT_W + 2 * KV_W
        a = proj[:, off:off + CONV_CH]
        gt = proj[:, off + CONV_CH:off + 2 * CONV_CH]
        u_ref[rows, :] = a * _sigmoid(gt)

    @pl.when(i < npt)
    def _():
        for r, (kn, v) in enumerate(cache_rows):
            j, t0 = (r * sub) // SEQ, (r * sub) % SEQ
            kf_ref[j, :, :, t0:t0 + sub] = kn.T.reshape(ATT_KV_HEADS, HEAD_DIM, sub)
            vf_ref[j, :, :, t0:t0 + sub] = v.T.reshape(ATT_KV_HEADS, HEAD_DIM, sub)


def _rope_tables(tm):
    t = jnp.arange(DEC_SEQ)
    row = (t // GRID_W).astype(F32)
    col = (t % GRID_W).astype(F32)
    freqs = ROPE_THETA ** (-jnp.arange(0, ROPE_AXIS_DIM, 2, dtype=F32) / ROPE_AXIS_DIM)
    ang = jnp.concatenate([row[:, None] * freqs, col[:, None] * freqs], axis=-1)
    pair = (jnp.arange(LANES) % HEAD_DIM) // 2
    sign = jnp.where(jnp.arange(LANES) % 2 == 0, -1.0, 1.0).astype(F32)
    cos = jnp.cos(ang)[:, pair]
    sin = jnp.sin(ang)[:, pair] * sign
    cos = jnp.concatenate([jnp.ones((tm, LANES), F32), cos], axis=0)
    sin = jnp.concatenate([jnp.zeros((tm, LANES), F32), sin], axis=0)
    return cos, sin


def _inproj_a(xp, xs, mod, w, q_gain, k_gain):
    tm = TM_PROJ
    npt = NP // tm
    tps = DEC_SEQ // tm
    nt = NT // tm
    cos, sin = _rope_tables(tm)
    qg = jnp.tile(q_gain, LANES // HEAD_DIM).reshape(1, LANES)
    kg = jnp.tile(k_gain, LANES // HEAD_DIM).reshape(1, LANES)

    def rope_idx(i):
        return (jnp.where(i < npt, 0, 1 + (i - npt) % tps), 0)

    return pl.pallas_call(
        functools.partial(_inproj_a_kernel, npt=npt),
        out_shape=[
            jax.ShapeDtypeStruct((ATT_HEADS, NT, HEAD_DIM), BF16),
            jax.ShapeDtypeStruct((ATT_KV_HEADS, NT, HEAD_DIM), BF16),
            jax.ShapeDtypeStruct((ATT_KV_HEADS, NT, HEAD_DIM), BF16),
            jax.ShapeDtypeStruct((BATCH, ATT_KV_HEADS, HEAD_DIM, SEQ), F32),
            jax.ShapeDtypeStruct((BATCH, ATT_KV_HEADS, HEAD_DIM, SEQ), F32),
            jax.ShapeDtypeStruct((NT, CONV_CH), F32),
        ],
        grid=(nt,),
        in_specs=[
            pl.BlockSpec((tm, D_MODEL), lambda i: (jnp.minimum(i, npt - 1), 0)),
            pl.BlockSpec((tm, D_MODEL), lambda i: (jnp.maximum(i - npt, 0), 0)),
            pl.BlockSpec((None, 6, D_MODEL), lambda i: (_cond_index(i, tm), 0, 0)),
            pl.BlockSpec((None, D_MODEL, EVEN_IN), lambda i: (0, 0, 0)),
            pl.BlockSpec((1, LANES), lambda i: (0, 0)),
            pl.BlockSpec((1, LANES), lambda i: (0, 0)),
            pl.BlockSpec((tm, LANES), rope_idx),
            pl.BlockSpec((tm, LANES), rope_idx),
        ],
        out_specs=[
            pl.BlockSpec((ATT_HEADS, tm, HEAD_DIM), lambda i: (0, i, 0)),
            pl.BlockSpec((ATT_KV_HEADS, tm, HEAD_DIM), lambda i: (0, i, 0)),
            pl.BlockSpec((ATT_KV_HEADS, tm, HEAD_DIM), lambda i: (0, i, 0)),
            pl.BlockSpec((tm // SEQ, ATT_KV_HEADS, HEAD_DIM, SEQ), lambda i: (jnp.minimum(i, npt - 1), 0, 0, 0)),
            pl.BlockSpec((tm // SEQ, ATT_KV_HEADS, HEAD_DIM, SEQ), lambda i: (jnp.minimum(i, npt - 1), 0, 0, 0)),
            pl.BlockSpec((tm, CONV_CH), lambda i: (i, 0)),
        ],
        scratch_shapes=[pltpu.VMEM((D_MODEL, EVEN_IN), BF16)],
        compiler_params=_params(("arbitrary",)),
        name="inproj_a",
    )(xp, xs, mod, w, qg, kg, cos, sin)


def _attn_chunk(qs, kc, vc, m, acc):
    s = lax.dot_general(qs, kc, (((1,), (1,)), ((), ())), preferred_element_type=F32)
    fold = s[:, :LANES]
    for c in range(1, s.shape[1] // LANES):
        fold = jnp.maximum(fold, s[:, c * LANES:(c + 1) * LANES])
    m_new = jnp.maximum(m, jnp.max(fold, axis=1, keepdims=True))
    alpha = jnp.exp2(m - m_new)
    p = jnp.exp2(s - m_new).astype(BF16)
    va = jnp.concatenate([vc, jnp.ones((vc.shape[0], LANES - HEAD_DIM), BF16)], axis=1)
    acc = alpha * acc + jnp.dot(p, va, preferred_element_type=F32)
    return m_new, acc


def _attn_init(nq):
    return jnp.full((nq, 1), -1e30, F32), jnp.zeros((nq, LANES), F32)


def _attn_finish(acc, o_ref):
    tq = o_ref.shape[0]
    o = acc[:, :HEAD_DIM] * (1.0 / acc[:, HEAD_DIM:HEAD_DIM + 1])
    o_ref[...] = jnp.concatenate([o[h * tq:(h + 1) * tq] for h in range(ATT_GROUP)],
                                 axis=1).astype(BF16)


def _stack_heads(q_ref):
    return q_ref[...].reshape(ATT_GROUP * q_ref.shape[1], HEAD_DIM)


def _attn_prompt_kernel(q_ref, k_ref, v_ref, o_ref):
    for b in range(q_ref.shape[1] // SEQ):
        rows = slice(b * SEQ, (b + 1) * SEQ)
        qs = q_ref[:, rows, :].reshape(ATT_GROUP * SEQ, HEAD_DIM)
        _, acc = _attn_chunk(qs, k_ref[rows, :], v_ref[rows, :], *_attn_init(qs.shape[0]))
        _attn_finish(acc, o_ref.at[rows, :])


def _attn_sample_kernel(q_ref, k_ref, v_ref, ck_ref, cv_ref, o_ref, k_scr, v_scr, *, nchunks):
    @pl.when(pl.program_id(2) == 0)
    def _():
        own = k_ref.shape[0]
        k_scr[0:own, :] = k_ref[...]
        v_scr[0:own, :] = v_ref[...]
        first = pl.program_id(1) == 0
        ck = ck_ref[...]
        cv = cv_ref[...]
        k_scr[own:, :] = jnp.where(first, ck[:, :HEAD_DIM], ck[:, HEAD_DIM:]).astype(BF16)
        v_scr[own:, :] = jnp.where(first, cv[:, :HEAD_DIM], cv[:, HEAD_DIM:]).astype(BF16)

    qs = _stack_heads(q_ref)

    def body(j, carry):
        rows = pl.ds(pl.multiple_of(j * TK_ATT, TK_ATT), TK_ATT)
        return _attn_chunk(qs, k_scr[rows, :], v_scr[rows, :], *carry)

    _, acc = lax.fori_loop(0, nchunks, body, _attn_init(qs.shape[0]), unroll=ATT_UNROLL)
    _attn_finish(acc, o_ref)


def _attention_prompt(q, k, v):
    return pl.pallas_call(
        _attn_prompt_kernel,
        out_shape=jax.ShapeDtypeStruct((NP, ATT_W), BF16),
        grid=(BATCH // ATT_PROMPT_SEQS, ATT_KV_HEADS),
        in_specs=[
            pl.BlockSpec((ATT_GROUP, ATT_PROMPT_SEQS * SEQ, HEAD_DIM), lambda b, g: (g, b, 0)),
            pl.BlockSpec((None, ATT_PROMPT_SEQS * SEQ, HEAD_DIM), lambda b, g: (g, b, 0)),
            pl.BlockSpec((None, ATT_PROMPT_SEQS * SEQ, HEAD_DIM), lambda b, g: (g, b, 0)),
        ],
        out_specs=pl.BlockSpec((ATT_PROMPT_SEQS * SEQ, ATT_GROUP * HEAD_DIM), lambda b, g: (b, g)),
        compiler_params=_params(("arbitrary", "arbitrary")),
        name="attn_prompt",
    )(q, k, v)


def _attention_sample(q, k, v, ctx_k, ctx_v):
    tq = TQ_ATT
    nkeys = DEC_SEQ + PAST_LEN
    nchunks = nkeys // TK_ATT
    q_off = NP // tq
    kv_off = NP // DEC_SEQ
    return pl.pallas_call(
        functools.partial(_attn_sample_kernel, nchunks=nchunks),
        out_shape=jax.ShapeDtypeStruct((NS, ATT_W), BF16),
        grid=(DEC_BATCH, ATT_KV_HEADS, DEC_SEQ // tq),
        in_specs=[
            pl.BlockSpec((ATT_GROUP, tq, HEAD_DIM),
                         lambda b, g, i: (g, q_off + b * (DEC_SEQ // tq) + i, 0)),
            pl.BlockSpec((None, DEC_SEQ, HEAD_DIM), lambda b, g, i: (g, kv_off + b, 0)),
            pl.BlockSpec((None, DEC_SEQ, HEAD_DIM), lambda b, g, i: (g, kv_off + b, 0)),
            pl.BlockSpec((None, PAST_LEN, KV_W), lambda b, g, i: (b, 0, 0)),
            pl.BlockSpec((None, PAST_LEN, KV_W), lambda b, g, i: (b, 0, 0)),
        ],
        out_specs=pl.BlockSpec((tq, ATT_GROUP * HEAD_DIM),
                               lambda b, g, i: (b * (DEC_SEQ // tq) + i, g)),
        scratch_shapes=[pltpu.VMEM((nkeys, HEAD_DIM), BF16), pltpu.VMEM((nkeys, HEAD_DIM), BF16)],
        compiler_params=_params(("arbitrary", "arbitrary", "arbitrary")),
        name="attn_sample",
    )(q, k, v, ctx_k, ctx_v)


def _conv_window(left, cur, right, w_ref, b_ref, g_ref, bb_ref, win_ref, y_ref, out_ref):
    r = cur.shape[0]
    win_ref[0:HALO, :] = left
    win_ref[HALO:HALO + r, :] = cur
    win_ref[HALO + r:2 * HALO + r, :] = right
    base = HALO - CONV_K // 2
    sub = 8
    for c in range(CONV_CH // LANES):
        cs = slice(c * LANES, (c + 1) * LANES)
        acc = None
        for res in range(sub):
            p = None
            for k in range(CONV_K):
                if (base + k) % sub != res:
                    continue
                a = (base + k) - res
                term = win_ref[a:a + r + sub, cs] * w_ref[k:k + 1, cs]
                p = term if p is None else p + term
            if p is None:
                continue
            p = p[res:res + r]
            acc = p if acc is None else acc + p
        y_ref[:, cs] = acc + b_ref[:, cs]
    y = _layernorm(y_ref[...], g_ref[...], bb_ref[...])
    out_ref[...] = (y * _sigmoid(y)).astype(BF16)


def _tail_kernel(*refs, mode, ncast, npt, tps, off):
    nsrc = 11 if mode == "a" else 4
    src = refs[:nsrc]
    (mod_ref, wo_ref, w1_ref, w2_ref, g0_ref, b0_ref, g1_ref, b1_ref, o_ref,
     wo_scr, w1_scr, w2_scr) = refs[nsrc:nsrc + 12]
    s = pl.program_id(0)
    per = ncast // w1_scr.shape[0]
    wcol = w1_ref.shape[1]

    @pl.when(s < ncast)
    def _():
        wo_scr[s] = wo_ref[...].astype(BF16)
        w2_scr[s] = w2_ref[...].astype(BF16)

    for q in range(per):
        @pl.when(jnp.logical_and(s < ncast, s % per == q))
        def _():
            w1_scr[s // per, :, q * wcol:(q + 1) * wcol] = w1_ref[...].astype(BF16)

    def conv_half(r, tile):
        ul_ref, u_ref, ur_ref, cw_ref, cb_ref, cg_ref, cbb_ref = src[2:5] + src[7:11]
        win_scr, y_scr, uc_scr = refs[nsrc + 12:]
        ctx = tile < npt
        t = (tile - npt) % tps
        half = u_ref.shape[0] // 2
        starts = jnp.logical_or(ctx, jnp.logical_and(t == 0, r == 0))
        ends = jnp.logical_or(ctx, jnp.logical_and(t == tps - 1, r == 1))
        left = ul_ref[...] if r == 0 else u_ref[half - HALO:half, :]
        right = u_ref[half:half + HALO, :] if r == 0 else ur_ref[...]
        _conv_window(jnp.where(starts, 0.0, left), u_ref[r * half:(r + 1) * half, :],
                     jnp.where(ends, 0.0, right), cw_ref, cb_ref, cg_ref, cbb_ref,
                     win_scr.at[r], y_scr.at[r], uc_scr.at[r])
        return uc_scr[r]

    @pl.when(s >= ncast)
    def _():
        tile = s - ncast + off
        ctx = tile < npt
        mod = mod_ref[...]
        w_out = wo_scr[...].reshape(D_MODEL, D_MODEL)
        nchunk = w1_scr.shape[0]
        rows_per = w2_scr.shape[0] // nchunk
        half = o_ref.shape[0] // 2
        for r in range(2):
            rows = slice(r * half, (r + 1) * half)
            if mode == "a":
                ap_ref, as_ref, xp_ref, xs_ref = src[0], src[1], src[5], src[6]
                att = jnp.where(ctx, ap_ref[rows, :], as_ref[rows, :])
                x = jnp.where(ctx, xp_ref[rows, :], xs_ref[rows, :])
                y = jnp.dot(att, w_out[0:ATT_W, :], preferred_element_type=F32)
                y = y + jnp.dot(conv_half(r, tile), w_out[ATT_W:, :], preferred_element_type=F32)
            else:
                hp_ref, lo_ref, hi_ref, x_ref = src
                upper = ((tile - npt) % tps) >= tps // 2
                hg = jnp.where(ctx, hp_ref[rows, :], jnp.where(upper, hi_ref[rows, :], lo_ref[rows, :]))
                x = x_ref[rows, :]
                y = jnp.dot(hg, w_out, preferred_element_type=F32)
            x1 = _layernorm(ALPHA * x + mod[2:3] * y, g0_ref[...], b0_ref[...])
            h = (x1 * (1.0 + mod[4:5]) + mod[3:4]).astype(BF16)
            acc = None
            for c in range(nchunk):
                a = jnp.maximum(jnp.dot(h, w1_scr[c], preferred_element_type=F32), 0.0)
                w2 = w2_scr[c * rows_per:(c + 1) * rows_per].reshape(TF_FFN, D_MODEL)
                t = jnp.dot((a * a).astype(BF16), w2, preferred_element_type=F32)
                acc = t if acc is None else acc + t
            o_ref[rows, :] = _layernorm(ALPHA * x1 + mod[5:6] * acc, g1_ref[...], b1_ref[...])


def _block_tail(mode, src, row0, nrows, mod, w_out, w1, w2, layer, ln0, ln1):
    tm = TM_PROJ
    ncast = TAIL_CAST_STEPS
    npt = NP // tm
    tps = DEC_SEQ // tm
    hps = tps // 2
    off = row0 // tm
    ntiles = nrows // tm
    tile = lambda s: jnp.maximum(s - ncast, 0) + off
    blk = lambda s: jnp.minimum(s, ncast - 1)
    first = lambda s: (jnp.minimum(tile(s), npt - 1), 0)
    second = lambda s: (jnp.maximum(tile(s) - npt, 0), 0)

    def half_idx(s, upper):
        t0 = jnp.maximum(tile(s) - npt, 0)
        t = t0 % tps
        t = jnp.maximum(t - hps, 0) if upper else jnp.minimum(t, hps - 1)
        return ((t0 // tps) * hps + t, 0)

    scratch = []
    if mode == "a":
        assert tm == 2 * R_CONV
        att_p, att_s, u, xp, xs, conv_w, conv_b, cln_g, cln_b = src
        hb = tm // HALO
        nh = NT // HALO
        cvec = pl.BlockSpec((1, CONV_CH), lambda s: (0, 0))
        src = (att_p, att_s, u, u, u, xp, xs,
               jnp.concatenate([conv_w, jnp.zeros((1, CONV_CH), F32)], axis=0),
               conv_b.reshape(1, CONV_CH), cln_g.reshape(1, CONV_CH), cln_b.reshape(1, CONV_CH))
        src_specs = [
            pl.BlockSpec((tm, ATT_W), first),
            pl.BlockSpec((tm, ATT_W), second),
            pl.BlockSpec((HALO, CONV_CH), lambda s: (jnp.maximum(tile(s) * hb - 1, 0), 0)),
            pl.BlockSpec((tm, CONV_CH), lambda s: (tile(s), 0)),
            pl.BlockSpec((HALO, CONV_CH), lambda s: (jnp.minimum((tile(s) + 1) * hb, nh - 1), 0)),
            pl.BlockSpec((tm, D_MODEL), first),
            pl.BlockSpec((tm, D_MODEL), second),
            pl.BlockSpec((CONV_K + 1, CONV_CH), lambda s: (0, 0)),
            cvec, cvec, cvec,
        ]
        scratch = [pltpu.VMEM((2, R_CONV + 2 * HALO, CONV_CH), F32), pltpu.VMEM((2, R_CONV, CONV_CH), F32),
                   pltpu.VMEM((2, R_CONV, CONV_CH), BF16)]
    else:
        src_specs = [
            pl.BlockSpec((tm, M_INNER), first),
            pl.BlockSpec((tm, M_INNER), lambda s: half_idx(s, False)),
            pl.BlockSpec((tm, M_INNER), lambda s: half_idx(s, True)),
            pl.BlockSpec((tm, D_MODEL), lambda s: (tile(s), 0)),
        ]
    nchunk = D_FF // TF_FFN
    vec = pl.BlockSpec((1, D_MODEL), lambda s: (0, 0))
    return pl.pallas_call(
        functools.partial(_tail_kernel, mode=mode, ncast=ncast, npt=npt, tps=tps, off=off),
        out_shape=jax.ShapeDtypeStruct((nrows, D_MODEL), F32),
        grid=(ncast + ntiles,),
        in_specs=src_specs + [
            pl.BlockSpec((None, 6, D_MODEL), lambda s: (_cond_index(tile(s), tm), 0, 0)),
            pl.BlockSpec((None, D_MODEL // ncast, D_MODEL), lambda s: (0, blk(s), 0)),
            pl.BlockSpec((None, D_MODEL, D_FF // ncast), lambda s: (layer, 0, blk(s))),
            pl.BlockSpec((None, D_FF // ncast, D_MODEL), lambda s: (layer, blk(s), 0)),
            vec, vec, vec, vec,
        ],
        out_specs=pl.BlockSpec((tm, D_MODEL), lambda s: (jnp.maximum(s - ncast, 0), 0)),
        scratch_shapes=[
            pltpu.VMEM((ncast, D_MODEL // ncast, D_MODEL), BF16),
            pltpu.VMEM((nchunk, D_MODEL, TF_FFN), BF16),
            pltpu.VMEM((ncast, D_FF // ncast, D_MODEL), BF16),
        ] + scratch,
        compiler_params=_params(("arbitrary",)),
        name="tail_" + mode,
    )(*src, mod, w_out, w1, w2, *ln0, *ln1)


def _inproj_m_kernel(x_ref, mod_ref, w_ref, wg_ref, bg_ref, o_ref, gt_ref, vt_ref):
    nslab = vt_ref.shape[0]
    L = L_CHUNK
    mod = mod_ref[...]
    h = (x_ref[...] * (1.0 + mod[1:2]) + mod[0:1]).astype(BF16)

    gates = jnp.dot(h, wg_ref[...], preferred_element_type=F32) + bg_ref[...]
    kind = lax.broadcasted_iota(jnp.int32, gates.shape, 1) % GATE_STRIDE
    gates = jnp.where((kind == 2) | (kind == 3), _log_sigmoid(gates), gates)
    gates_t = gates.T
    for s in range(nslab):
        gt_ref[s] = gates_t[0:GATE_ROWS, s * L:(s + 1) * L]

    for j in range(4):
        cols = slice(j * M_INNER, (j + 1) * M_INNER)
        p = jnp.dot(h, w_ref[:, cols], preferred_element_type=F32)
        if j == 1:
            p = p * (M_DK ** -0.5)
        o_ref[:, cols] = p.astype(BF16)
        if j == 2:
            v_t = p.T.astype(BF16)
            for s in range(nslab):
                vt_ref[s] = v_t[:, s * L:(s + 1) * L]


def _inproj_m(x, mod, w, wg, bg):
    tm = TM_PROJ
    nslab = tm // L_CHUNK
    return pl.pallas_call(
        _inproj_m_kernel,
        out_shape=[
            jax.ShapeDtypeStruct((NT, 4 * M_INNER), BF16),
            jax.ShapeDtypeStruct((NT // L_CHUNK, GATE_ROWS, L_CHUNK), F32),
            jax.ShapeDtypeStruct((NT // L_CHUNK, M_INNER, L_CHUNK), BF16),
        ],
        grid=(NT // tm,),
        in_specs=[
            pl.BlockSpec((tm, D_MODEL), lambda i: (i, 0)),
            pl.BlockSpec((None, 6, D_MODEL), lambda i: (_cond_index(i, tm), 0, 0)),
            pl.BlockSpec((D_MODEL, 4 * M_INNER), lambda i: (0, 0)),
            pl.BlockSpec((D_MODEL, LANES), lambda i: (0, 0)),
            pl.BlockSpec((1, LANES), lambda i: (0, 0)),
        ],
        out_specs=[
            pl.BlockSpec((tm, 4 * M_INNER), lambda i: (i, 0)),
            pl.BlockSpec((nslab, GATE_ROWS, L_CHUNK), lambda i: (i, 0, 0)),
            pl.BlockSpec((nslab, M_INNER, L_CHUNK), lambda i: (i, 0, 0)),
        ],
        compiler_params=_params(("arbitrary",)),
        name="inproj_m",
    )(x, mod, w, wg, bg)


def _log_sigmoid(x):
    return jnp.minimum(x, 0.0) - jnp.log1p(jnp.exp(-jnp.abs(x)))


def _split3(x):
    x1 = x.astype(BF16)
    r1 = x - x1.astype(F32)
    x2 = r1.astype(BF16)
    x3 = (r1 - x2.astype(F32)).astype(BF16)
    return x1, x2, x3


def _mlstm_prefix(gt):
    L = gt.shape[1]
    parts = jnp.concatenate([t.astype(F32) for t in _split3(gt)]
                            + [jnp.zeros((GATE_ROWS, L), F32)], axis=0).astype(BF16)
    upper = jnp.where(lax.broadcasted_iota(jnp.int32, (L, L), 0)
                      <= lax.broadcasted_iota(jnp.int32, (L, L), 1), 1.0, 0.0).astype(BF16)
    r = jnp.dot(parts, upper, preferred_element_type=F32)
    return r[0:GATE_ROWS] + r[GATE_ROWS:2 * GATE_ROWS] + r[2 * GATE_ROWS:3 * GATE_ROWS]


def _rows3(x, n):
    g = x.shape[0]
    return jnp.concatenate([t.astype(F32) for t in _split3(x)]
                           + [jnp.zeros((g, ONES_ROWS - 3, n), F32)], axis=1).astype(BF16)


def _sum3(x, r0):
    return x[:, r0:r0 + 1] + x[:, r0 + 1:r0 + 2] + x[:, r0 + 2:r0 + 3]


def _mlstm_group(rev, q, k, vt, gt, pre, state):
    G, L = q.shape[0], q.shape[1]
    d = 1 if rev else 0
    tpos = lax.broadcasted_iota(jnp.int32, (L, L), 0)
    spos = lax.broadcasted_iota(jnp.int32, (L, L), 1)
    keep = (tpos >= spos) if rev else (tpos <= spos)
    if rev:
        tot = pre[:, 3:4, L - 1:L]
        a_row = tot - pre[:, 3:4, :] + gt[:, 3:4, :]
    else:
        a_row = pre[:, 2:3, :]
        tot = a_row[:, :, L - 1:L]
    b_row = gt[:, d:d + 1, :] - a_row
    m = jnp.zeros((G, 1, 1), F32) if state is None else state[0]

    gp = -(-G // 8) * 8
    rows = [b_row[p] for p in range(G)] + [jnp.zeros((gp - G, L), F32)] * (1 if gp > G else 0)
    bt = jnp.concatenate(rows, axis=0).T
    bm = jnp.stack([jnp.where(keep, bt[:, p:p + 1], -jnp.inf) for p in range(G)])
    mm = jnp.maximum(m, jnp.max(bm, axis=1, keepdims=True))
    w = jnp.exp(bm - mm)
    s = jnp.einsum('gsd,gtd->gst', k, q, preferred_element_type=F32)
    qk = s * w
    num = jnp.einsum('gvs,gst->gvt', vt, qk.astype(BF16), preferred_element_type=F32)
    den = jnp.sum(qk, axis=1, keepdims=True)
    if state is not None:
        _, ct, n = state
        cn = jnp.concatenate([ct.astype(BF16), _rows3(n, n.shape[2])], axis=1)
        inter = jnp.einsum('gcd,gtd->gct', cn, q, preferred_element_type=F32)
        s_inter = jnp.exp(m - mm)
        num = num + s_inter * inter[:, :M_DV]
        den = den + s_inter * _sum3(inter, M_DV)
    h = num * (1.0 / jnp.maximum(jnp.abs(den), jnp.exp(-(a_row + mm))))

    g_row = tot + b_row
    m_new = jnp.maximum(tot + m, jnp.max(g_row, axis=2, keepdims=True))
    ws = jnp.exp(g_row - m_new)
    lhs = jnp.concatenate([(vt.astype(F32) * ws).astype(BF16), _rows3(ws, L)], axis=1)
    upd = jnp.einsum('gcs,gsd->gcd', lhs, k, preferred_element_type=F32)
    ct_new = upd[:, :M_DV]
    n_new = _sum3(upd, M_DV)
    if state is not None:
        decay = jnp.exp(tot + m - m_new)
        ct_new = decay * ct + ct_new
        n_new = decay * n + n_new
    return h, (m_new, ct_new, n_new)


def _heads(x):
    w = x.shape[1] // M_HEADS
    return jnp.stack([x[:, h * w:(h + 1) * w] for h in range(M_HEADS)])


def _mlstm_finish(ht, o, gain):
    ms = jnp.mean(ht * ht, axis=1, keepdims=True)
    hn = ht * lax.rsqrt(ms + EPS)
    hn = jnp.concatenate([hn[h].T for h in range(M_HEADS)], axis=1)
    return (_sigmoid(o.astype(F32)) * hn * gain).astype(BF16)


def _mlstm_sample_kernel(qf_ref, kf_ref, vtf_ref, gtf_ref, of_ref, qb_ref, kb_ref, vtb_ref, gtb_ref,
                         ob_ref, gain_ref, c0_ref, n0_ref, m0_ref, hi_ref, lo_ref,
                         hs_ref, ct_scr, n_scr, m_scr, *, nc):
    j = pl.program_id(1)
    jb = nc - 1 - j
    H = M_HEADS
    L = L_CHUNK

    @pl.when(j == 0)
    def _():
        for d in range(2):
            for h in range(H):
                ct_scr[d * H + h] = c0_ref[d, h].T
                n_scr[d * H + h] = n0_ref[d, h]
                m_scr[d * H + h] = jnp.broadcast_to(m0_ref[d, h], (1, LANES))

    def run(rev, q_ref, k_ref, vt_ref, gt_ref, lo):
        gt = gt_ref[...]
        pre = _mlstm_prefix(gt).reshape(H, GATE_STRIDE, L)
        state = (m_scr[lo:lo + H][:, :, 0:1], ct_scr[lo:lo + H], n_scr[lo:lo + H])
        h, (m_new, ct_new, n_new) = _mlstm_group(
            rev, _heads(q_ref[...]), _heads(k_ref[...]), vt_ref[...].reshape(H, M_DV, L),
            gt.reshape(H, GATE_STRIDE, L), pre, state)
        ct_scr[lo:lo + H] = ct_new
        n_scr[lo:lo + H] = n_new
        m_scr[lo:lo + H] = jnp.broadcast_to(m_new, (H, 1, LANES))
        return h

    h_f = run(False, qf_ref, kf_ref, vtf_ref, gtf_ref, 0)
    h_b = run(True, qb_ref, kb_ref, vtb_ref, gtb_ref, H)

    @pl.when(j < nc // 2)
    def _():
        hs_ref[j] = h_f
        hs_ref[jb] = h_b

    @pl.when(j >= nc // 2)
    def _():
        gain = gain_ref[...]
        hi_ref[...] = _mlstm_finish(hs_ref[j] + h_f, of_ref[...], gain)
        lo_ref[...] = _mlstm_finish(hs_ref[jb] + h_b, ob_ref[...], gain)


def _mlstm_sample(qkvo, vt, gates_t, gain, init):
    nc = DEC_SEQ // L_CHUNK
    half = nc // 2
    c0 = NP // L_CHUNK
    L = L_CHUNK
    fwd = lambda b, j: c0 + b * nc + j
    bwd = lambda b, j: c0 + b * nc + nc - 1 - j

    def specs(f):
        return [
            pl.BlockSpec((L, M_INNER), lambda b, j: (f(b, j), 0)),
            pl.BlockSpec((L, M_INNER), lambda b, j: (f(b, j), 1)),
            pl.BlockSpec((None, M_INNER, L), lambda b, j: (f(b, j), 0, 0)),
            pl.BlockSpec((None, GATE_ROWS, L), lambda b, j: (f(b, j), 0, 0)),
            pl.BlockSpec((L, M_INNER), lambda b, j: (f(b, j), 3)),
        ]

    state_specs = [
        pl.BlockSpec((None, 2, M_HEADS, M_DK, M_DV), lambda b, j: (b, 0, 0, 0, 0)),
        pl.BlockSpec((None, 2, M_HEADS, 1, M_DK), lambda b, j: (b, 0, 0, 0, 0)),
        pl.BlockSpec((None, 2, M_HEADS, 1, 1), lambda b, j: (b, 0, 0, 0, 0)),
    ]
    out_rows = DEC_BATCH * DEC_SEQ // 2
    return pl.pallas_call(
        functools.partial(_mlstm_sample_kernel, nc=nc),
        out_shape=[jax.ShapeDtypeStruct((out_rows, M_INNER), BF16)] * 2,
        grid=(DEC_BATCH, nc),
        in_specs=specs(fwd) + specs(bwd) + [pl.BlockSpec((1, M_INNER), lambda b, j: (0, 0))] + state_specs,
        out_specs=[
            pl.BlockSpec((L, M_INNER), lambda b, j: (b * half + jnp.maximum(j, half) - half, 0)),
            pl.BlockSpec((L, M_INNER), lambda b, j: (b * half + jnp.minimum(nc - 1 - j, half - 1), 0)),
        ],
        scratch_shapes=[
            pltpu.VMEM((nc, M_HEADS, M_DV, L), F32),
            pltpu.VMEM((2 * M_HEADS, M_DV, M_DK), F32),
            pltpu.VMEM((2 * M_HEADS, 1, M_DK), F32),
            pltpu.VMEM((2 * M_HEADS, 1, LANES), F32),
        ],
        compiler_params=_params(("arbitrary", "arbitrary")),
        name="mlstm",
    )(qkvo, qkvo, vt, gates_t, qkvo, qkvo, qkvo, vt, gates_t, qkvo, gain, *init)


def _mlstm_prompt_kernel(q_ref, k_ref, vt_ref, gt_ref, o_ref, gain_ref, out_ref, co_ref, no_ref, mo_ref,
                         *, nseq):
    H = M_HEADS
    L = L_CHUNK
    G = nseq * H
    q = jnp.concatenate([_heads(q_ref[b * L:(b + 1) * L, :]) for b in range(nseq)], axis=0)
    k = jnp.concatenate([_heads(k_ref[b * L:(b + 1) * L, :]) for b in range(nseq)], axis=0)
    vt = vt_ref[...].reshape(G, M_DV, L)
    gt3 = jnp.concatenate([gt_ref[b] for b in range(nseq)], axis=0).reshape(G, GATE_STRIDE, L)
    pre3 = jnp.concatenate([_mlstm_prefix(gt_ref[b]) for b in range(nseq)],
                           axis=0).reshape(G, GATE_STRIDE, L)
    gain = gain_ref[...]
    hs = None
    for d in range(2):
        h, (m_new, ct_new, n_new) = _mlstm_group(d == 1, q, k, vt, gt3, pre3, None)
        hs = h if hs is None else hs + h
        for b in range(nseq):
            for hh in range(H):
                co_ref[b, d, hh] = ct_new[b * H + hh].T
                no_ref[b, d, hh] = n_new[b * H + hh]
                mo_ref[b, d, hh] = m_new[b * H + hh]
    for b in range(nseq):
        out_ref[b * L:(b + 1) * L, :] = _mlstm_finish(hs[b * H:(b + 1) * H], o_ref[b * L:(b + 1) * L, :], gain)


def _mlstm_prompt(qkvo, vt, gates_t, gain):
    nseq = MLSTM_SEQS
    L = L_CHUNK
    rows = nseq * L
    state_specs = [
        pl.BlockSpec((nseq, 2, M_HEADS, M_DK, M_DV), lambda i: (i, 0, 0, 0, 0)),
        pl.BlockSpec((nseq, 2, M_HEADS, 1, M_DK), lambda i: (i, 0, 0, 0, 0)),
        pl.BlockSpec((nseq, 2, M_HEADS, 1, 1), lambda i: (i, 0, 0, 0, 0)),
    ]
    return pl.pallas_call(
        functools.partial(_mlstm_prompt_kernel, nseq=nseq),
        out_shape=[
            jax.ShapeDtypeStruct((NP, M_INNER), BF16),
            jax.ShapeDtypeStruct((BATCH, 2, M_HEADS, M_DK, M_DV), F32),
            jax.ShapeDtypeStruct((BATCH, 2, M_HEADS, 1, M_DK), F32),
            jax.ShapeDtypeStruct((BATCH, 2, M_HEADS, 1, 1), F32),
        ],
        grid=(BATCH // nseq,),
        in_specs=[
            pl.BlockSpec((rows, M_INNER), lambda i: (i, 0)),
            pl.BlockSpec((rows, M_INNER), lambda i: (i, 1)),
            pl.BlockSpec((nseq, M_INNER, L), lambda i: (i, 0, 0)),
            pl.BlockSpec((nseq, GATE_ROWS, L), lambda i: (i, 0, 0)),
            pl.BlockSpec((rows, M_INNER), lambda i: (i, 3)),
            pl.BlockSpec((1, M_INNER), lambda i: (0, 0)),
        ],
        out_specs=[pl.BlockSpec((rows, M_INNER), lambda i: (i, 0))] + state_specs,
        compiler_params=_params(("arbitrary",)),
        name="mlstm_state",
    )(qkvo, qkvo, vt, gates_t, qkvo, gain)


@jax.jit
def kernel(x_prompt, x_sample, cache_k, cache_v, state_c, state_n, state_m, c, c_ctx, w_ada, b_ada,
           ln_g, ln_b, w_ff1, w_ff2, w_in_a, q_gain, k_gain, conv_w, conv_b, conv_ln_g, conv_ln_b,
           w_out_a, w_in_m, b_gate_m, mh_gain, w_out_m):
    xp = x_prompt.reshape(NP, D_MODEL)
    xs = x_sample.reshape(NS, D_MODEL)

    cond = jnp.concatenate(
        [c_ctx[None, :], c, jnp.zeros((N_COND - 1 - DEC_BATCH, D_MODEL), F32)], axis=0)
    mods = _adaln(cond, w_ada, b_ada)

    ln = lambda l, s: (ln_g[l, s].reshape(1, D_MODEL), ln_b[l, s].reshape(1, D_MODEL))

    q, k, v, kf, vf, u = _inproj_a(xp, xs, mods[0], w_in_a, q_gain[0], k_gain[0])
    att_p = _attention_prompt(q, k, v)
    ctx_k = cache_k[:, 0].reshape(DEC_BATCH, PAST_LEN, KV_W)
    ctx_v = cache_v[:, 0].reshape(DEC_BATCH, PAST_LEN, KV_W)
    att_s = _attention_sample(q, k, v, ctx_k, ctx_v)
    tail_src = (att_p, att_s, u, xp, xs, conv_w[0], conv_b[0], conv_ln_g[0], conv_ln_b[0])
    x2 = _block_tail("a", tail_src, 0, NT, mods[0], w_out_a, w_ff1, w_ff2, 0, ln(0, 0), ln(0, 1))

    wm = w_in_m[0]
    src = jnp.array([4 * M_INNER + g * M_HEADS + h
                     for h in range(M_HEADS) for g in (0, 2, 1, 3)], jnp.int32)
    dst = jnp.array([h * GATE_STRIDE + t for h in range(M_HEADS) for t in range(4)], jnp.int32)
    wg = jnp.zeros((D_MODEL, LANES), F32).at[:, dst].set(wm[:, src]).astype(BF16)
    bg = jnp.zeros((1, LANES), F32).at[0, dst].set(b_gate_m[0][src - 4 * M_INNER])
    qkvo, gates_t, vt_m = _inproj_m(x2, mods[1], wm[:, :4 * M_INNER].astype(BF16), wg, bg)
    gain = mh_gain[0].reshape(1, M_INNER)
    hg_p, st_c, st_n, st_m = _mlstm_prompt(qkvo, vt_m, gates_t, gain)
    init = (state_c[:, 0], state_n[:, 0].reshape(DEC_BATCH, 2, M_HEADS, 1, M_DK),
            state_m[:, 0].reshape(DEC_BATCH, 2, M_HEADS, 1, 1))
    hg_hi, hg_lo = _mlstm_sample(qkvo, vt_m, gates_t, gain, init)
    tail_src = (hg_p, hg_lo, hg_hi, x2)
    y_p = _block_tail("m", tail_src, 0, NP, mods[1], w_out_m, w_ff1, w_ff2, 1, ln(1, 0), ln(1, 1))
    y_s = _block_tail("m", tail_src, NP, NS, mods[1], w_out_m, w_ff1, w_ff2, 1, ln(1, 0), ln(1, 1))

    new_k = jnp.transpose(kf, (0, 3, 1, 2))[:, None]
    new_v = jnp.transpose(vf, (0, 3, 1, 2))[:, None]
    return (y_p.reshape(BATCH, SEQ, D_MODEL), y_s.reshape(DEC_BATCH, DEC_SEQ, D_MODEL),
            new_k, new_v,
            st_c.reshape(BATCH, 1, 2, M_HEADS, M_DK, M_DV),
            st_n.reshape(BATCH, 1, 2, M_HEADS, M_DK),
            st_m.reshape(BATCH, 1, 2, M_HEADS))
```

```python
import functools

import jax
import jax.numpy as jnp
from jax import lax
from jax.experimental import pallas as pl
from jax.experimental.pallas import tpu as pltpu

F32 = jnp.float32
BF16 = jnp.bfloat16

D_MODEL = 1024
BATCH = 32
SEQ = 256
DEPTH = 2
DEC_BATCH = 4
DEC_SEQ = 4096
PAST_LEN = 512
GRID_W = 64
ATT_HEADS = 8
ATT_KV_HEADS = 2
HEAD_DIM = 64
ATT_GROUP = ATT_HEADS // ATT_KV_HEADS
ATT_W = ATT_HEADS * HEAD_DIM
KV_W = ATT_KV_HEADS * HEAD_DIM
ROPE_AXIS_DIM = HEAD_DIM // 2
ROPE_THETA = 10000.0
CONV_CH = D_MODEL // 2
CONV_K = 31
EVEN_IN = ATT_W + 2 * KV_W + 2 * CONV_CH
M_HEADS = 4
M_INNER = D_MODEL
M_DK = M_INNER // M_HEADS
M_DV = M_INNER // M_HEADS
D_FF = 4 * D_MODEL
ALPHA = (2 * DEPTH) ** 0.25
EPS = 1e-6
LOG2E = 1.4426950408889634

NP = BATCH * SEQ
NS = DEC_BATCH * DEC_SEQ
NT = NP + NS
N_COND = 8

LANES = 128
VMEM_LIMIT = 56 * 1024 * 1024

TM_PROJ = 512
TF_FFN = 1024
TQ_ATT = 512
TK_ATT = 512
R_CONV = 256
HALO = 16
L_CHUNK = 256
GATE_STRIDE = 8
ONES_ROWS = 16
GATE_ROWS = 32
MLSTM_SEQS = 4
TAIL_CAST_STEPS = 16
INPROJ_SUBTILES = 4
ATT_UNROLL = 9
ATT_PROMPT_SEQS = 1


def _cond_index(i_global, tm):
    npt = NP // tm
    tps = DEC_SEQ // tm
    return jnp.where(i_global < npt, 0, 1 + (i_global - npt) // tps)


def _layernorm(r, g, b):
    mu = jnp.mean(r, axis=-1, keepdims=True)
    d = r - mu
    var = jnp.mean(d * d, axis=-1, keepdims=True)
    return d * lax.rsqrt(var + EPS) * g + b


def _sigmoid(x):
    return 1.0 / (1.0 + jnp.exp(-x))


def _params(sem, vmem=VMEM_LIMIT, flags=None):
    return pltpu.CompilerParams(dimension_semantics=sem, vmem_limit_bytes=vmem, flags=flags)


def _adaln_kernel(cond_ref, w_ref, b_ref, o_ref):
    c = cond_ref[...]
    s = (c * _sigmoid(c)).astype(BF16)
    o_ref[...] = jnp.dot(s, w_ref[...].astype(BF16), preferred_element_type=F32) + b_ref[...]


def _adaln(cond, w_ada, b_ada):
    tn = 1536
    n = 6 * D_MODEL
    out = pl.pallas_call(
        _adaln_kernel,
        out_shape=jax.ShapeDtypeStruct((DEPTH, N_COND, n), F32),
        grid=(DEPTH, n // tn),
        in_specs=[
            pl.BlockSpec((N_COND, D_MODEL), lambda l, j: (0, 0)),
            pl.BlockSpec((None, D_MODEL, tn), lambda l, j: (l, 0, j)),
            pl.BlockSpec((None, 1, tn), lambda l, j: (l, 0, j)),
        ],
        out_specs=pl.BlockSpec((None, N_COND, tn), lambda l, j: (l, 0, j)),
        compiler_params=_params(("arbitrary", "arbitrary")),
        name="adaln",
    )(cond, w_ada, b_ada.reshape(DEPTH, 1, n))
    return out.reshape(DEPTH, N_COND, 6, D_MODEL)


def _cast_once(w_ref, w_scr):
    @pl.when(pl.program_id(0) == 0)
    def _():
        w_scr[...] = w_ref[...].astype(BF16)


def _inproj_a_kernel(xp_ref, xs_ref, mod_ref, w_ref, qg_ref, kg_ref, cos_ref, sin_ref,
                     q_ref, k_ref, v_ref, kf_ref, vf_ref, u_ref, w_scr, *, npt):
    i = pl.program_id(0)
    _cast_once(w_ref, w_scr)
    mod = mod_ref[...]
    tm = u_ref.shape[0]
    sub = tm // INPROJ_SUBTILES

    ri = lax.broadcasted_iota(jnp.int32, (LANES, LANES), 0) // HEAD_DIM
    ci = lax.broadcasted_iota(jnp.int32, (LANES, LANES), 1) // HEAD_DIM
    seg = jnp.where(ri == ci, 1.0, 0.0).astype(BF16)
    lane = lax.broadcasted_iota(jnp.int32, (sub, LANES), 1)
    even = (lane % 2) == 0
    qg = qg_ref[...]
    kg = kg_ref[...]

    def norm(xc, gain):
        ss = jnp.dot((xc * xc).astype(BF16), seg, preferred_element_type=F32)
        return xc * lax.rsqrt(ss * (1.0 / HEAD_DIM) + EPS) * gain

    cache_rows = []
    for r in range(INPROJ_SUBTILES):
        rows = slice(r * sub, (r + 1) * sub)
        x = jnp.where(i < npt, xp_ref[rows, :], xs_ref[rows, :])
        h = (x * (1.0 + mod[1:2]) + mod[0:1]).astype(BF16)
        proj = jnp.dot(h, w_scr[...], preferred_element_type=F32)
        cos = cos_ref[rows, :]
        sin = sin_ref[rows, :]

        def rope(xn):
            partner = jnp.where(even, pltpu.roll(xn, LANES - 1, 1), pltpu.roll(xn, 1, 1))
            return xn * cos + partner * sin

        for c in range(ATT_W // LANES):
            qr = rope(norm(proj[:, c * LANES:(c + 1) * LANES], qg))
            qs = (qr * (HEAD_DIM ** -0.5 * LOG2E)).astype(BF16)
            q_ref[2 * c, rows, :] = qs[:, :HEAD_DIM]
            q_ref[2 * c + 1, rows, :] = qs[:, HEAD_DIM:]

        kn = norm(proj[:, ATT_W:ATT_W + KV_W], kg)
        kr = rope(kn).astype(BF16)
        k_ref[0, rows, :] = kr[:, :HEAD_DIM]
        k_ref[1, rows, :] = kr[:, HEAD_DIM:]

        v = proj[:, ATT_W + KV_W:ATT_W + 2 * KV_W]
        cache_rows.append((kn, v))
        vb = v.astype(BF16)
        v_ref[0, rows, :] = vb[:, :HEAD_DIM]
        v_ref[1, rows, :] = vb[:, HEAD_DIM:]

        off = ATT_W + 2 * KV_W
        a = proj[:, off:off + CONV_CH]
        gt = proj[:, off + CONV_CH:off + 2 * CONV_CH]
        u_ref[rows, :] = a * _sigmoid(gt)

    @pl.when(i < npt)
    def _():
        for r, (kn, v) in enumerate(cache_rows):
            j, t0 = (r * sub) // SEQ, (r * sub) % SEQ
            kf_ref[j, :, :, t0:t0 + sub] = kn.T.reshape(ATT_KV_HEADS, HEAD_DIM, sub)
            vf_ref[j, :, :, t0:t0 + sub] = v.T.reshape(ATT_KV_HEADS, HEAD_DIM, sub)


def _rope_tables(tm):
    t = jnp.arange(DEC_SEQ)
    row = (t // GRID_W).astype(F32)
    col = (t % GRID_W).astype(F32)
    freqs = ROPE_THETA ** (-jnp.arange(0, ROPE_AXIS_DIM, 2, dtype=F32) / ROPE_AXIS_DIM)
    ang = jnp.concatenate([row[:, None] * freqs, col[:, None] * freqs], axis=-1)
    pair = (jnp.arange(LANES) % HEAD_DIM) // 2
    sign = jnp.where(jnp.arange(LANES) % 2 == 0, -1.0, 1.0).astype(F32)
    cos = jnp.cos(ang)[:, pair]
    sin = jnp.sin(ang)[:, pair] * sign
    cos = jnp.concatenate([jnp.ones((tm, LANES), F32), cos], axis=0)
    sin = jnp.concatenate([jnp.zeros((tm, LANES), F32), sin], axis=0)
    return cos, sin


def _inproj_a(xp, xs, mod, w, q_gain, k_gain):
    tm = TM_PROJ
    npt = NP // tm
    tps = DEC_SEQ // tm
    nt = NT // tm
    cos, sin = _rope_tables(tm)
    qg = jnp.tile(q_gain, LANES // HEAD_DIM).reshape(1, LANES)
    kg = jnp.tile(k_gain, LANES // HEAD_DIM).reshape(1, LANES)

    def rope_idx(i):
        return (jnp.where(i < npt, 0, 1 + (i - npt) % tps), 0)

    return pl.pallas_call(
        functools.partial(_inproj_a_kernel, npt=npt),
        out_shape=[
            jax.ShapeDtypeStruct((ATT_HEADS, NT, HEAD_DIM), BF16),
            jax.ShapeDtypeStruct((ATT_KV_HEADS, NT, HEAD_DIM), BF16),
            jax.ShapeDtypeStruct((ATT_KV_HEADS, NT, HEAD_DIM), BF16),
            jax.ShapeDtypeStruct((BATCH, ATT_KV_HEADS, HEAD_DIM, SEQ), F32),
            jax.ShapeDtypeStruct((BATCH, ATT_KV_HEADS, HEAD_DIM, SEQ), F32),
            jax.ShapeDtypeStruct((NT, CONV_CH), F32),
        ],
        grid=(nt,),
        in_specs=[
            pl.BlockSpec((tm, D_MODEL), lambda i: (jnp.minimum(i, npt - 1), 0)),
            pl.BlockSpec((tm, D_MODEL), lambda i: (jnp.maximum(i - npt, 0), 0)),
            pl.BlockSpec((None, 6, D_MODEL), lambda i: (_cond_index(i, tm), 0, 0)),
            pl.BlockSpec((None, D_MODEL, EVEN_IN), lambda i: (0, 0, 0)),
            pl.BlockSpec((1, LANES), lambda i: (0, 0)),
            pl.BlockSpec((1, LANES), lambda i: (0, 0)),
            pl.BlockSpec((tm, LANES), rope_idx),
            pl.BlockSpec((tm, LANES), rope_idx),
        ],
        out_specs=[
            pl.BlockSpec((ATT_HEADS, tm, HEAD_DIM), lambda i: (0, i, 0)),
            pl.BlockSpec((ATT_KV_HEADS, tm, HEAD_DIM), lambda i: (0, i, 0)),
            pl.BlockSpec((ATT_KV_HEADS, tm, HEAD_DIM), lambda i: (0, i, 0)),
            pl.BlockSpec((tm // SEQ, ATT_KV_HEADS, HEAD_DIM, SEQ), lambda i: (jnp.minimum(i, npt - 1), 0, 0, 0)),
            pl.BlockSpec((tm // SEQ, ATT_KV_HEADS, HEAD_DIM, SEQ), lambda i: (jnp.minimum(i, npt - 1), 0, 0, 0)),
            pl.BlockSpec((tm, CONV_CH), lambda i: (i, 0)),
        ],
        scratch_shapes=[pltpu.VMEM((D_MODEL, EVEN_IN), BF16)],
        compiler_params=_params(("arbitrary",)),
        name="inproj_a",
    )(xp, xs, mod, w, qg, kg, cos, sin)


def _attn_chunk(qs, kc, vc, m, acc):
    s = lax.dot_general(qs, kc, (((1,), (1,)), ((), ())), preferred_element_type=F32)
    fold = s[:, :LANES]
    for c in range(1, s.shape[1] // LANES):
        fold = jnp.maximum(fold, s[:, c * LANES:(c + 1) * LANES])
    m_new = jnp.maximum(m, jnp.max(fold, axis=1, keepdims=True))
    alpha = jnp.exp2(m - m_new)
    p = jnp.exp2(s - m_new).astype(BF16)
    va = jnp.concatenate([vc, jnp.ones((vc.shape[0], LANES - HEAD_DIM), BF16)], axis=1)
    acc = alpha * acc + jnp.dot(p, va, preferred_element_type=F32)
    return m_new, acc


def _attn_init(nq):
    return jnp.full((nq, 1), -1e30, F32), jnp.zeros((nq, LANES), F32)


def _attn_finish(acc, o_ref):
    tq = o_ref.shape[0]
    o = acc[:, :HEAD_DIM] * (1.0 / acc[:, HEAD_DIM:HEAD_DIM + 1])
    o_ref[...] = jnp.concatenate([o[h * tq:(h + 1) * tq] for h in range(ATT_GROUP)],
                                 axis=1).astype(BF16)


def _stack_heads(q_ref):
    return q_ref[...].reshape(ATT_GROUP * q_ref.shape[1], HEAD_DIM)


def _attn_prompt_kernel(q_ref, k_ref, v_ref, o_ref):
    for b in range(q_ref.shape[1] // SEQ):
        rows = slice(b * SEQ, (b + 1) * SEQ)
        qs = q_ref[:, rows, :].reshape(ATT_GROUP * SEQ, HEAD_DIM)
        _, acc = _attn_chunk(qs, k_ref[rows, :], v_ref[rows, :], *_attn_init(qs.shape[0]))
        _attn_finish(acc, o_ref.at[rows, :])


def _attn_sample_kernel(q_ref, k_ref, v_ref, ck_ref, cv_ref, o_ref, k_scr, v_scr, *, nchunks):
    @pl.when(pl.program_id(2) == 0)
    def _():
        own = k_ref.shape[0]
        k_scr[0:own, :] = k_ref[...]
        v_scr[0:own, :] = v_ref[...]
        first = pl.program_id(1) == 0
        ck = ck_ref[...]
        cv = cv_ref[...]
        k_scr[own:, :] = jnp.where(first, ck[:, :HEAD_DIM], ck[:, HEAD_DIM:]).astype(BF16)
        v_scr[own:, :] = jnp.where(first, cv[:, :HEAD_DIM], cv[:, HEAD_DIM:]).astype(BF16)

    qs = _stack_heads(q_ref)

    def body(j, carry):
        rows = pl.ds(pl.multiple_of(j * TK_ATT, TK_ATT), TK_ATT)
        return _attn_chunk(qs, k_scr[rows, :], v_scr[rows, :], *carry)

    _, acc = lax.fori_loop(0, nchunks, body, _attn_init(qs.shape[0]), unroll=ATT_UNROLL)
    _attn_finish(acc, o_ref)


def _attention_prompt(q, k, v):
    return pl.pallas_call(
        _attn_prompt_kernel,
        out_shape=jax.ShapeDtypeStruct((NP, ATT_W), BF16),
        grid=(BATCH // ATT_PROMPT_SEQS, ATT_KV_HEADS),
        in_specs=[
            pl.BlockSpec((ATT_GROUP, ATT_PROMPT_SEQS * SEQ, HEAD_DIM), lambda b, g: (g, b, 0)),
            pl.BlockSpec((None, ATT_PROMPT_SEQS * SEQ, HEAD_DIM), lambda b, g: (g, b, 0)),
            pl.BlockSpec((None, ATT_PROMPT_SEQS * SEQ, HEAD_DIM), lambda b, g: (g, b, 0)),
        ],
        out_specs=pl.BlockSpec((ATT_PROMPT_SEQS * SEQ, ATT_GROUP * HEAD_DIM), lambda b, g: (b, g)),
        compiler_params=_params(("arbitrary", "arbitrary")),
        name="attn_prompt",
    )(q, k, v)


def _attention_sample(q, k, v, ctx_k, ctx_v):
    tq = TQ_ATT
    nkeys = DEC_SEQ + PAST_LEN
    nchunks = nkeys // TK_ATT
    q_off = NP // tq
    kv_off = NP // DEC_SEQ
    return pl.pallas_call(
        functools.partial(_attn_sample_kernel, nchunks=nchunks),
        out_shape=jax.ShapeDtypeStruct((NS, ATT_W), BF16),
        grid=(DEC_BATCH, ATT_KV_HEADS, DEC_SEQ // tq),
        in_specs=[
            pl.BlockSpec((ATT_GROUP, tq, HEAD_DIM),
                         lambda b, g, i: (g, q_off + b * (DEC_SEQ // tq) + i, 0)),
            pl.BlockSpec((None, DEC_SEQ, HEAD_DIM), lambda b, g, i: (g, kv_off + b, 0)),
            pl.BlockSpec((None, DEC_SEQ, HEAD_DIM), lambda b, g, i: (g, kv_off + b, 0)),
            pl.BlockSpec((None, PAST_LEN, KV_W), lambda b, g, i: (b, 0, 0)),
            pl.BlockSpec((None, PAST_LEN, KV_W), lambda b, g, i: (b, 0, 0)),
        ],
        out_specs=pl.BlockSpec((tq, ATT_GROUP * HEAD_DIM),
                               lambda b, g, i: (b * (DEC_SEQ // tq) + i, g)),
        scratch_shapes=[pltpu.VMEM((nkeys, HEAD_DIM), BF16), pltpu.VMEM((nkeys, HEAD_DIM), BF16)],
        compiler_params=_params(("arbitrary", "arbitrary", "arbitrary")),
        name="attn_sample",
    )(q, k, v, ctx_k, ctx_v)


def _conv_window(left, cur, right, w_ref, b_ref, g_ref, bb_ref, win_ref, y_ref, out_ref):
    r = cur.shape[0]
    win_ref[0:HALO, :] = left
    win_ref[HALO:HALO + r, :] = cur
    win_ref[HALO + r:2 * HALO + r, :] = right
    base = HALO - CONV_K // 2
    sub = 8
    for c in range(CONV_CH // LANES):
        cs = slice(c * LANES, (c + 1) * LANES)
        acc = None
        for res in range(sub):
            p = None
            for k in range(CONV_K):
                if (base + k) % sub != res:
                    continue
                a = (base + k) - res
                term = win_ref[a:a + r + sub, cs] * w_ref[k:k + 1, cs]
                p = term if p is None else p + term
            if p is None:
                continue
            p = p[res:res + r]
            acc = p if acc is None else acc + p
        y_ref[:, cs] = acc + b_ref[:, cs]
    y = _layernorm(y_ref[...], g_ref[...], bb_ref[...])
    out_ref[...] = (y * _sigmoid(y)).astype(BF16)


def _tail_kernel(*refs, mode, ncast, npt, tps, off):
    nsrc = 11 if mode == "a" else 4
    src = refs[:nsrc]
    (mod_ref, wo_ref, w1_ref, w2_ref, g0_ref, b0_ref, g1_ref, b1_ref, o_ref,
     wo_scr, w1_scr, w2_scr) = refs[nsrc:nsrc + 12]
    s = pl.program_id(0)
    per = ncast // w1_scr.shape[0]
    wcol = w1_ref.shape[1]

    @pl.when(s < ncast)
    def _():
        wo_scr[s] = wo_ref[...].astype(BF16)
        w2_scr[s] = w2_ref[...].astype(BF16)

    for q in range(per):
        @pl.when(jnp.logical_and(s < ncast, s % per == q))
        def _():
            w1_scr[s // per, :, q * wcol:(q + 1) * wcol] = w1_ref[...].astype(BF16)

    def conv_half(r, tile):
        ul_ref, u_ref, ur_ref, cw_ref, cb_ref, cg_ref, cbb_ref = src[2:5] + src[7:11]
        win_scr, y_scr, uc_scr = refs[nsrc + 12:]
        ctx = tile < npt
        t = (tile - npt) % tps
        half = u_ref.shape[0] // 2
        starts = jnp.logical_or(ctx, jnp.logical_and(t == 0, r == 0))
        ends = jnp.logical_or(ctx, jnp.logical_and(t == tps - 1, r == 1))
        left = ul_ref[...] if r == 0 else u_ref[half - HALO:half, :]
        right = u_ref[half:half + HALO, :] if r == 0 else ur_ref[...]
        _conv_window(jnp.where(starts, 0.0, left), u_ref[r * half:(r + 1) * half, :],
                     jnp.where(ends, 0.0, right), cw_ref, cb_ref, cg_ref, cbb_ref,
                     win_scr.at[r], y_scr.at[r], uc_scr.at[r])
        return uc_scr[r]

    @pl.when(s >= ncast)
    def _():
        tile = s - ncast + off
        ctx = tile < npt
        mod = mod_ref[...]
        w_out = wo_scr[...].reshape(D_MODEL, D_MODEL)
        nchunk = w1_scr.shape[0]
        rows_per = w2_scr.shape[0] // nchunk
        half = o_ref.shape[0] // 2
        for r in range(2):
            rows = slice(r * half, (r + 1) * half)
            if mode == "a":
                ap_ref, as_ref, xp_ref, xs_ref = src[0], src[1], src[5], src[6]
                att = jnp.where(ctx, ap_ref[rows, :], as_ref[rows, :])
                x = jnp.where(ctx, xp_ref[rows, :], xs_ref[rows, :])
                y = jnp.dot(att, w_out[0:ATT_W, :], preferred_element_type=F32)
                y = y + jnp.dot(conv_half(r, tile), w_out[ATT_W:, :], preferred_element_type=F32)
            else:
                hp_ref, lo_ref, hi_ref, x_ref = src
                upper = ((tile - npt) % tps) >= tps // 2
                hg = jnp.where(ctx, hp_ref[rows, :], jnp.where(upper, hi_ref[rows, :], lo_ref[rows, :]))
                x = x_ref[rows, :]
                y = jnp.dot(hg, w_out, preferred_element_type=F32)
            x1 = _layernorm(ALPHA * x + mod[2:3] * y, g0_ref[...], b0_ref[...])
            h = (x1 * (1.0 + mod[4:5]) + mod[3:4]).astype(BF16)
            acc = None
            for c in range(nchunk):
                a = jnp.maximum(jnp.dot(h, w1_scr[c], preferred_element_type=F32), 0.0)
                w2 = w2_scr[c * rows_per:(c + 1) * rows_per].reshape(TF_FFN, D_MODEL)
                t = jnp.dot((a * a).astype(BF16), w2, preferred_element_type=F32)
                acc = t if acc is None else acc + t
            o_ref[rows, :] = _layernorm(ALPHA * x1 + mod[5:6] * acc, g1_ref[...], b1_ref[...])


def _block_tail(mode, src, row0, nrows, mod, w_out, w1, w2, layer, ln0, ln1):
    tm = TM_PROJ
    ncast = TAIL_CAST_STEPS
    npt = NP // tm
    tps = DEC_SEQ // tm
    hps = tps // 2
    off = row0 // tm
    ntiles = nrows // tm
    tile = lambda s: jnp.maximum(s - ncast, 0) + off
    blk = lambda s: jnp.minimum(s, ncast - 1)
    first = lambda s: (jnp.minimum(tile(s), npt - 1), 0)
    second = lambda s: (jnp.maximum(tile(s) - npt, 0), 0)

    def half_idx(s, upper):
        t0 = jnp.maximum(tile(s) - npt, 0)
        t = t0 % tps
        t = jnp.maximum(t - hps, 0) if upper else jnp.minimum(t, hps - 1)
        return ((t0 // tps) * hps + t, 0)

    scratch = []
    if mode == "a":
        assert tm == 2 * R_CONV
        att_p, att_s, u, xp, xs, conv_w, conv_b, cln_g, cln_b = src
        hb = tm // HALO
        nh = NT // HALO
        cvec = pl.BlockSpec((1, CONV_CH), lambda s: (0, 0))
        src = (att_p, att_s, u, u, u, xp, xs,
               jnp.concatenate([conv_w, jnp.zeros((1, CONV_CH), F32)], axis=0),
               conv_b.reshape(1, CONV_CH), cln_g.reshape(1, CONV_CH), cln_b.reshape(1, CONV_CH))
        src_specs = [
            pl.BlockSpec((tm, ATT_W), first),
            pl.BlockSpec((tm, ATT_W), second),
            pl.BlockSpec((HALO, CONV_CH), lambda s: (jnp.maximum(tile(s) * hb - 1, 0), 0)),
            pl.BlockSpec((tm, CONV_CH), lambda s: (tile(s), 0)),
            pl.BlockSpec((HALO, CONV_CH), lambda s: (jnp.minimum((tile(s) + 1) * hb, nh - 1), 0)),
            pl.BlockSpec((tm, D_MODEL), first),
            pl.BlockSpec((tm, D_MODEL), second),
            pl.BlockSpec((CONV_K + 1, CONV_CH), lambda s: (0, 0)),
            cvec, cvec, cvec,
        ]
        scratch = [pltpu.VMEM((2, R_CONV + 2 * HALO, CONV_CH), F32), pltpu.VMEM((2, R_CONV, CONV_CH), F32),
                   pltpu.VMEM((2, R_CONV, CONV_CH), BF16)]
    else:
        src_specs = [
            pl.BlockSpec((tm, M_INNER), first),
            pl.BlockSpec((tm, M_INNER), lambda s: half_idx(s, False)),
            pl.BlockSpec((tm, M_INNER), lambda s: half_idx(s, True)),
            pl.BlockSpec((tm, D_MODEL), lambda s: (tile(s), 0)),
        ]
    nchunk = D_FF // TF_FFN
    vec = pl.BlockSpec((1, D_MODEL), lambda s: (0, 0))
    return pl.pallas_call(
        functools.partial(_tail_kernel, mode=mode, ncast=ncast, npt=npt, tps=tps, off=off),
        out_shape=jax.ShapeDtypeStruct((nrows, D_MODEL), F32),
        grid=(ncast + ntiles,),
        in_specs=src_specs + [
            pl.BlockSpec((None, 6, D_MODEL), lambda s: (_cond_index(tile(s), tm), 0, 0)),
            pl.BlockSpec((None, D_MODEL // ncast, D_MODEL), lambda s: (0, blk(s), 0)),
            pl.BlockSpec((None, D_MODEL, D_FF // ncast), lambda s: (layer, 0, blk(s))),
            pl.BlockSpec((None, D_FF // ncast, D_MODEL), lambda s: (layer, blk(s), 0)),
            vec, vec, vec, vec,
        ],
        out_specs=pl.BlockSpec((tm, D_MODEL), lambda s: (jnp.maximum(s - ncast, 0), 0)),
        scratch_shapes=[
            pltpu.VMEM((ncast, D_MODEL // ncast, D_MODEL), BF16),
            pltpu.VMEM((nchunk, D_MODEL, TF_FFN), BF16),
            pltpu.VMEM((ncast, D_FF // ncast, D_MODEL), BF16),
        ] + scratch,
        compiler_params=_params(("arbitrary",)),
        name="tail_" + mode,
    )(*src, mod, w_out, w1, w2, *ln0, *ln1)


def _inproj_m_kernel(x_ref, mod_ref, w_ref, wg_ref, bg_ref, o_ref, gt_ref, vt_ref):
    nslab = vt_ref.shape[0]
    L = L_CHUNK
    mod = mod_ref[...]
    h = (x_ref[...] * (1.0 + mod[1:2]) + mod[0:1]).astype(BF16)

    gates = jnp.dot(h, wg_ref[...], preferred_element_type=F32) + bg_ref[...]
    kind = lax.broadcasted_iota(jnp.int32, gates.shape, 1) % GATE_STRIDE
    gates = jnp.where((kind == 2) | (kind == 3), _log_sigmoid(gates), gates)
    gates_t = gates.T
    for s in range(nslab):
        gt_ref[s] = gates_t[0:GATE_ROWS, s * L:(s + 1) * L]

    for j in range(4):
        cols = slice(j * M_INNER, (j + 1) * M_INNER)
        p = jnp.dot(h, w_ref[:, cols], preferred_element_type=F32)
        if j == 1:
            p = p * (M_DK ** -0.5)
        o_ref[:, cols] = p.astype(BF16)
        if j == 2:
            v_t = p.T.astype(BF16)
            for s in range(nslab):
                vt_ref[s] = v_t[:, s * L:(s + 1) * L]


def _inproj_m(x, mod, w, wg, bg):
    tm = TM_PROJ
    nslab = tm // L_CHUNK
    return pl.pallas_call(
        _inproj_m_kernel,
        out_shape=[
            jax.ShapeDtypeStruct((NT, 4 * M_INNER), BF16),
            jax.ShapeDtypeStruct((NT // L_CHUNK, GATE_ROWS, L_CHUNK), F32),
            jax.ShapeDtypeStruct((NT // L_CHUNK, M_INNER, L_CHUNK), BF16),
        ],
        grid=(NT // tm,),
        in_specs=[
            pl.BlockSpec((tm, D_MODEL), lambda i: (i, 0)),
            pl.BlockSpec((None, 6, D_MODEL), lambda i: (_cond_index(i, tm), 0, 0)),
            pl.BlockSpec((D_MODEL, 4 * M_INNER), lambda i: (0, 0)),
            pl.BlockSpec((D_MODEL, LANES), lambda i: (0, 0)),
            pl.BlockSpec((1, LANES), lambda i: (0, 0)),
        ],
        out_specs=[
            pl.BlockSpec((tm, 4 * M_INNER), lambda i: (i, 0)),
            pl.BlockSpec((nslab, GATE_ROWS, L_CHUNK), lambda i: (i, 0, 0)),
            pl.BlockSpec((nslab, M_INNER, L_CHUNK), lambda i: (i, 0, 0)),
        ],
        compiler_params=_params(("arbitrary",)),
        name="inproj_m",
    )(x, mod, w, wg, bg)


def _log_sigmoid(x):
    return jnp.minimum(x, 0.0) - jnp.log1p(jnp.exp(-jnp.abs(x)))


def _split3(x):
    x1 = x.astype(BF16)
    r1 = x - x1.astype(F32)
    x2 = r1.astype(BF16)
    x3 = (r1 - x2.astype(F32)).astype(BF16)
    return x1, x2, x3


def _mlstm_prefix(gt):
    L = gt.shape[1]
    parts = jnp.concatenate([t.astype(F32) for t in _split3(gt)]
                            + [jnp.zeros((GATE_ROWS, L), F32)], axis=0).astype(BF16)
    upper = jnp.where(lax.broadcasted_iota(jnp.int32, (L, L), 0)
                      <= lax.broadcasted_iota(jnp.int32, (L, L), 1), 1.0, 0.0).astype(BF16)
    r = jnp.dot(parts, upper, preferred_element_type=F32)
    return r[0:GATE_ROWS] + r[GATE_ROWS:2 * GATE_ROWS] + r[2 * GATE_ROWS:3 * GATE_ROWS]


def _rows3(x, n):
    g = x.shape[0]
    return jnp.concatenate([t.astype(F32) for t in _split3(x)]
                           + [jnp.zeros((g, ONES_ROWS - 3, n), F32)], axis=1).astype(BF16)


def _sum3(x, r0):
    return x[:, r0:r0 + 1] + x[:, r0 + 1:r0 + 2] + x[:, r0 + 2:r0 + 3]


def _mlstm_group(rev, q, k, vt, gt, pre, state):
    G, L = q.shape[0], q.shape[1]
    d = 1 if rev else 0
    tpos = lax.broadcasted_iota(jnp.int32, (L, L), 0)
    spos = lax.broadcasted_iota(jnp.int32, (L, L), 1)
    keep = (tpos >= spos) if rev else (tpos <= spos)
    if rev:
        tot = pre[:, 3:4, L - 1:L]
        a_row = tot - pre[:, 3:4, :] + gt[:, 3:4, :]
    else:
        a_row = pre[:, 2:3, :]
        tot = a_row[:, :, L - 1:L]
    b_row = gt[:, d:d + 1, :] - a_row
    m = jnp.zeros((G, 1, 1), F32) if state is None else state[0]

    gp = -(-G // 8) * 8
    rows = [b_row[p] for p in range(G)] + [jnp.zeros((gp - G, L), F32)] * (1 if gp > G else 0)
    bt = jnp.concatenate(rows, axis=0).T
    bm = jnp.stack([jnp.where(keep, bt[:, p:p + 1], -jnp.inf) for p in range(G)])
    mm = jnp.maximum(m, jnp.max(bm, axis=1, keepdims=True))
    w = jnp.exp(bm - mm)
    s = jnp.einsum('gsd,gtd->gst', k, q, preferred_element_type=F32)
    qk = s * w
    num = jnp.einsum('gvs,gst->gvt', vt, qk.astype(BF16), preferred_element_type=F32)
    den = jnp.sum(qk, axis=1, keepdims=True)
    if state is not None:
        _, ct, n = state
        cn = jnp.concatenate([ct.astype(BF16), _rows3(n, n.shape[2])], axis=1)
        inter = jnp.einsum('gcd,gtd->gct', cn, q, preferred_element_type=F32)
        s_inter = jnp.exp(m - mm)
        num = num + s_inter * inter[:, :M_DV]
        den = den + s_inter * _sum3(inter, M_DV)
    h = num * (1.0 / jnp.maximum(jnp.abs(den), jnp.exp(-(a_row + mm))))

    g_row = tot + b_row
    m_new = jnp.maximum(tot + m, jnp.max(g_row, axis=2, keepdims=True))
    ws = jnp.exp(g_row - m_new)
    lhs = jnp.concatenate([(vt.astype(F32) * ws).astype(BF16), _rows3(ws, L)], axis=1)
    upd = jnp.einsum('gcs,gsd->gcd', lhs, k, preferred_element_type=F32)
    ct_new = upd[:, :M_DV]
    n_new = _sum3(upd, M_DV)
    if state is not None:
        decay = jnp.exp(tot + m - m_new)
        ct_new = decay * ct + ct_new
        n_new = decay * n + n_new
    return h, (m_new, ct_new, n_new)


def _heads(x):
    w = x.shape[1] // M_HEADS
    return jnp.stack([x[:, h * w:(h + 1) * w] for h in range(M_HEADS)])


def _mlstm_finish(ht, o, gain):
    ms = jnp.mean(ht * ht, axis=1, keepdims=True)
    hn = ht * lax.rsqrt(ms + EPS)
    hn = jnp.concatenate([hn[h].T for h in range(M_HEADS)], axis=1)
    return (_sigmoid(o.astype(F32)) * hn * gain).astype(BF16)


def _mlstm_sample_kernel(qf_ref, kf_ref, vtf_ref, gtf_ref, of_ref, qb_ref, kb_ref, vtb_ref, gtb_ref,
                         ob_ref, gain_ref, c0_ref, n0_ref, m0_ref, hi_ref, lo_ref,
                         hs_ref, ct_scr, n_scr, m_scr, *, nc):
    j = pl.program_id(1)
    jb = nc - 1 - j
    H = M_HEADS
    L = L_CHUNK

    @pl.when(j == 0)
    def _():
        for d in range(2):
            for h in range(H):
                ct_scr[d * H + h] = c0_ref[d, h].T
                n_scr[d * H + h] = n0_ref[d, h]
                m_scr[d * H + h] = jnp.broadcast_to(m0_ref[d, h], (1, LANES))

    def run(rev, q_ref, k_ref, vt_ref, gt_ref, lo):
        gt = gt_ref[...]
        pre = _mlstm_prefix(gt).reshape(H, GATE_STRIDE, L)
        state = (m_scr[lo:lo + H][:, :, 0:1], ct_scr[lo:lo + H], n_scr[lo:lo + H])
        h, (m_new, ct_new, n_new) = _mlstm_group(
            rev, _heads(q_ref[...]), _heads(k_ref[...]), vt_ref[...].reshape(H, M_DV, L),
            gt.reshape(H, GATE_STRIDE, L), pre, state)
        ct_scr[lo:lo + H] = ct_new
        n_scr[lo:lo + H] = n_new
        m_scr[lo:lo + H] = jnp.broadcast_to(m_new, (H, 1, LANES))
        return h

    h_f = run(False, qf_ref, kf_ref, vtf_ref, gtf_ref, 0)
    h_b = run(True, qb_ref, kb_ref, vtb_ref, gtb_ref, H)

    @pl.when(j < nc // 2)
    def _():
        hs_ref[j] = h_f
        hs_ref[jb] = h_b

    @pl.when(j >= nc // 2)
    def _():
        gain = gain_ref[...]
        hi_ref[...] = _mlstm_finish(hs_ref[j] + h_f, of_ref[...], gain)
        lo_ref[...] = _mlstm_finish(hs_ref[jb] + h_b, ob_ref[...], gain)


def _mlstm_sample(qkvo, vt, gates_t, gain, init):
    nc = DEC_SEQ // L_CHUNK
    half = nc // 2
    c0 = NP // L_CHUNK
    L = L_CHUNK
    fwd = lambda b, j: c0 + b * nc + j
    bwd = lambda b, j: c0 + b * nc + nc - 1 - j

    def specs(f):
        return [
            pl.BlockSpec((L, M_INNER), lambda b, j: (f(b, j), 0)),
            pl.BlockSpec((L, M_INNER), lambda b, j: (f(b, j), 1)),
            pl.BlockSpec((None, M_INNER, L), lambda b, j: (f(b, j), 0, 0)),
            pl.BlockSpec((None, GATE_ROWS, L), lambda b, j: (f(b, j), 0, 0)),
            pl.BlockSpec((L, M_INNER), lambda b, j: (f(b, j), 3)),
        ]

    state_specs = [
        pl.BlockSpec((None, 2, M_HEADS, M_DK, M_DV), lambda b, j: (b, 0, 0, 0, 0)),
        pl.BlockSpec((None, 2, M_HEADS, 1, M_DK), lambda b, j: (b, 0, 0, 0, 0)),
        pl.BlockSpec((None, 2, M_HEADS, 1, 1), lambda b, j: (b, 0, 0, 0, 0)),
    ]
    out_rows = DEC_BATCH * DEC_SEQ // 2
    return pl.pallas_call(
        functools.partial(_mlstm_sample_kernel, nc=nc),
        out_shape=[jax.ShapeDtypeStruct((out_rows, M_INNER), BF16)] * 2,
        grid=(DEC_BATCH, nc),
        in_specs=specs(fwd) + specs(bwd) + [pl.BlockSpec((1, M_INNER), lambda b, j: (0, 0))] + state_specs,
        out_specs=[
            pl.BlockSpec((L, M_INNER), lambda b, j: (b * half + jnp.maximum(j, half) - half, 0)),
            pl.BlockSpec((L, M_INNER), lambda b, j: (b * half + jnp.minimum(nc - 1 - j, half - 1), 0)),
        ],
        scratch_shapes=[
            pltpu.VMEM((nc, M_HEADS, M_DV, L), F32),
            pltpu.VMEM((2 * M_HEADS, M_DV, M_DK), F32),
            pltpu.VMEM((2 * M_HEADS, 1, M_DK), F32),
            pltpu.VMEM((2 * M_HEADS, 1, LANES), F32),
        ],
        compiler_params=_params(("arbitrary", "arbitrary")),
        name="mlstm",
    )(qkvo, qkvo, vt, gates_t, qkvo, qkvo, qkvo, vt, gates_t, qkvo, gain, *init)


def _mlstm_prompt_kernel(q_ref, k_ref, vt_ref, gt_ref, o_ref, gain_ref, out_ref, co_ref, no_ref, mo_ref,
                         *, nseq):
    H = M_HEADS
    L = L_CHUNK
    G = nseq * H
    q = jnp.concatenate([_heads(q_ref[b * L:(b + 1) * L, :]) for b in range(nseq)], axis=0)
    k = jnp.concatenate([_heads(k_ref[b * L:(b + 1) * L, :]) for b in range(nseq)], axis=0)
    vt = vt_ref[...].reshape(G, M_DV, L)
    gt3 = jnp.concatenate([gt_ref[b] for b in range(nseq)], axis=0).reshape(G, GATE_STRIDE, L)
    pre3 = jnp.concatenate([_mlstm_prefix(gt_ref[b]) for b in range(nseq)],
                           axis=0).reshape(G, GATE_STRIDE, L)
    gain = gain_ref[...]
    hs = None
    for d in range(2):
        h, (m_new, ct_new, n_new) = _mlstm_group(d == 1, q, k, vt, gt3, pre3, None)
        hs = h if hs is None else hs + h
        for b in range(nseq):
            for hh in range(H):
                co_ref[b, d, hh] = ct_new[b * H + hh].T
                no_ref[b, d, hh] = n_new[b * H + hh]
                mo_ref[b, d, hh] = m_new[b * H + hh]
    for b in range(nseq):
        out_ref[b * L:(b + 1) * L, :] = _mlstm_finish(hs[b * H:(b + 1) * H], o_ref[b * L:(b + 1) * L, :], gain)


def _mlstm_prompt(qkvo, vt, gates_t, gain):
    nseq = MLSTM_SEQS
    L = L_CHUNK
    rows = nseq * L
    state_specs = [
        pl.BlockSpec((nseq, 2, M_HEADS, M_DK, M_DV), lambda i: (i, 0, 0, 0, 0)),
        pl.BlockSpec((nseq, 2, M_HEADS, 1, M_DK), lambda i: (i, 0, 0, 0, 0)),
        pl.BlockSpec((nseq, 2, M_HEADS, 1, 1), lambda i: (i, 0, 0, 0, 0)),
    ]
    return pl.pallas_call(
        functools.partial(_mlstm_prompt_kernel, nseq=nseq),
        out_shape=[
            jax.ShapeDtypeStruct((NP, M_INNER), BF16),
            jax.ShapeDtypeStruct((BATCH, 2, M_HEADS, M_DK, M_DV), F32),
            jax.ShapeDtypeStruct((BATCH, 2, M_HEADS, 1, M_DK), F32),
            jax.ShapeDtypeStruct((BATCH, 2, M_HEADS, 1, 1), F32),
        ],
        grid=(BATCH // nseq,),
        in_specs=[
            pl.BlockSpec((rows, M_INNER), lambda i: (i, 0)),
            pl.BlockSpec((rows, M_INNER), lambda i: (i, 1)),
            pl.BlockSpec((nseq, M_INNER, L), lambda i: (i, 0, 0)),
            pl.BlockSpec((nseq, GATE_ROWS, L), lambda i: (i, 0, 0)),
            pl.BlockSpec((rows, M_INNER), lambda i: (i, 3)),
            pl.BlockSpec((1, M_INNER), lambda i: (0, 0)),
        ],
        out_specs=[pl.BlockSpec((rows, M_INNER), lambda i: (i, 0))] + state_specs,
        compiler_params=_params(("arbitrary",)),
        name="mlstm_state",
    )(qkvo, qkvo, vt, gates_t, qkvo, gain)


@jax.jit
def kernel(x_prompt, x_sample, cache_k, cache_v, state_c, state_n, state_m, c, c_ctx, w_ada, b_ada,
           ln_g, ln_b, w_ff1, w_ff2, w_in_a, q_gain, k_gain, conv_w, conv_b, conv_ln_g, conv_ln_b,
           w_out_a, w_in_m, b_gate_m, mh_gain, w_out_m):
    xp = x_prompt.reshape(NP, D_MODEL)
    xs = x_sample.reshape(NS, D_MODEL)

    cond = jnp.concatenate(
        [c_ctx[None, :], c, jnp.zeros((N_COND - 1 - DEC_BATCH, D_MODEL), F32)], axis=0)
    mods = _adaln(cond, w_ada, b_ada)

    ln = lambda l, s: (ln_g[l, s].reshape(1, D_MODEL), ln_b[l, s].reshape(1, D_MODEL))

    q, k, v, kf, vf, u = _inproj_a(xp, xs, mods[0], w_in_a, q_gain[0], k_gain[0])
    att_p = _attention_prompt(q, k, v)
    ctx_k = cache_k[:, 0].reshape(DEC_BATCH, PAST_LEN, KV_W)
    ctx_v = cache_v[:, 0].reshape(DEC_BATCH, PAST_LEN, KV_W)
    att_s = _attention_sample(q, k, v, ctx_k, ctx_v)
    tail_src = (att_p, att_s, u, xp, xs, conv_w[0], conv_b[0], conv_ln_g[0], conv_ln_b[0])
    x2 = _block_tail("a", tail_src, 0, NT, mods[0], w_out_a, w_ff1, w_ff2, 0, ln(0, 0), ln(0, 1))

    wm = w_in_m[0]
    src = jnp.array([4 * M_INNER + g * M_HEADS + h
                     for h in range(M_HEADS) for g in (0, 2, 1, 3)], jnp.int32)
    dst = jnp.array([h * GATE_STRIDE + t for h in range(M_HEADS) for t in range(4)], jnp.int32)
    wg = jnp.zeros((D_MODEL, LANES), F32).at[:, dst].set(wm[:, src]).astype(BF16)
    bg = jnp.zeros((1, LANES), F32).at[0, dst].set(b_gate_m[0][src - 4 * M_INNER])
    qkvo, gates_t, vt_m = _inproj_m(x2, mods[1], wm[:, :4 * M_INNER].astype(BF16), wg, bg)
    gain = mh_gain[0].reshape(1, M_INNER)
    hg_p, st_c, st_n, st_m = _mlstm_prompt(qkvo, vt_m, gates_t, gain)
    init = (state_c[:, 0], state_n[:, 0].reshape(DEC_BATCH, 2, M_HEADS, 1, M_DK),
            state_m[:, 0].reshape(DEC_BATCH, 2, M_HEADS, 1, 1))
    hg_hi, hg_lo = _mlstm_sample(qkvo, vt_m, gates_t, gain, init)
    tail_src = (hg_p, hg_lo, hg_hi, x2)
    y_p = _block_tail("m", tail_src, 0, NP, mods[1], w_out_m, w_ff1, w_ff2, 1, ln(1, 0), ln(1, 1))
    y_s = _block_tail("m", tail_src, NP, NS, mods[1], w_out_m, w_ff1, w_ff2, 1, ln(1, 0), ln(1, 1))

    new_k = jnp.transpose(kf, (0, 3, 1, 2))[:, None]
    new_v = jnp.transpose(vf, (0, 3, 1, 2))[:, None]
    return (y_p.reshape(BATCH, SEQ, D_MODEL), y_s.reshape(DEC_BATCH, DEC_SEQ, D_MODEL),
            new_k, new_v,
            st_c.reshape(BATCH, 1, 2, M_HEADS, M_DK, M_DV),
            st_n.reshape(BATCH, 1, 2, M_HEADS, M_DK),
            st_m.reshape(BATCH, 1, 2, M_HEADS))
```

```python
import functools

import jax
import jax.numpy as jnp
from jax import lax
from jax.experimental import pallas as pl
from jax.experimental.pallas import tpu as pltpu

F32 = jnp.float32
BF16 = jnp.bfloat16

D_MODEL = 1024
BATCH = 32
SEQ = 256
DEPTH = 2
DEC_BATCH = 4
DEC_SEQ = 4096
PAST_LEN = 512
GRID_W = 64
ATT_HEADS = 8
ATT_KV_HEADS = 2
HEAD_DIM = 64
ATT_GROUP = ATT_HEADS // ATT_KV_HEADS
ATT_W = ATT_HEADS * HEAD_DIM
KV_W = ATT_KV_HEADS * HEAD_DIM
ROPE_AXIS_DIM = HEAD_DIM // 2
ROPE_THETA = 10000.0
CONV_CH = D_MODEL // 2
CONV_K = 31
EVEN_IN = ATT_W + 2 * KV_W + 2 * CONV_CH
M_HEADS = 4
M_INNER = D_MODEL
M_DK = M_INNER // M_HEADS
M_DV = M_INNER // M_HEADS
D_FF = 4 * D_MODEL
ALPHA = (2 * DEPTH) ** 0.25
EPS = 1e-6
LOG2E = 1.4426950408889634

NP = BATCH * SEQ
NS = DEC_BATCH * DEC_SEQ
NT = NP + NS
N_COND = 8

LANES = 128
VMEM_LIMIT = 56 * 1024 * 1024

TM_PROJ = 512
TF_FFN = 1024
TQ_ATT = 512
TK_ATT = 512
R_CONV = 256
HALO = 16
L_CHUNK = 256
GATE_STRIDE = 8
ONES_ROWS = 16
GATE_ROWS = 32
MLSTM_SEQS = 4
TAIL_CAST_STEPS = 16
INPROJ_SUBTILES = 4
ATT_UNROLL = 9


def _cond_index(i_global, tm):
    npt = NP // tm
    tps = DEC_SEQ // tm
    return jnp.where(i_global < npt, 0, 1 + (i_global - npt) // tps)


def _layernorm(r, g, b):
    mu = jnp.mean(r, axis=-1, keepdims=True)
    d = r - mu
    var = jnp.mean(d * d, axis=-1, keepdims=True)
    return d * lax.rsqrt(var + EPS) * g + b


def _sigmoid(x):
    return 1.0 / (1.0 + jnp.exp(-x))


def _params(sem, vmem=VMEM_LIMIT, flags=None):
    return pltpu.CompilerParams(dimension_semantics=sem, vmem_limit_bytes=vmem, flags=flags)


def _adaln_kernel(cond_ref, w_ref, b_ref, o_ref):
    c = cond_ref[...]
    s = (c * _sigmoid(c)).astype(BF16)
    o_ref[...] = jnp.dot(s, w_ref[...].astype(BF16), preferred_element_type=F32) + b_ref[...]


def _adaln(cond, w_ada, b_ada):
    tn = 1536
    n = 6 * D_MODEL
    out = pl.pallas_call(
        _adaln_kernel,
        out_shape=jax.ShapeDtypeStruct((DEPTH, N_COND, n), F32),
        grid=(DEPTH, n // tn),
        in_specs=[
            pl.BlockSpec((N_COND, D_MODEL), lambda l, j: (0, 0)),
            pl.BlockSpec((None, D_MODEL, tn), lambda l, j: (l, 0, j)),
            pl.BlockSpec((None, 1, tn), lambda l, j: (l, 0, j)),
        ],
        out_specs=pl.BlockSpec((None, N_COND, tn), lambda l, j: (l, 0, j)),
        compiler_params=_params(("arbitrary", "arbitrary")),
        name="adaln",
    )(cond, w_ada, b_ada.reshape(DEPTH, 1, n))
    return out.reshape(DEPTH, N_COND, 6, D_MODEL)


def _cast_once(w_ref, w_scr):
    @pl.when(pl.program_id(0) == 0)
    def _():
        w_scr[...] = w_ref[...].astype(BF16)


def _inproj_a_kernel(xp_ref, xs_ref, mod_ref, w_ref, qg_ref, kg_ref, cos_ref, sin_ref,
                     q_ref, k_ref, v_ref, kf_ref, vf_ref, u_ref, w_scr, *, npt):
    i = pl.program_id(0)
    _cast_once(w_ref, w_scr)
    mod = mod_ref[...]
    tm = u_ref.shape[0]
    sub = tm // INPROJ_SUBTILES

    ri = lax.broadcasted_iota(jnp.int32, (LANES, LANES), 0) // HEAD_DIM
    ci = lax.broadcasted_iota(jnp.int32, (LANES, LANES), 1) // HEAD_DIM
    seg = jnp.where(ri == ci, 1.0, 0.0).astype(BF16)
    lane = lax.broadcasted_iota(jnp.int32, (sub, LANES), 1)
    even = (lane % 2) == 0
    qg = qg_ref[...]
    kg = kg_ref[...]

    def norm(xc, gain):
        ss = jnp.dot((xc * xc).astype(BF16), seg, preferred_element_type=F32)
        return xc * lax.rsqrt(ss * (1.0 / HEAD_DIM) + EPS) * gain

    cache_rows = []
    for r in range(INPROJ_SUBTILES):
        rows = slice(r * sub, (r + 1) * sub)
        x = jnp.where(i < npt, xp_ref[rows, :], xs_ref[rows, :])
        h = (x * (1.0 + mod[1:2]) + mod[0:1]).astype(BF16)
        proj = jnp.dot(h, w_scr[...], preferred_element_type=F32)
        cos = cos_ref[rows, :]
        sin = sin_ref[rows, :]

        def rope(xn):
            partner = jnp.where(even, pltpu.roll(xn, LANES - 1, 1), pltpu.roll(xn, 1, 1))
            return xn * cos + partner * sin

        for c in range(ATT_W // LANES):
            qr = rope(norm(proj[:, c * LANES:(c + 1) * LANES], qg))
            qs = (qr * (HEAD_DIM ** -0.5 * LOG2E)).astype(BF16)
            q_ref[2 * c, rows, :] = qs[:, :HEAD_DIM]
            q_ref[2 * c + 1, rows, :] = qs[:, HEAD_DIM:]

        kn = norm(proj[:, ATT_W:ATT_W + KV_W], kg)
        kr = rope(kn).astype(BF16)
        k_ref[0, rows, :] = kr[:, :HEAD_DIM]
        k_ref[1, rows, :] = kr[:, HEAD_DIM:]

        v = proj[:, ATT_W + KV_W:ATT_W + 2 * KV_W]
        cache_rows.append((kn, v))
        vb = v.astype(BF16)
        v_ref[0, rows, :] = vb[:, :HEAD_DIM]
        v_ref[1, rows, :] = vb[:, HEAD_DIM:]

        off = ATT_W + 2 * KV_W
        a = proj[:, off:off + CONV_CH]
        gt = proj[:, off + CONV_CH:off + 2 * CONV_CH]
        u_ref[rows, :] = a * _sigmoid(gt)

    @pl.when(i < npt)
    def _():
        for r, (kn, v) in enumerate(cache_rows):
            j, t0 = (r * sub) // SEQ, (r * sub) % SEQ
            kf_ref[j, :, :, t0:t0 + sub] = kn.T.reshape(ATT_KV_HEADS, HEAD_DIM, sub)
            vf_ref[j, :, :, t0:t0 + sub] = v.T.reshape(ATT_KV_HEADS, HEAD_DIM, sub)


def _rope_tables(tm):
    t = jnp.arange(DEC_SEQ)
    row = (t // GRID_W).astype(F32)
    col = (t % GRID_W).astype(F32)
    freqs = ROPE_THETA ** (-jnp.arange(0, ROPE_AXIS_DIM, 2, dtype=F32) / ROPE_AXIS_DIM)
    ang = jnp.concatenate([row[:, None] * freqs, col[:, None] * freqs], axis=-1)
    pair = (jnp.arange(LANES) % HEAD_DIM) // 2
    sign = jnp.where(jnp.arange(LANES) % 2 == 0, -1.0, 1.0).astype(F32)
    cos = jnp.cos(ang)[:, pair]
    sin = jnp.sin(ang)[:, pair] * sign
    cos = jnp.concatenate([jnp.ones((tm, LANES), F32), cos], axis=0)
    sin = jnp.concatenate([jnp.zeros((tm, LANES), F32), sin], axis=0)
    return cos, sin


def _inproj_a(xp, xs, mod, w, q_gain, k_gain):
    tm = TM_PROJ
    npt = NP // tm
    tps = DEC_SEQ // tm
    nt = NT // tm
    cos, sin = _rope_tables(tm)
    qg = jnp.tile(q_gain, LANES // HEAD_DIM).reshape(1, LANES)
    kg = jnp.tile(k_gain, LANES // HEAD_DIM).reshape(1, LANES)

    def rope_idx(i):
        return (jnp.where(i < npt, 0, 1 + (i - npt) % tps), 0)

    return pl.pallas_call(
        functools.partial(_inproj_a_kernel, npt=npt),
        out_shape=[
            jax.ShapeDtypeStruct((ATT_HEADS, NT, HEAD_DIM), BF16),
            jax.ShapeDtypeStruct((ATT_KV_HEADS, NT, HEAD_DIM), BF16),
            jax.ShapeDtypeStruct((ATT_KV_HEADS, NT, HEAD_DIM), BF16),
            jax.ShapeDtypeStruct((BATCH, ATT_KV_HEADS, HEAD_DIM, SEQ), F32),
            jax.ShapeDtypeStruct((BATCH, ATT_KV_HEADS, HEAD_DIM, SEQ), F32),
            jax.ShapeDtypeStruct((NT, CONV_CH), F32),
        ],
        grid=(nt,),
        in_specs=[
            pl.BlockSpec((tm, D_MODEL), lambda i: (jnp.minimum(i, npt - 1), 0)),
            pl.BlockSpec((tm, D_MODEL), lambda i: (jnp.maximum(i - npt, 0), 0)),
            pl.BlockSpec((None, 6, D_MODEL), lambda i: (_cond_index(i, tm), 0, 0)),
            pl.BlockSpec((None, D_MODEL, EVEN_IN), lambda i: (0, 0, 0)),
            pl.BlockSpec((1, LANES), lambda i: (0, 0)),
            pl.BlockSpec((1, LANES), lambda i: (0, 0)),
            pl.BlockSpec((tm, LANES), rope_idx),
            pl.BlockSpec((tm, LANES), rope_idx),
        ],
        out_specs=[
            pl.BlockSpec((ATT_HEADS, tm, HEAD_DIM), lambda i: (0, i, 0)),
            pl.BlockSpec((ATT_KV_HEADS, tm, HEAD_DIM), lambda i: (0, i, 0)),
            pl.BlockSpec((ATT_KV_HEADS, tm, HEAD_DIM), lambda i: (0, i, 0)),
            pl.BlockSpec((tm // SEQ, ATT_KV_HEADS, HEAD_DIM, SEQ), lambda i: (jnp.minimum(i, npt - 1), 0, 0, 0)),
            pl.BlockSpec((tm // SEQ, ATT_KV_HEADS, HEAD_DIM, SEQ), lambda i: (jnp.minimum(i, npt - 1), 0, 0, 0)),
            pl.BlockSpec((tm, CONV_CH), lambda i: (i, 0)),
        ],
        scratch_shapes=[pltpu.VMEM((D_MODEL, EVEN_IN), BF16)],
        compiler_params=_params(("arbitrary",)),
        name="inproj_a",
    )(xp, xs, mod, w, qg, kg, cos, sin)


def _attn_chunk(qs, kc, vc, m, acc):
    s = lax.dot_general(qs, kc, (((1,), (1,)), ((), ())), preferred_element_type=F32)
    fold = s[:, :LANES]
    for c in range(1, s.shape[1] // LANES):
        fold = jnp.maximum(fold, s[:, c * LANES:(c + 1) * LANES])
    m_new = jnp.maximum(m, jnp.max(fold, axis=1, keepdims=True))
    alpha = jnp.exp2(m - m_new)
    p = jnp.exp2(s - m_new).astype(BF16)
    va = jnp.concatenate([vc, jnp.ones((vc.shape[0], LANES - HEAD_DIM), BF16)], axis=1)
    acc = alpha * acc + jnp.dot(p, va, preferred_element_type=F32)
    return m_new, acc


def _attn_init(nq):
    return jnp.full((nq, 1), -1e30, F32), jnp.zeros((nq, LANES), F32)


def _attn_finish(acc, o_ref):
    tq = o_ref.shape[0]
    o = acc[:, :HEAD_DIM] * (1.0 / acc[:, HEAD_DIM:HEAD_DIM + 1])
    o_ref[...] = jnp.concatenate([o[h * tq:(h + 1) * tq] for h in range(ATT_GROUP)],
                                 axis=1).astype(BF16)


def _stack_heads(q_ref):
    return q_ref[...].reshape(ATT_GROUP * q_ref.shape[1], HEAD_DIM)


def _attn_prompt_kernel(q_ref, k_ref, v_ref, o_ref):
    qs = _stack_heads(q_ref)
    _, acc = _attn_chunk(qs, k_ref[...], v_ref[...], *_attn_init(qs.shape[0]))
    _attn_finish(acc, o_ref)


def _attn_sample_kernel(q_ref, k_ref, v_ref, ck_ref, cv_ref, o_ref, k_scr, v_scr, *, nchunks):
    @pl.when(pl.program_id(2) == 0)
    def _():
        own = k_ref.shape[0]
        k_scr[0:own, :] = k_ref[...]
        v_scr[0:own, :] = v_ref[...]
        first = pl.program_id(1) == 0
        ck = ck_ref[...]
        cv = cv_ref[...]
        k_scr[own:, :] = jnp.where(first, ck[:, :HEAD_DIM], ck[:, HEAD_DIM:]).astype(BF16)
        v_scr[own:, :] = jnp.where(first, cv[:, :HEAD_DIM], cv[:, HEAD_DIM:]).astype(BF16)

    qs = _stack_heads(q_ref)

    def body(j, carry):
        rows = pl.ds(pl.multiple_of(j * TK_ATT, TK_ATT), TK_ATT)
        return _attn_chunk(qs, k_scr[rows, :], v_scr[rows, :], *carry)

    _, acc = lax.fori_loop(0, nchunks, body, _attn_init(qs.shape[0]), unroll=ATT_UNROLL)
    _attn_finish(acc, o_ref)


def _attention_prompt(q, k, v):
    return pl.pallas_call(
        _attn_prompt_kernel,
        out_shape=jax.ShapeDtypeStruct((NP, ATT_W), BF16),
        grid=(BATCH, ATT_KV_HEADS),
        in_specs=[
            pl.BlockSpec((ATT_GROUP, SEQ, HEAD_DIM), lambda b, g: (g, b, 0)),
            pl.BlockSpec((None, SEQ, HEAD_DIM), lambda b, g: (g, b, 0)),
            pl.BlockSpec((None, SEQ, HEAD_DIM), lambda b, g: (g, b, 0)),
        ],
        out_specs=pl.BlockSpec((SEQ, ATT_GROUP * HEAD_DIM), lambda b, g: (b, g)),
        compiler_params=_params(("arbitrary", "arbitrary")),
        name="attn_prompt",
    )(q, k, v)


def _attention_sample(q, k, v, ctx_k, ctx_v):
    tq = TQ_ATT
    nkeys = DEC_SEQ + PAST_LEN
    nchunks = nkeys // TK_ATT
    q_off = NP // tq
    kv_off = NP // DEC_SEQ
    return pl.pallas_call(
        functools.partial(_attn_sample_kernel, nchunks=nchunks),
        out_shape=jax.ShapeDtypeStruct((NS, ATT_W), BF16),
        grid=(DEC_BATCH, ATT_KV_HEADS, DEC_SEQ // tq),
        in_specs=[
            pl.BlockSpec((ATT_GROUP, tq, HEAD_DIM),
                         lambda b, g, i: (g, q_off + b * (DEC_SEQ // tq) + i, 0)),
            pl.BlockSpec((None, DEC_SEQ, HEAD_DIM), lambda b, g, i: (g, kv_off + b, 0)),
            pl.BlockSpec((None, DEC_SEQ, HEAD_DIM), lambda b, g, i: (g, kv_off + b, 0)),
            pl.BlockSpec((None, PAST_LEN, KV_W), lambda b, g, i: (b, 0, 0)),
            pl.BlockSpec((None, PAST_LEN, KV_W), lambda b, g, i: (b, 0, 0)),
        ],
        out_specs=pl.BlockSpec((tq, ATT_GROUP * HEAD_DIM),
                               lambda b, g, i: (b * (DEC_SEQ // tq) + i, g)),
        scratch_shapes=[pltpu.VMEM((nkeys, HEAD_DIM), BF16), pltpu.VMEM((nkeys, HEAD_DIM), BF16)],
        compiler_params=_params(("arbitrary", "arbitrary", "arbitrary")),
        name="attn_sample",
    )(q, k, v, ctx_k, ctx_v)


def _conv_window(left, cur, right, w_ref, b_ref, g_ref, bb_ref, win_ref, y_ref, out_ref):
    r = cur.shape[0]
    win_ref[0:HALO, :] = left
    win_ref[HALO:HALO + r, :] = cur
    win_ref[HALO + r:2 * HALO + r, :] = right
    base = HALO - CONV_K // 2
    sub = 8
    for c in range(CONV_CH // LANES):
        cs = slice(c * LANES, (c + 1) * LANES)
        acc = None
        for res in range(sub):
            p = None
            for k in range(CONV_K):
                if (base + k) % sub != res:
                    continue
                a = (base + k) - res
                term = win_ref[a:a + r + sub, cs] * w_ref[k:k + 1, cs]
                p = term if p is None else p + term
            if p is None:
                continue
            p = p[res:res + r]
            acc = p if acc is None else acc + p
        y_ref[:, cs] = acc + b_ref[:, cs]
    y = _layernorm(y_ref[...], g_ref[...], bb_ref[...])
    out_ref[...] = (y * _sigmoid(y)).astype(BF16)


def _tail_kernel(*refs, mode, ncast, npt, tps, off):
    nsrc = 11 if mode == "a" else 4
    src = refs[:nsrc]
    (mod_ref, wo_ref, w1_ref, w2_ref, g0_ref, b0_ref, g1_ref, b1_ref, o_ref,
     wo_scr, w1_scr, w2_scr) = refs[nsrc:nsrc + 12]
    s = pl.program_id(0)
    per = ncast // w1_scr.shape[0]
    wcol = w1_ref.shape[1]

    @pl.when(s < ncast)
    def _():
        wo_scr[s] = wo_ref[...].astype(BF16)
        w2_scr[s] = w2_ref[...].astype(BF16)

    for q in range(per):
        @pl.when(jnp.logical_and(s < ncast, s % per == q))
        def _():
            w1_scr[s // per, :, q * wcol:(q + 1) * wcol] = w1_ref[...].astype(BF16)

    def conv_half(r, tile):
        ul_ref, u_ref, ur_ref, cw_ref, cb_ref, cg_ref, cbb_ref = src[2:5] + src[7:11]
        win_scr, y_scr, uc_scr = refs[nsrc + 12:]
        ctx = tile < npt
        t = (tile - npt) % tps
        half = u_ref.shape[0] // 2
        starts = jnp.logical_or(ctx, jnp.logical_and(t == 0, r == 0))
        ends = jnp.logical_or(ctx, jnp.logical_and(t == tps - 1, r == 1))
        left = ul_ref[...] if r == 0 else u_ref[half - HALO:half, :]
        right = u_ref[half:half + HALO, :] if r == 0 else ur_ref[...]
        _conv_window(jnp.where(starts, 0.0, left), u_ref[r * half:(r + 1) * half, :],
                     jnp.where(ends, 0.0, right), cw_ref, cb_ref, cg_ref, cbb_ref,
                     win_scr.at[r], y_scr.at[r], uc_scr.at[r])
        return uc_scr[r]

    @pl.when(s >= ncast)
    def _():
        tile = s - ncast + off
        ctx = tile < npt
        mod = mod_ref[...]
        w_out = wo_scr[...].reshape(D_MODEL, D_MODEL)
        nchunk = w1_scr.shape[0]
        rows_per = w2_scr.shape[0] // nchunk
        half = o_ref.shape[0] // 2
        for r in range(2):
            rows = slice(r * half, (r + 1) * half)
            if mode == "a":
                ap_ref, as_ref, xp_ref, xs_ref = src[0], src[1], src[5], src[6]
                att = jnp.where(ctx, ap_ref[rows, :], as_ref[rows, :])
                x = jnp.where(ctx, xp_ref[rows, :], xs_ref[rows, :])
                y = jnp.dot(att, w_out[0:ATT_W, :], preferred_element_type=F32)
                y = y + jnp.dot(conv_half(r, tile), w_out[ATT_W:, :], preferred_element_type=F32)
            else:
                hp_ref, lo_ref, hi_ref, x_ref = src
                upper = ((tile - npt) % tps) >= tps // 2
                hg = jnp.where(ctx, hp_ref[rows, :], jnp.where(upper, hi_ref[rows, :], lo_ref[rows, :]))
                x = x_ref[rows, :]
                y = jnp.dot(hg, w_out, preferred_element_type=F32)
            x1 = _layernorm(ALPHA * x + mod[2:3] * y, g0_ref[...], b0_ref[...])
            h = (x1 * (1.0 + mod[4:5]) + mod[3:4]).astype(BF16)
            acc = None
            for c in range(nchunk):
                a = jnp.maximum(jnp.dot(h, w1_scr[c], preferred_element_type=F32), 0.0)
                w2 = w2_scr[c * rows_per:(c + 1) * rows_per].reshape(TF_FFN, D_MODEL)
                t = jnp.dot((a * a).astype(BF16), w2, preferred_element_type=F32)
                acc = t if acc is None else acc + t
            o_ref[rows, :] = _layernorm(ALPHA * x1 + mod[5:6] * acc, g1_ref[...], b1_ref[...])


def _block_tail(mode, src, row0, nrows, mod, w_out, w1, w2, layer, ln0, ln1):
    tm = TM_PROJ
    ncast = TAIL_CAST_STEPS
    npt = NP // tm
    tps = DEC_SEQ // tm
    hps = tps // 2
    off = row0 // tm
    ntiles = nrows // tm
    tile = lambda s: jnp.maximum(s - ncast, 0) + off
    blk = lambda s: jnp.minimum(s, ncast - 1)
    first = lambda s: (jnp.minimum(tile(s), npt - 1), 0)
    second = lambda s: (jnp.maximum(tile(s) - npt, 0), 0)

    def half_idx(s, upper):
        t0 = jnp.maximum(tile(s) - npt, 0)
        t = t0 % tps
        t = jnp.maximum(t - hps, 0) if upper else jnp.minimum(t, hps - 1)
        return ((t0 // tps) * hps + t, 0)

    scratch = []
    if mode == "a":
        assert tm == 2 * R_CONV
        att_p, att_s, u, xp, xs, conv_w, conv_b, cln_g, cln_b = src
        hb = tm // HALO
        nh = NT // HALO
        cvec = pl.BlockSpec((1, CONV_CH), lambda s: (0, 0))
        src = (att_p, att_s, u, u, u, xp, xs,
               jnp.concatenate([conv_w, jnp.zeros((1, CONV_CH), F32)], axis=0),
               conv_b.reshape(1, CONV_CH), cln_g.reshape(1, CONV_CH), cln_b.reshape(1, CONV_CH))
        src_specs = [
            pl.BlockSpec((tm, ATT_W), first),
            pl.BlockSpec((tm, ATT_W), second),
            pl.BlockSpec((HALO, CONV_CH), lambda s: (jnp.maximum(tile(s) * hb - 1, 0), 0)),
            pl.BlockSpec((tm, CONV_CH), lambda s: (tile(s), 0)),
            pl.BlockSpec((HALO, CONV_CH), lambda s: (jnp.minimum((tile(s) + 1) * hb, nh - 1), 0)),
            pl.BlockSpec((tm, D_MODEL), first),
            pl.BlockSpec((tm, D_MODEL), second),
            pl.BlockSpec((CONV_K + 1, CONV_CH), lambda s: (0, 0)),
            cvec, cvec, cvec,
        ]
        scratch = [pltpu.VMEM((2, R_CONV + 2 * HALO, CONV_CH), F32), pltpu.VMEM((2, R_CONV, CONV_CH), F32),
                   pltpu.VMEM((2, R_CONV, CONV_CH), BF16)]
    else:
        src_specs = [
            pl.BlockSpec((tm, M_INNER), first),
            pl.BlockSpec((tm, M_INNER), lambda s: half_idx(s, False)),
            pl.BlockSpec((tm, M_INNER), lambda s: half_idx(s, True)),
            pl.BlockSpec((tm, D_MODEL), lambda s: (tile(s), 0)),
        ]
    nchunk = D_FF // TF_FFN
    vec = pl.BlockSpec((1, D_MODEL), lambda s: (0, 0))
    return pl.pallas_call(
        functools.partial(_tail_kernel, mode=mode, ncast=ncast, npt=npt, tps=tps, off=off),
        out_shape=jax.ShapeDtypeStruct((nrows, D_MODEL), F32),
        grid=(ncast + ntiles,),
        in_specs=src_specs + [
            pl.BlockSpec((None, 6, D_MODEL), lambda s: (_cond_index(tile(s), tm), 0, 0)),
            pl.BlockSpec((None, D_MODEL // ncast, D_MODEL), lambda s: (0, blk(s), 0)),
            pl.BlockSpec((None, D_MODEL, D_FF // ncast), lambda s: (layer, 0, blk(s))),
            pl.BlockSpec((None, D_FF // ncast, D_MODEL), lambda s: (layer, blk(s), 0)),
            vec, vec, vec, vec,
        ],
        out_specs=pl.BlockSpec((tm, D_MODEL), lambda s: (jnp.maximum(s - ncast, 0), 0)),
        scratch_shapes=[
            pltpu.VMEM((ncast, D_MODEL // ncast, D_MODEL), BF16),
            pltpu.VMEM((nchunk, D_MODEL, TF_FFN), BF16),
            pltpu.VMEM((ncast, D_FF // ncast, D_MODEL), BF16),
        ] + scratch,
        compiler_params=_params(("arbitrary",)),
        name="tail_" + mode,
    )(*src, mod, w_out, w1, w2, *ln0, *ln1)


def _inproj_m_kernel(x_ref, mod_ref, w_ref, wg_ref, bg_ref, o_ref, gt_ref, vt_ref):
    nslab = vt_ref.shape[0]
    L = L_CHUNK
    mod = mod_ref[...]
    h = (x_ref[...] * (1.0 + mod[1:2]) + mod[0:1]).astype(BF16)

    gates = jnp.dot(h, wg_ref[...], preferred_element_type=F32) + bg_ref[...]
    kind = lax.broadcasted_iota(jnp.int32, gates.shape, 1) % GATE_STRIDE
    gates = jnp.where((kind == 2) | (kind == 3), _log_sigmoid(gates), gates)
    gates_t = gates.T
    for s in range(nslab):
        gt_ref[s] = gates_t[0:GATE_ROWS, s * L:(s + 1) * L]

    for j in range(4):
        cols = slice(j * M_INNER, (j + 1) * M_INNER)
        p = jnp.dot(h, w_ref[:, cols], preferred_element_type=F32)
        if j == 1:
            p = p * (M_DK ** -0.5)
        o_ref[:, cols] = p.astype(BF16)
        if j == 2:
            v_t = p.T.astype(BF16)
            for s in range(nslab):
                vt_ref[s] = v_t[:, s * L:(s + 1) * L]


def _inproj_m(x, mod, w, wg, bg):
    tm = TM_PROJ
    nslab = tm // L_CHUNK
    return pl.pallas_call(
        _inproj_m_kernel,
        out_shape=[
            jax.ShapeDtypeStruct((NT, 4 * M_INNER), BF16),
            jax.ShapeDtypeStruct((NT // L_CHUNK, GATE_ROWS, L_CHUNK), F32),
            jax.ShapeDtypeStruct((NT // L_CHUNK, M_INNER, L_CHUNK), BF16),
        ],
        grid=(NT // tm,),
        in_specs=[
            pl.BlockSpec((tm, D_MODEL), lambda i: (i, 0)),
            pl.BlockSpec((None, 6, D_MODEL), lambda i: (_cond_index(i, tm), 0, 0)),
            pl.BlockSpec((D_MODEL, 4 * M_INNER), lambda i: (0, 0)),
            pl.BlockSpec((D_MODEL, LANES), lambda i: (0, 0)),
            pl.BlockSpec((1, LANES), lambda i: (0, 0)),
        ],
        out_specs=[
            pl.BlockSpec((tm, 4 * M_INNER), lambda i: (i, 0)),
            pl.BlockSpec((nslab, GATE_ROWS, L_CHUNK), lambda i: (i, 0, 0)),
            pl.BlockSpec((nslab, M_INNER, L_CHUNK), lambda i: (i, 0, 0)),
        ],
        compiler_params=_params(("arbitrary",)),
        name="inproj_m",
    )(x, mod, w, wg, bg)


def _log_sigmoid(x):
    return jnp.minimum(x, 0.0) - jnp.log1p(jnp.exp(-jnp.abs(x)))


def _split3(x):
    x1 = x.astype(BF16)
    r1 = x - x1.astype(F32)
    x2 = r1.astype(BF16)
    x3 = (r1 - x2.astype(F32)).astype(BF16)
    return x1, x2, x3


def _mlstm_prefix(gt):
    L = gt.shape[1]
    parts = jnp.concatenate([t.astype(F32) for t in _split3(gt)]
                            + [jnp.zeros((GATE_ROWS, L), F32)], axis=0).astype(BF16)
    upper = jnp.where(lax.broadcasted_iota(jnp.int32, (L, L), 0)
                      <= lax.broadcasted_iota(jnp.int32, (L, L), 1), 1.0, 0.0).astype(BF16)
    r = jnp.dot(parts, upper, preferred_element_type=F32)
    return r[0:GATE_ROWS] + r[GATE_ROWS:2 * GATE_ROWS] + r[2 * GATE_ROWS:3 * GATE_ROWS]


def _rows3(x, n):
    g = x.shape[0]
    return jnp.concatenate([t.astype(F32) for t in _split3(x)]
                           + [jnp.zeros((g, ONES_ROWS - 3, n), F32)], axis=1).astype(BF16)


def _sum3(x, r0):
    return x[:, r0:r0 + 1] + x[:, r0 + 1:r0 + 2] + x[:, r0 + 2:r0 + 3]


def _mlstm_group(rev, q, k, vt, gt, pre, state):
    G, L = q.shape[0], q.shape[1]
    d = 1 if rev else 0
    tpos = lax.broadcasted_iota(jnp.int32, (L, L), 0)
    spos = lax.broadcasted_iota(jnp.int32, (L, L), 1)
    keep = (tpos >= spos) if rev else (tpos <= spos)
    if rev:
        tot = pre[:, 3:4, L - 1:L]
        a_row = tot - pre[:, 3:4, :] + gt[:, 3:4, :]
    else:
        a_row = pre[:, 2:3, :]
        tot = a_row[:, :, L - 1:L]
    b_row = gt[:, d:d + 1, :] - a_row
    m = jnp.zeros((G, 1, 1), F32) if state is None else state[0]

    gp = -(-G // 8) * 8
    rows = [b_row[p] for p in range(G)] + [jnp.zeros((gp - G, L), F32)] * (1 if gp > G else 0)
    bt = jnp.concatenate(rows, axis=0).T
    bm = jnp.stack([jnp.where(keep, bt[:, p:p + 1], -jnp.inf) for p in range(G)])
    mm = jnp.maximum(m, jnp.max(bm, axis=1, keepdims=True))
    w = jnp.exp(bm - mm)
    s = jnp.einsum('gsd,gtd->gst', k, q, preferred_element_type=F32)
    qk = s * w
    num = jnp.einsum('gvs,gst->gvt', vt, qk.astype(BF16), preferred_element_type=F32)
    den = jnp.sum(qk, axis=1, keepdims=True)
    if state is not None:
        _, ct, n = state
        cn = jnp.concatenate([ct.astype(BF16), _rows3(n, n.shape[2])], axis=1)
        inter = jnp.einsum('gcd,gtd->gct', cn, q, preferred_element_type=F32)
        s_inter = jnp.exp(m - mm)
        num = num + s_inter * inter[:, :M_DV]
        den = den + s_inter * _sum3(inter, M_DV)
    h = num * (1.0 / jnp.maximum(jnp.abs(den), jnp.exp(-(a_row + mm))))

    g_row = tot + b_row
    m_new = jnp.maximum(tot + m, jnp.max(g_row, axis=2, keepdims=True))
    ws = jnp.exp(g_row - m_new)
    lhs = jnp.concatenate([(vt.astype(F32) * ws).astype(BF16), _rows3(ws, L)], axis=1)
    upd = jnp.einsum('gcs,gsd->gcd', lhs, k, preferred_element_type=F32)
    ct_new = upd[:, :M_DV]
    n_new = _sum3(upd, M_DV)
    if state is not None:
        decay = jnp.exp(tot + m - m_new)
        ct_new = decay * ct + ct_new
        n_new = decay * n + n_new
    return h, (m_new, ct_new, n_new)


def _heads(x):
    w = x.shape[1] // M_HEADS
    return jnp.stack([x[:, h * w:(h + 1) * w] for h in range(M_HEADS)])


def _mlstm_finish(ht, o, gain):
    ms = jnp.mean(ht * ht, axis=1, keepdims=True)
    hn = ht * lax.rsqrt(ms + EPS)
    hn = jnp.concatenate([hn[h].T for h in range(M_HEADS)], axis=1)
    return (_sigmoid(o.astype(F32)) * hn * gain).astype(BF16)


def _mlstm_sample_kernel(qf_ref, kf_ref, vtf_ref, gtf_ref, of_ref, qb_ref, kb_ref, vtb_ref, gtb_ref,
                         ob_ref, gain_ref, c0_ref, n0_ref, m0_ref, hi_ref, lo_ref,
                         hs_ref, ct_scr, n_scr, m_scr, *, nc):
    j = pl.program_id(1)
    jb = nc - 1 - j
    H = M_HEADS
    L = L_CHUNK

    @pl.when(j == 0)
    def _():
        for d in range(2):
            for h in range(H):
                ct_scr[d * H + h] = c0_ref[d, h].T
                n_scr[d * H + h] = n0_ref[d, h]
                m_scr[d * H + h] = jnp.broadcast_to(m0_ref[d, h], (1, LANES))

    def run(rev, q_ref, k_ref, vt_ref, gt_ref, lo):
        gt = gt_ref[...]
        pre = _mlstm_prefix(gt).reshape(H, GATE_STRIDE, L)
        state = (m_scr[lo:lo + H][:, :, 0:1], ct_scr[lo:lo + H], n_scr[lo:lo + H])
        h, (m_new, ct_new, n_new) = _mlstm_group(
            rev, _heads(q_ref[...]), _heads(k_ref[...]), vt_ref[...].reshape(H, M_DV, L),
            gt.reshape(H, GATE_STRIDE, L), pre, state)
        ct_scr[lo:lo + H] = ct_new
        n_scr[lo:lo + H] = n_new
        m_scr[lo:lo + H] = jnp.broadcast_to(m_new, (H, 1, LANES))
        return h

    h_f = run(False, qf_ref, kf_ref, vtf_ref, gtf_ref, 0)
    h_b = run(True, qb_ref, kb_ref, vtb_ref, gtb_ref, H)

    @pl.when(j < nc // 2)
    def _():
        hs_ref[j] = h_f
        hs_ref[jb] = h_b

    @pl.when(j >= nc // 2)
    def _():
        gain = gain_ref[...]
        hi_ref[...] = _mlstm_finish(hs_ref[j] + h_f, of_ref[...], gain)
        lo_ref[...] = _mlstm_finish(hs_ref[jb] + h_b, ob_ref[...], gain)


def _mlstm_sample(qkvo, vt, gates_t, gain, init):
    nc = DEC_SEQ // L_CHUNK
    half = nc // 2
    c0 = NP // L_CHUNK
    L = L_CHUNK
    fwd = lambda b, j: c0 + b * nc + j
    bwd = lambda b, j: c0 + b * nc + nc - 1 - j

    def specs(f):
        return [
            pl.BlockSpec((L, M_INNER), lambda b, j: (f(b, j), 0)),
            pl.BlockSpec((L, M_INNER), lambda b, j: (f(b, j), 1)),
            pl.BlockSpec((None, M_INNER, L), lambda b, j: (f(b, j), 0, 0)),
            pl.BlockSpec((None, GATE_ROWS, L), lambda b, j: (f(b, j), 0, 0)),
            pl.BlockSpec((L, M_INNER), lambda b, j: (f(b, j), 3)),
        ]

    state_specs = [
        pl.BlockSpec((None, 2, M_HEADS, M_DK, M_DV), lambda b, j: (b, 0, 0, 0, 0)),
        pl.BlockSpec((None, 2, M_HEADS, 1, M_DK), lambda b, j: (b, 0, 0, 0, 0)),
        pl.BlockSpec((None, 2, M_HEADS, 1, 1), lambda b, j: (b, 0, 0, 0, 0)),
    ]
    out_rows = DEC_BATCH * DEC_SEQ // 2
    return pl.pallas_call(
        functools.partial(_mlstm_sample_kernel, nc=nc),
        out_shape=[jax.ShapeDtypeStruct((out_rows, M_INNER), BF16)] * 2,
        grid=(DEC_BATCH, nc),
        in_specs=specs(fwd) + specs(bwd) + [pl.BlockSpec((1, M_INNER), lambda b, j: (0, 0))] + state_specs,
        out_specs=[
            pl.BlockSpec((L, M_INNER), lambda b, j: (b * half + jnp.maximum(j, half) - half, 0)),
            pl.BlockSpec((L, M_INNER), lambda b, j: (b * half + jnp.minimum(nc - 1 - j, half - 1), 0)),
        ],
        scratch_shapes=[
            pltpu.VMEM((nc, M_HEADS, M_DV, L), F32),
            pltpu.VMEM((2 * M_HEADS, M_DV, M_DK), F32),
            pltpu.VMEM((2 * M_HEADS, 1, M_DK), F32),
            pltpu.VMEM((2 * M_HEADS, 1, LANES), F32),
        ],
        compiler_params=_params(("arbitrary", "arbitrary")),
        name="mlstm",
    )(qkvo, qkvo, vt, gates_t, qkvo, qkvo, qkvo, vt, gates_t, qkvo, gain, *init)


def _mlstm_prompt_kernel(q_ref, k_ref, vt_ref, gt_ref, o_ref, gain_ref, out_ref, co_ref, no_ref, mo_ref,
                         *, nseq):
    H = M_HEADS
    L = L_CHUNK
    G = nseq * H
    q = jnp.concatenate([_heads(q_ref[b * L:(b + 1) * L, :]) for b in range(nseq)], axis=0)
    k = jnp.concatenate([_heads(k_ref[b * L:(b + 1) * L, :]) for b in range(nseq)], axis=0)
    vt = vt_ref[...].reshape(G, M_DV, L)
    gt3 = jnp.concatenate([gt_ref[b] for b in range(nseq)], axis=0).reshape(G, GATE_STRIDE, L)
    pre3 = jnp.concatenate([_mlstm_prefix(gt_ref[b]) for b in range(nseq)],
                           axis=0).reshape(G, GATE_STRIDE, L)
    gain = gain_ref[...]
    hs = None
    for d in range(2):
        h, (m_new, ct_new, n_new) = _mlstm_group(d == 1, q, k, vt, gt3, pre3, None)
        hs = h if hs is None else hs + h
        for b in range(nseq):
            for hh in range(H):
                co_ref[b, d, hh] = ct_new[b * H + hh].T
                no_ref[b, d, hh] = n_new[b * H + hh]
                mo_ref[b, d, hh] = m_new[b * H + hh]
    for b in range(nseq):
        out_ref[b * L:(b + 1) * L, :] = _mlstm_finish(hs[b * H:(b + 1) * H], o_ref[b * L:(b + 1) * L, :], gain)


def _mlstm_prompt(qkvo, vt, gates_t, gain):
    nseq = MLSTM_SEQS
    L = L_CHUNK
    rows = nseq * L
    state_specs = [
        pl.BlockSpec((nseq, 2, M_HEADS, M_DK, M_DV), lambda i: (i, 0, 0, 0, 0)),
        pl.BlockSpec((nseq, 2, M_HEADS, 1, M_DK), lambda i: (i, 0, 0, 0, 0)),
        pl.BlockSpec((nseq, 2, M_HEADS, 1, 1), lambda i: (i, 0, 0, 0, 0)),
    ]
    return pl.pallas_call(
        functools.partial(_mlstm_prompt_kernel, nseq=nseq),
        out_shape=[
            jax.ShapeDtypeStruct((NP, M_INNER), BF16),
            jax.ShapeDtypeStruct((BATCH, 2, M_HEADS, M_DK, M_DV), F32),
            jax.ShapeDtypeStruct((BATCH, 2, M_HEADS, 1, M_DK), F32),
            jax.ShapeDtypeStruct((BATCH, 2, M_HEADS, 1, 1), F32),
        ],
        grid=(BATCH // nseq,),
        in_specs=[
            pl.BlockSpec((rows, M_INNER), lambda i: (i, 0)),
            pl.BlockSpec((rows, M_INNER), lambda i: (i, 1)),
            pl.BlockSpec((nseq, M_INNER, L), lambda i: (i, 0, 0)),
            pl.BlockSpec((nseq, GATE_ROWS, L), lambda i: (i, 0, 0)),
            pl.BlockSpec((rows, M_INNER), lambda i: (i, 3)),
            pl.BlockSpec((1, M_INNER), lambda i: (0, 0)),
        ],
        out_specs=[pl.BlockSpec((rows, M_INNER), lambda i: (i, 0))] + state_specs,
        compiler_params=_params(("arbitrary",)),
        name="mlstm_state",
    )(qkvo, qkvo, vt, gates_t, qkvo, gain)


@jax.jit
def kernel(x_prompt, x_sample, cache_k, cache_v, state_c, state_n, state_m, c, c_ctx, w_ada, b_ada,
           ln_g, ln_b, w_ff1, w_ff2, w_in_a, q_gain, k_gain, conv_w, conv_b, conv_ln_g, conv_ln_b,
           w_out_a, w_in_m, b_gate_m, mh_gain, w_out_m):
    xp = x_prompt.reshape(NP, D_MODEL)
    xs = x_sample.reshape(NS, D_MODEL)

    cond = jnp.concatenate(
        [c_ctx[None, :], c, jnp.zeros((N_COND - 1 - DEC_BATCH, D_MODEL), F32)], axis=0)
    mods = _adaln(cond, w_ada, b_ada)

    ln = lambda l, s: (ln_g[l, s].reshape(1, D_MODEL), ln_b[l, s].reshape(1, D_MODEL))

    q, k, v, kf, vf, u = _inproj_a(xp, xs, mods[0], w_in_a, q_gain[0], k_gain[0])
    att_p = _attention_prompt(q, k, v)
    ctx_k = cache_k[:, 0].reshape(DEC_BATCH, PAST_LEN, KV_W)
    ctx_v = cache_v[:, 0].reshape(DEC_BATCH, PAST_LEN, KV_W)
    att_s = _attention_sample(q, k, v, ctx_k, ctx_v)
    tail_src = (att_p, att_s, u, xp, xs, conv_w[0], conv_b[0], conv_ln_g[0], conv_ln_b[0])
    x2 = _block_tail("a", tail_src, 0, NT, mods[0], w_out_a, w_ff1, w_ff2, 0, ln(0, 0), ln(0, 1))

    wm = w_in_m[0]
    src = jnp.array([4 * M_INNER + g * M_HEADS + h
                     for h in range(M_HEADS) for g in (0, 2, 1, 3)], jnp.int32)
    dst = jnp.array([h * GATE_STRIDE + t for h in range(M_HEADS) for t in range(4)], jnp.int32)
    wg = jnp.zeros((D_MODEL, LANES), F32).at[:, dst].set(wm[:, src]).astype(BF16)
    bg = jnp.zeros((1, LANES), F32).at[0, dst].set(b_gate_m[0][src - 4 * M_INNER])
    qkvo, gates_t, vt_m = _inproj_m(x2, mods[1], wm[:, :4 * M_INNER].astype(BF16), wg, bg)
    gain = mh_gain[0].reshape(1, M_INNER)
    hg_p, st_c, st_n, st_m = _mlstm_prompt(qkvo, vt_m, gates_t, gain)
    init = (state_c[:, 0], state_n[:, 0].reshape(DEC_BATCH, 2, M_HEADS, 1, M_DK),
            state_m[:, 0].reshape(DEC_BATCH, 2, M_HEADS, 1, 1))
    hg_hi, hg_lo = _mlstm_sample(qkvo, vt_m, gates_t, gain, init)
    tail_src = (hg_p, hg_lo, hg_hi, x2)
    y_p = _block_tail("m", tail_src, 0, NP, mods[1], w_out_m, w_ff1, w_ff2, 1, ln(1, 0), ln(1, 1))
    y_s = _block_tail("m", tail_src, NP, NS, mods[1], w_out_m, w_ff1, w_ff2, 1, ln(1, 0), ln(1, 1))

    new_k = jnp.transpose(kf, (0, 3, 1, 2))[:, None]
    new_v = jnp.transpose(vf, (0, 3, 1, 2))[:, None]
    return (y_p.reshape(BATCH, SEQ, D_MODEL), y_s.reshape(DEC_BATCH, DEC_SEQ, D_MODEL),
            new_k, new_v,
            st_c.reshape(BATCH, 1, 2, M_HEADS, M_DK, M_DV),
            st_n.reshape(BATCH, 1, 2, M_HEADS, M_DK),
            st_m.reshape(BATCH, 1, 2, M_HEADS))
```

```python
import functools

import jax
import jax.numpy as jnp
from jax import lax
from jax.experimental import pallas as pl
from jax.experimental.pallas import tpu as pltpu

F32 = jnp.float32
BF16 = jnp.bfloat16

D_MODEL = 1024
BATCH = 32
SEQ = 256
DEPTH = 2
DEC_BATCH = 4
DEC_SEQ = 4096
PAST_LEN = 512
GRID_W = 64
ATT_HEADS = 8
ATT_KV_HEADS = 2
HEAD_DIM = 64
ATT_GROUP = ATT_HEADS // ATT_KV_HEADS
ATT_W = ATT_HEADS * HEAD_DIM
KV_W = ATT_KV_HEADS * HEAD_DIM
ROPE_AXIS_DIM = HEAD_DIM // 2
ROPE_THETA = 10000.0
CONV_CH = D_MODEL // 2
CONV_K = 31
EVEN_IN = ATT_W + 2 * KV_W + 2 * CONV_CH
M_HEADS = 4
M_INNER = D_MODEL
M_DK = M_INNER // M_HEADS
M_DV = M_INNER // M_HEADS
D_FF = 4 * D_MODEL
ALPHA = (2 * DEPTH) ** 0.25
EPS = 1e-6
LOG2E = 1.4426950408889634

NP = BATCH * SEQ
NS = DEC_BATCH * DEC_SEQ
NT = NP + NS
N_COND = 8

LANES = 128
VMEM_LIMIT = 56 * 1024 * 1024

TM_PROJ = 512
TF_FFN = 1024
TQ_ATT = 512
TK_ATT = 512
R_CONV = 256
HALO = 16
L_CHUNK = 256
GATE_STRIDE = 8
ONES_ROWS = 16
GATE_ROWS = 32
MLSTM_SEQS = 4
TAIL_CAST_STEPS = 16
INPROJ_M_CAST_STEPS = 8
INPROJ_SUBTILES = 4
ATT_UNROLL = 9


def _cond_index(i_global, tm):
    npt = NP // tm
    tps = DEC_SEQ // tm
    return jnp.where(i_global < npt, 0, 1 + (i_global - npt) // tps)


def _layernorm(r, g, b):
    mu = jnp.mean(r, axis=-1, keepdims=True)
    d = r - mu
    var = jnp.mean(d * d, axis=-1, keepdims=True)
    return d * lax.rsqrt(var + EPS) * g + b


def _sigmoid(x):
    return 1.0 / (1.0 + jnp.exp(-x))


def _params(sem, vmem=VMEM_LIMIT, flags=None):
    return pltpu.CompilerParams(dimension_semantics=sem, vmem_limit_bytes=vmem, flags=flags)


def _adaln_kernel(cond_ref, w_ref, b_ref, o_ref):
    c = cond_ref[...]
    s = (c * _sigmoid(c)).astype(BF16)
    o_ref[...] = jnp.dot(s, w_ref[...].astype(BF16), preferred_element_type=F32) + b_ref[...]


def _adaln(cond, w_ada, b_ada):
    tn = 1536
    n = 6 * D_MODEL
    out = pl.pallas_call(
        _adaln_kernel,
        out_shape=jax.ShapeDtypeStruct((DEPTH, N_COND, n), F32),
        grid=(DEPTH, n // tn),
        in_specs=[
            pl.BlockSpec((N_COND, D_MODEL), lambda l, j: (0, 0)),
            pl.BlockSpec((None, D_MODEL, tn), lambda l, j: (l, 0, j)),
            pl.BlockSpec((None, 1, tn), lambda l, j: (l, 0, j)),
        ],
        out_specs=pl.BlockSpec((None, N_COND, tn), lambda l, j: (l, 0, j)),
        compiler_params=_params(("arbitrary", "arbitrary")),
        name="adaln",
    )(cond, w_ada, b_ada.reshape(DEPTH, 1, n))
    return out.reshape(DEPTH, N_COND, 6, D_MODEL)


def _cast_once(w_ref, w_scr):
    @pl.when(pl.program_id(0) == 0)
    def _():
        w_scr[...] = w_ref[...].astype(BF16)


def _inproj_a_kernel(xp_ref, xs_ref, mod_ref, w_ref, qg_ref, kg_ref, cos_ref, sin_ref,
                     q_ref, k_ref, v_ref, kf_ref, vf_ref, u_ref, w_scr, *, npt):
    i = pl.program_id(0)
    _cast_once(w_ref, w_scr)
    mod = mod_ref[...]
    tm = u_ref.shape[0]
    sub = tm // INPROJ_SUBTILES

    ri = lax.broadcasted_iota(jnp.int32, (LANES, LANES), 0) // HEAD_DIM
    ci = lax.broadcasted_iota(jnp.int32, (LANES, LANES), 1) // HEAD_DIM
    seg = jnp.where(ri == ci, 1.0, 0.0).astype(BF16)
    lane = lax.broadcasted_iota(jnp.int32, (sub, LANES), 1)
    even = (lane % 2) == 0
    qg = qg_ref[...]
    kg = kg_ref[...]

    def norm(xc, gain):
        ss = jnp.dot((xc * xc).astype(BF16), seg, preferred_element_type=F32)
        return xc * lax.rsqrt(ss * (1.0 / HEAD_DIM) + EPS) * gain

    cache_rows = []
    for r in range(INPROJ_SUBTILES):
        rows = slice(r * sub, (r + 1) * sub)
        x = jnp.where(i < npt, xp_ref[rows, :], xs_ref[rows, :])
        h = (x * (1.0 + mod[1:2]) + mod[0:1]).astype(BF16)
        proj = jnp.dot(h, w_scr[...], preferred_element_type=F32)
        cos = cos_ref[rows, :]
        sin = sin_ref[rows, :]

        def rope(xn):
            partner = jnp.where(even, pltpu.roll(xn, LANES - 1, 1), pltpu.roll(xn, 1, 1))
            return xn * cos + partner * sin

        for c in range(ATT_W // LANES):
            qr = rope(norm(proj[:, c * LANES:(c + 1) * LANES], qg))
            qs = (qr * (HEAD_DIM ** -0.5 * LOG2E)).astype(BF16)
            q_ref[2 * c, rows, :] = qs[:, :HEAD_DIM]
            q_ref[2 * c + 1, rows, :] = qs[:, HEAD_DIM:]

        kn = norm(proj[:, ATT_W:ATT_W + KV_W], kg)
        kr = rope(kn).astype(BF16)
        k_ref[0, rows, :] = kr[:, :HEAD_DIM]
        k_ref[1, rows, :] = kr[:, HEAD_DIM:]

        v = proj[:, ATT_W + KV_W:ATT_W + 2 * KV_W]
        cache_rows.append((kn, v))
        vb = v.astype(BF16)
        v_ref[0, rows, :] = vb[:, :HEAD_DIM]
        v_ref[1, rows, :] = vb[:, HEAD_DIM:]

        off = ATT_W + 2 * KV_W
        a = proj[:, off:off + CONV_CH]
        gt = proj[:, off + CONV_CH:off + 2 * CONV_CH]
        u_ref[rows, :] = a * _sigmoid(gt)

    @pl.when(i < npt)
    def _():
        for r, (kn, v) in enumerate(cache_rows):
            j, t0 = (r * sub) // SEQ, (r * sub) % SEQ
            kf_ref[j, :, :, t0:t0 + sub] = kn.T.reshape(ATT_KV_HEADS, HEAD_DIM, sub)
            vf_ref[j, :, :, t0:t0 + sub] = v.T.reshape(ATT_KV_HEADS, HEAD_DIM, sub)


def _rope_tables(tm):
    t = jnp.arange(DEC_SEQ)
    row = (t // GRID_W).astype(F32)
    col = (t % GRID_W).astype(F32)
    freqs = ROPE_THETA ** (-jnp.arange(0, ROPE_AXIS_DIM, 2, dtype=F32) / ROPE_AXIS_DIM)
    ang = jnp.concatenate([row[:, None] * freqs, col[:, None] * freqs], axis=-1)
    pair = (jnp.arange(LANES) % HEAD_DIM) // 2
    sign = jnp.where(jnp.arange(LANES) % 2 == 0, -1.0, 1.0).astype(F32)
    cos = jnp.cos(ang)[:, pair]
    sin = jnp.sin(ang)[:, pair] * sign
    cos = jnp.concatenate([jnp.ones((tm, LANES), F32), cos], axis=0)
    sin = jnp.concatenate([jnp.zeros((tm, LANES), F32), sin], axis=0)
    return cos, sin


def _inproj_a(xp, xs, mod, w, q_gain, k_gain):
    tm = TM_PROJ
    npt = NP // tm
    tps = DEC_SEQ // tm
    nt = NT // tm
    cos, sin = _rope_tables(tm)
    qg = jnp.tile(q_gain, LANES // HEAD_DIM).reshape(1, LANES)
    kg = jnp.tile(k_gain, LANES // HEAD_DIM).reshape(1, LANES)

    def rope_idx(i):
        return (jnp.where(i < npt, 0, 1 + (i - npt) % tps), 0)

    return pl.pallas_call(
        functools.partial(_inproj_a_kernel, npt=npt),
        out_shape=[
            jax.ShapeDtypeStruct((ATT_HEADS, NT, HEAD_DIM), BF16),
            jax.ShapeDtypeStruct((ATT_KV_HEADS, NT, HEAD_DIM), BF16),
            jax.ShapeDtypeStruct((ATT_KV_HEADS, NT, HEAD_DIM), BF16),
            jax.ShapeDtypeStruct((BATCH, ATT_KV_HEADS, HEAD_DIM, SEQ), F32),
            jax.ShapeDtypeStruct((BATCH, ATT_KV_HEADS, HEAD_DIM, SEQ), F32),
            jax.ShapeDtypeStruct((NT, CONV_CH), F32),
        ],
        grid=(nt,),
        in_specs=[
            pl.BlockSpec((tm, D_MODEL), lambda i: (jnp.minimum(i, npt - 1), 0)),
            pl.BlockSpec((tm, D_MODEL), lambda i: (jnp.maximum(i - npt, 0), 0)),
            pl.BlockSpec((None, 6, D_MODEL), lambda i: (_cond_index(i, tm), 0, 0)),
            pl.BlockSpec((None, D_MODEL, EVEN_IN), lambda i: (0, 0, 0)),
            pl.BlockSpec((1, LANES), lambda i: (0, 0)),
            pl.BlockSpec((1, LANES), lambda i: (0, 0)),
            pl.BlockSpec((tm, LANES), rope_idx),
            pl.BlockSpec((tm, LANES), rope_idx),
        ],
        out_specs=[
            pl.BlockSpec((ATT_HEADS, tm, HEAD_DIM), lambda i: (0, i, 0)),
            pl.BlockSpec((ATT_KV_HEADS, tm, HEAD_DIM), lambda i: (0, i, 0)),
            pl.BlockSpec((ATT_KV_HEADS, tm, HEAD_DIM), lambda i: (0, i, 0)),
            pl.BlockSpec((tm // SEQ, ATT_KV_HEADS, HEAD_DIM, SEQ), lambda i: (jnp.minimum(i, npt - 1), 0, 0, 0)),
            pl.BlockSpec((tm // SEQ, ATT_KV_HEADS, HEAD_DIM, SEQ), lambda i: (jnp.minimum(i, npt - 1), 0, 0, 0)),
            pl.BlockSpec((tm, CONV_CH), lambda i: (i, 0)),
        ],
        scratch_shapes=[pltpu.VMEM((D_MODEL, EVEN_IN), BF16)],
        compiler_params=_params(("arbitrary",)),
        name="inproj_a",
    )(xp, xs, mod, w, qg, kg, cos, sin)


def _attn_chunk(qs, kc, vc, m, acc):
    s = lax.dot_general(qs, kc, (((1,), (1,)), ((), ())), preferred_element_type=F32)
    fold = s[:, :LANES]
    for c in range(1, s.shape[1] // LANES):
        fold = jnp.maximum(fold, s[:, c * LANES:(c + 1) * LANES])
    m_new = jnp.maximum(m, jnp.max(fold, axis=1, keepdims=True))
    alpha = jnp.exp2(m - m_new)
    p = jnp.exp2(s - m_new).astype(BF16)
    va = jnp.concatenate([vc, jnp.ones((vc.shape[0], LANES - HEAD_DIM), BF16)], axis=1)
    acc = alpha * acc + jnp.dot(p, va, preferred_element_type=F32)
    return m_new, acc


def _attn_init(nq):
    return jnp.full((nq, 1), -1e30, F32), jnp.zeros((nq, LANES), F32)


def _attn_finish(acc, o_ref):
    tq = o_ref.shape[0]
    o = acc[:, :HEAD_DIM] * (1.0 / acc[:, HEAD_DIM:HEAD_DIM + 1])
    o_ref[...] = jnp.concatenate([o[h * tq:(h + 1) * tq] for h in range(ATT_GROUP)],
                                 axis=1).astype(BF16)


def _stack_heads(q_ref):
    return q_ref[...].reshape(ATT_GROUP * q_ref.shape[1], HEAD_DIM)


def _attn_prompt_kernel(q_ref, k_ref, v_ref, o_ref):
    qs = _stack_heads(q_ref)
    _, acc = _attn_chunk(qs, k_ref[...], v_ref[...], *_attn_init(qs.shape[0]))
    _attn_finish(acc, o_ref)


def _attn_sample_kernel(q_ref, k_ref, v_ref, ck_ref, cv_ref, o_ref, k_scr, v_scr, *, nchunks):
    @pl.when(pl.program_id(2) == 0)
    def _():
        own = k_ref.shape[0]
        k_scr[0:own, :] = k_ref[...]
        v_scr[0:own, :] = v_ref[...]
        first = pl.program_id(1) == 0
        ck = ck_ref[...]
        cv = cv_ref[...]
        k_scr[own:, :] = jnp.where(first, ck[:, :HEAD_DIM], ck[:, HEAD_DIM:]).astype(BF16)
        v_scr[own:, :] = jnp.where(first, cv[:, :HEAD_DIM], cv[:, HEAD_DIM:]).astype(BF16)

    qs = _stack_heads(q_ref)

    def body(j, carry):
        rows = pl.ds(pl.multiple_of(j * TK_ATT, TK_ATT), TK_ATT)
        return _attn_chunk(qs, k_scr[rows, :], v_scr[rows, :], *carry)

    _, acc = lax.fori_loop(0, nchunks, body, _attn_init(qs.shape[0]), unroll=ATT_UNROLL)
    _attn_finish(acc, o_ref)


def _attention_prompt(q, k, v):
    return pl.pallas_call(
        _attn_prompt_kernel,
        out_shape=jax.ShapeDtypeStruct((NP, ATT_W), BF16),
        grid=(BATCH, ATT_KV_HEADS),
        in_specs=[
            pl.BlockSpec((ATT_GROUP, SEQ, HEAD_DIM), lambda b, g: (g, b, 0)),
            pl.BlockSpec((None, SEQ, HEAD_DIM), lambda b, g: (g, b, 0)),
            pl.BlockSpec((None, SEQ, HEAD_DIM), lambda b, g: (g, b, 0)),
        ],
        out_specs=pl.BlockSpec((SEQ, ATT_GROUP * HEAD_DIM), lambda b, g: (b, g)),
        compiler_params=_params(("arbitrary", "arbitrary")),
        name="attn_prompt",
    )(q, k, v)


def _attention_sample(q, k, v, ctx_k, ctx_v):
    tq = TQ_ATT
    nkeys = DEC_SEQ + PAST_LEN
    nchunks = nkeys // TK_ATT
    q_off = NP // tq
    kv_off = NP // DEC_SEQ
    return pl.pallas_call(
        functools.partial(_attn_sample_kernel, nchunks=nchunks),
        out_shape=jax.ShapeDtypeStruct((NS, ATT_W), BF16),
        grid=(DEC_BATCH, ATT_KV_HEADS, DEC_SEQ // tq),
        in_specs=[
            pl.BlockSpec((ATT_GROUP, tq, HEAD_DIM),
                         lambda b, g, i: (g, q_off + b * (DEC_SEQ // tq) + i, 0)),
            pl.BlockSpec((None, DEC_SEQ, HEAD_DIM), lambda b, g, i: (g, kv_off + b, 0)),
            pl.BlockSpec((None, DEC_SEQ, HEAD_DIM), lambda b, g, i: (g, kv_off + b, 0)),
            pl.BlockSpec((None, PAST_LEN, KV_W), lambda b, g, i: (b, 0, 0)),
            pl.BlockSpec((None, PAST_LEN, KV_W), lambda b, g, i: (b, 0, 0)),
        ],
        out_specs=pl.BlockSpec((tq, ATT_GROUP * HEAD_DIM),
                               lambda b, g, i: (b * (DEC_SEQ // tq) + i, g)),
        scratch_shapes=[pltpu.VMEM((nkeys, HEAD_DIM), BF16), pltpu.VMEM((nkeys, HEAD_DIM), BF16)],
        compiler_params=_params(("arbitrary", "arbitrary", "arbitrary")),
        name="attn_sample",
    )(q, k, v, ctx_k, ctx_v)


def _conv_window(left, cur, right, w_ref, b_ref, g_ref, bb_ref, win_ref, y_ref, out_ref):
    r = cur.shape[0]
    win_ref[0:HALO, :] = left
    win_ref[HALO:HALO + r, :] = cur
    win_ref[HALO + r:2 * HALO + r, :] = right
    base = HALO - CONV_K // 2
    sub = 8
    for c in range(CONV_CH // LANES):
        cs = slice(c * LANES, (c + 1) * LANES)
        acc = None
        for res in range(sub):
            p = None
            for k in range(CONV_K):
                if (base + k) % sub != res:
                    continue
                a = (base + k) - res
                term = win_ref[a:a + r + sub, cs] * w_ref[k:k + 1, cs]
                p = term if p is None else p + term
            if p is None:
                continue
            p = p[res:res + r]
            acc = p if acc is None else acc + p
        y_ref[:, cs] = acc + b_ref[:, cs]
    y = _layernorm(y_ref[...], g_ref[...], bb_ref[...])
    out_ref[...] = (y * _sigmoid(y)).astype(BF16)


def _tail_kernel(*refs, mode, ncast, npt, tps, off):
    nsrc = 11 if mode == "a" else 4
    src = refs[:nsrc]
    (mod_ref, wo_ref, w1_ref, w2_ref, g0_ref, b0_ref, g1_ref, b1_ref, o_ref,
     wo_scr, w1_scr, w2_scr) = refs[nsrc:nsrc + 12]
    s = pl.program_id(0)
    per = ncast // w1_scr.shape[0]
    wcol = w1_ref.shape[1]

    @pl.when(s < ncast)
    def _():
        wo_scr[s] = wo_ref[...].astype(BF16)
        w2_scr[s] = w2_ref[...].astype(BF16)

    for q in range(per):
        @pl.when(jnp.logical_and(s < ncast, s % per == q))
        def _():
            w1_scr[s // per, :, q * wcol:(q + 1) * wcol] = w1_ref[...].astype(BF16)

    def conv_half(r, tile):
        ul_ref, u_ref, ur_ref, cw_ref, cb_ref, cg_ref, cbb_ref = src[2:5] + src[7:11]
        win_scr, y_scr, uc_scr = refs[nsrc + 12:]
        ctx = tile < npt
        t = (tile - npt) % tps
        half = u_ref.shape[0] // 2
        starts = jnp.logical_or(ctx, jnp.logical_and(t == 0, r == 0))
        ends = jnp.logical_or(ctx, jnp.logical_and(t == tps - 1, r == 1))
        left = ul_ref[...] if r == 0 else u_ref[half - HALO:half, :]
        right = u_ref[half:half + HALO, :] if r == 0 else ur_ref[...]
        _conv_window(jnp.where(starts, 0.0, left), u_ref[r * half:(r + 1) * half, :],
                     jnp.where(ends, 0.0, right), cw_ref, cb_ref, cg_ref, cbb_ref,
                     win_scr.at[r], y_scr.at[r], uc_scr.at[r])
        return uc_scr[r]

    @pl.when(s >= ncast)
    def _():
        tile = s - ncast + off
        ctx = tile < npt
        mod = mod_ref[...]
        w_out = wo_scr[...].reshape(D_MODEL, D_MODEL)
        nchunk = w1_scr.shape[0]
        rows_per = w2_scr.shape[0] // nchunk
        half = o_ref.shape[0] // 2
        for r in range(2):
            rows = slice(r * half, (r + 1) * half)
            if mode == "a":
                ap_ref, as_ref, xp_ref, xs_ref = src[0], src[1], src[5], src[6]
                att = jnp.where(ctx, ap_ref[rows, :], as_ref[rows, :])
                x = jnp.where(ctx, xp_ref[rows, :], xs_ref[rows, :])
                y = jnp.dot(att, w_out[0:ATT_W, :], preferred_element_type=F32)
                y = y + jnp.dot(conv_half(r, tile), w_out[ATT_W:, :], preferred_element_type=F32)
            else:
                hp_ref, lo_ref, hi_ref, x_ref = src
                upper = ((tile - npt) % tps) >= tps // 2
                hg = jnp.where(ctx, hp_ref[rows, :], jnp.where(upper, hi_ref[rows, :], lo_ref[rows, :]))
                x = x_ref[rows, :]
                y = jnp.dot(hg, w_out, preferred_element_type=F32)
            x1 = _layernorm(ALPHA * x + mod[2:3] * y, g0_ref[...], b0_ref[...])
            h = (x1 * (1.0 + mod[4:5]) + mod[3:4]).astype(BF16)
            acc = None
            for c in range(nchunk):
                a = jnp.maximum(jnp.dot(h, w1_scr[c], preferred_element_type=F32), 0.0)
                w2 = w2_scr[c * rows_per:(c + 1) * rows_per].reshape(TF_FFN, D_MODEL)
                t = jnp.dot((a * a).astype(BF16), w2, preferred_element_type=F32)
                acc = t if acc is None else acc + t
            o_ref[rows, :] = _layernorm(ALPHA * x1 + mod[5:6] * acc, g1_ref[...], b1_ref[...])


def _block_tail(mode, src, row0, nrows, mod, w_out, w1, w2, layer, ln0, ln1):
    tm = TM_PROJ
    ncast = TAIL_CAST_STEPS
    npt = NP // tm
    tps = DEC_SEQ // tm
    hps = tps // 2
    off = row0 // tm
    ntiles = nrows // tm
    tile = lambda s: jnp.maximum(s - ncast, 0) + off
    blk = lambda s: jnp.minimum(s, ncast - 1)
    first = lambda s: (jnp.minimum(tile(s), npt - 1), 0)
    second = lambda s: (jnp.maximum(tile(s) - npt, 0), 0)

    def half_idx(s, upper):
        t0 = jnp.maximum(tile(s) - npt, 0)
        t = t0 % tps
        t = jnp.maximum(t - hps, 0) if upper else jnp.minimum(t, hps - 1)
        return ((t0 // tps) * hps + t, 0)

    scratch = []
    if mode == "a":
        assert tm == 2 * R_CONV
        att_p, att_s, u, xp, xs, conv_w, conv_b, cln_g, cln_b = src
        hb = tm // HALO
        nh = NT // HALO
        cvec = pl.BlockSpec((1, CONV_CH), lambda s: (0, 0))
        src = (att_p, att_s, u, u, u, xp, xs,
               jnp.concatenate([conv_w, jnp.zeros((1, CONV_CH), F32)], axis=0),
               conv_b.reshape(1, CONV_CH), cln_g.reshape(1, CONV_CH), cln_b.reshape(1, CONV_CH))
        src_specs = [
            pl.BlockSpec((tm, ATT_W), first),
            pl.BlockSpec((tm, ATT_W), second),
            pl.BlockSpec((HALO, CONV_CH), lambda s: (jnp.maximum(tile(s) * hb - 1, 0), 0)),
            pl.BlockSpec((tm, CONV_CH), lambda s: (tile(s), 0)),
            pl.BlockSpec((HALO, CONV_CH), lambda s: (jnp.minimum((tile(s) + 1) * hb, nh - 1), 0)),
            pl.BlockSpec((tm, D_MODEL), first),
            pl.BlockSpec((tm, D_MODEL), second),
            pl.BlockSpec((CONV_K + 1, CONV_CH), lambda s: (0, 0)),
            cvec, cvec, cvec,
        ]
        scratch = [pltpu.VMEM((2, R_CONV + 2 * HALO, CONV_CH), F32), pltpu.VMEM((2, R_CONV, CONV_CH), F32),
                   pltpu.VMEM((2, R_CONV, CONV_CH), BF16)]
    else:
        src_specs = [
            pl.BlockSpec((tm, M_INNER), first),
            pl.BlockSpec((tm, M_INNER), lambda s: half_idx(s, False)),
            pl.BlockSpec((tm, M_INNER), lambda s: half_idx(s, True)),
            pl.BlockSpec((tm, D_MODEL), lambda s: (tile(s), 0)),
        ]
    nchunk = D_FF // TF_FFN
    vec = pl.BlockSpec((1, D_MODEL), lambda s: (0, 0))
    return pl.pallas_call(
        functools.partial(_tail_kernel, mode=mode, ncast=ncast, npt=npt, tps=tps, off=off),
        out_shape=jax.ShapeDtypeStruct((nrows, D_MODEL), F32),
        grid=(ncast + ntiles,),
        in_specs=src_specs + [
            pl.BlockSpec((None, 6, D_MODEL), lambda s: (_cond_index(tile(s), tm), 0, 0)),
            pl.BlockSpec((None, D_MODEL // ncast, D_MODEL), lambda s: (0, blk(s), 0)),
            pl.BlockSpec((None, D_MODEL, D_FF // ncast), lambda s: (layer, 0, blk(s))),
            pl.BlockSpec((None, D_FF // ncast, D_MODEL), lambda s: (layer, blk(s), 0)),
            vec, vec, vec, vec,
        ],
        out_specs=pl.BlockSpec((tm, D_MODEL), lambda s: (jnp.maximum(s - ncast, 0), 0)),
        scratch_shapes=[
            pltpu.VMEM((ncast, D_MODEL // ncast, D_MODEL), BF16),
            pltpu.VMEM((nchunk, D_MODEL, TF_FFN), BF16),
            pltpu.VMEM((ncast, D_FF // ncast, D_MODEL), BF16),
        ] + scratch,
        compiler_params=_params(("arbitrary",)),
        name="tail_" + mode,
    )(*src, mod, w_out, w1, w2, *ln0, *ln1)


def _inproj_m_kernel(x_ref, mod_ref, w_ref, wg_ref, bg_ref, o_ref, gt_ref, vt_ref, w_scr, *, ncast):
    step = pl.program_id(0)
    wcol = w_ref.shape[1]
    for c in range(ncast):
        @pl.when(step == c)
        def _():
            w_scr[:, c * wcol:(c + 1) * wcol] = w_ref[...].astype(BF16)

    @pl.when(step >= ncast)
    def _():
        nslab = vt_ref.shape[0]
        L = L_CHUNK
        mod = mod_ref[...]
        h = (x_ref[...] * (1.0 + mod[1:2]) + mod[0:1]).astype(BF16)

        gates = jnp.dot(h, wg_ref[...], preferred_element_type=F32) + bg_ref[...]
        kind = lax.broadcasted_iota(jnp.int32, gates.shape, 1) % GATE_STRIDE
        gates = jnp.where((kind == 2) | (kind == 3), _log_sigmoid(gates), gates)
        gates_t = gates.T
        for s in range(nslab):
            gt_ref[s] = gates_t[0:GATE_ROWS, s * L:(s + 1) * L]

        for j in range(4):
            cols = slice(j * M_INNER, (j + 1) * M_INNER)
            p = jnp.dot(h, w_scr[:, cols], preferred_element_type=F32)
            if j == 1:
                p = p * (M_DK ** -0.5)
            o_ref[:, cols] = p.astype(BF16)
            if j == 2:
                v_t = p.T.astype(BF16)
                for s in range(nslab):
                    vt_ref[s] = v_t[:, s * L:(s + 1) * L]


def _inproj_m(x, mod, w, wg, bg):
    tm = TM_PROJ
    nslab = tm // L_CHUNK
    ncast = INPROJ_M_CAST_STEPS
    tile = lambda i: jnp.maximum(i - ncast, 0)
    return pl.pallas_call(
        functools.partial(_inproj_m_kernel, ncast=ncast),
        out_shape=[
            jax.ShapeDtypeStruct((NT, 4 * M_INNER), BF16),
            jax.ShapeDtypeStruct((NT // L_CHUNK, GATE_ROWS, L_CHUNK), F32),
            jax.ShapeDtypeStruct((NT // L_CHUNK, M_INNER, L_CHUNK), BF16),
        ],
        grid=(ncast + NT // tm,),
        in_specs=[
            pl.BlockSpec((tm, D_MODEL), lambda i: (tile(i), 0)),
            pl.BlockSpec((None, 6, D_MODEL), lambda i: (_cond_index(tile(i), tm), 0, 0)),
            pl.BlockSpec((None, D_MODEL, 4 * M_INNER // ncast),
                         lambda i: (0, 0, jnp.minimum(i, ncast - 1))),
            pl.BlockSpec((D_MODEL, LANES), lambda i: (0, 0)),
            pl.BlockSpec((1, LANES), lambda i: (0, 0)),
        ],
        out_specs=[
            pl.BlockSpec((tm, 4 * M_INNER), lambda i: (tile(i), 0)),
            pl.BlockSpec((nslab, GATE_ROWS, L_CHUNK), lambda i: (tile(i), 0, 0)),
            pl.BlockSpec((nslab, M_INNER, L_CHUNK), lambda i: (tile(i), 0, 0)),
        ],
        scratch_shapes=[pltpu.VMEM((D_MODEL, 4 * M_INNER), BF16)],
        compiler_params=_params(("arbitrary",)),
        name="inproj_m",
    )(x, mod, w, wg, bg)


def _log_sigmoid(x):
    return jnp.minimum(x, 0.0) - jnp.log1p(jnp.exp(-jnp.abs(x)))


def _split3(x):
    x1 = x.astype(BF16)
    r1 = x - x1.astype(F32)
    x2 = r1.astype(BF16)
    x3 = (r1 - x2.astype(F32)).astype(BF16)
    return x1, x2, x3


def _mlstm_prefix(gt):
    L = gt.shape[1]
    parts = jnp.concatenate([t.astype(F32) for t in _split3(gt)]
                            + [jnp.zeros((GATE_ROWS, L), F32)], axis=0).astype(BF16)
    upper = jnp.where(lax.broadcasted_iota(jnp.int32, (L, L), 0)
                      <= lax.broadcasted_iota(jnp.int32, (L, L), 1), 1.0, 0.0).astype(BF16)
    r = jnp.dot(parts, upper, preferred_element_type=F32)
    return r[0:GATE_ROWS] + r[GATE_ROWS:2 * GATE_ROWS] + r[2 * GATE_ROWS:3 * GATE_ROWS]


def _rows3(x, n):
    g = x.shape[0]
    return jnp.concatenate([t.astype(F32) for t in _split3(x)]
                           + [jnp.zeros((g, ONES_ROWS - 3, n), F32)], axis=1).astype(BF16)


def _sum3(x, r0):
    return x[:, r0:r0 + 1] + x[:, r0 + 1:r0 + 2] + x[:, r0 + 2:r0 + 3]


def _mlstm_group(rev, q, k, vt, gt, pre, state):
    G, L = q.shape[0], q.shape[1]
    d = 1 if rev else 0
    tpos = lax.broadcasted_iota(jnp.int32, (L, L), 0)
    spos = lax.broadcasted_iota(jnp.int32, (L, L), 1)
    keep = (tpos >= spos) if rev else (tpos <= spos)
    if rev:
        tot = pre[:, 3:4, L - 1:L]
        a_row = tot - pre[:, 3:4, :] + gt[:, 3:4, :]
    else:
        a_row = pre[:, 2:3, :]
        tot = a_row[:, :, L - 1:L]
    b_row = gt[:, d:d + 1, :] - a_row
    m = jnp.zeros((G, 1, 1), F32) if state is None else state[0]

    gp = -(-G // 8) * 8
    rows = [b_row[p] for p in range(G)] + [jnp.zeros((gp - G, L), F32)] * (1 if gp > G else 0)
    bt = jnp.concatenate(rows, axis=0).T
    bm = jnp.stack([jnp.where(keep, bt[:, p:p + 1], -jnp.inf) for p in range(G)])
    mm = jnp.maximum(m, jnp.max(bm, axis=1, keepdims=True))
    w = jnp.exp(bm - mm)
    s = jnp.einsum('gsd,gtd->gst', k, q, preferred_element_type=F32)
    qk = s * w
    num = jnp.einsum('gvs,gst->gvt', vt, qk.astype(BF16), preferred_element_type=F32)
    den = jnp.sum(qk, axis=1, keepdims=True)
    if state is not None:
        _, ct, n = state
        cn = jnp.concatenate([ct.astype(BF16), _rows3(n, n.shape[2])], axis=1)
        inter = jnp.einsum('gcd,gtd->gct', cn, q, preferred_element_type=F32)
        s_inter = jnp.exp(m - mm)
        num = num + s_inter * inter[:, :M_DV]
        den = den + s_inter * _sum3(inter, M_DV)
    h = num * (1.0 / jnp.maximum(jnp.abs(den), jnp.exp(-(a_row + mm))))

    g_row = tot + b_row
    m_new = jnp.maximum(tot + m, jnp.max(g_row, axis=2, keepdims=True))
    ws = jnp.exp(g_row - m_new)
    lhs = jnp.concatenate([(vt.astype(F32) * ws).astype(BF16), _rows3(ws, L)], axis=1)
    upd = jnp.einsum('gcs,gsd->gcd', lhs, k, preferred_element_type=F32)
    ct_new = upd[:, :M_DV]
    n_new = _sum3(upd, M_DV)
    if state is not None:
        decay = jnp.exp(tot + m - m_new)
        ct_new = decay * ct + ct_new
        n_new = decay * n + n_new
    return h, (m_new, ct_new, n_new)


def _heads(x):
    w = x.shape[1] // M_HEADS
    return jnp.stack([x[:, h * w:(h + 1) * w] for h in range(M_HEADS)])


def _mlstm_finish(ht, o, gain):
    ms = jnp.mean(ht * ht, axis=1, keepdims=True)
    hn = ht * lax.rsqrt(ms + EPS)
    hn = jnp.concatenate([hn[h].T for h in range(M_HEADS)], axis=1)
    return (_sigmoid(o.astype(F32)) * hn * gain).astype(BF16)


def _mlstm_sample_kernel(qf_ref, kf_ref, vtf_ref, gtf_ref, of_ref, qb_ref, kb_ref, vtb_ref, gtb_ref,
                         ob_ref, gain_ref, c0_ref, n0_ref, m0_ref, hi_ref, lo_ref,
                         hs_ref, ct_scr, n_scr, m_scr, *, nc):
    j = pl.program_id(1)
    jb = nc - 1 - j
    H = M_HEADS
    L = L_CHUNK

    @pl.when(j == 0)
    def _():
        for d in range(2):
            for h in range(H):
                ct_scr[d * H + h] = c0_ref[d, h].T
                n_scr[d * H + h] = n0_ref[d, h]
                m_scr[d * H + h] = jnp.broadcast_to(m0_ref[d, h], (1, LANES))

    def run(rev, q_ref, k_ref, vt_ref, gt_ref, lo):
        gt = gt_ref[...]
        pre = _mlstm_prefix(gt).reshape(H, GATE_STRIDE, L)
        state = (m_scr[lo:lo + H][:, :, 0:1], ct_scr[lo:lo + H], n_scr[lo:lo + H])
        h, (m_new, ct_new, n_new) = _mlstm_group(
            rev, _heads(q_ref[...]), _heads(k_ref[...]), vt_ref[...].reshape(H, M_DV, L),
            gt.reshape(H, GATE_STRIDE, L), pre, state)
        ct_scr[lo:lo + H] = ct_new
        n_scr[lo:lo + H] = n_new
        m_scr[lo:lo + H] = jnp.broadcast_to(m_new, (H, 1, LANES))
        return h

    h_f = run(False, qf_ref, kf_ref, vtf_ref, gtf_ref, 0)
    h_b = run(True, qb_ref, kb_ref, vtb_ref, gtb_ref, H)

    @pl.when(j < nc // 2)
    def _():
        hs_ref[j] = h_f
        hs_ref[jb] = h_b

    @pl.when(j >= nc // 2)
    def _():
        gain = gain_ref[...]
        hi_ref[...] = _mlstm_finish(hs_ref[j] + h_f, of_ref[...], gain)
        lo_ref[...] = _mlstm_finish(hs_ref[jb] + h_b, ob_ref[...], gain)


def _mlstm_sample(qkvo, vt, gates_t, gain, init):
    nc = DEC_SEQ // L_CHUNK
    half = nc // 2
    c0 = NP // L_CHUNK
    L = L_CHUNK
    fwd = lambda b, j: c0 + b * nc + j
    bwd = lambda b, j: c0 + b * nc + nc - 1 - j

    def specs(f):
        return [
            pl.BlockSpec((L, M_INNER), lambda b, j: (f(b, j), 0)),
            pl.BlockSpec((L, M_INNER), lambda b, j: (f(b, j), 1)),
            pl.BlockSpec((None, M_INNER, L), lambda b, j: (f(b, j), 0, 0)),
            pl.BlockSpec((None, GATE_ROWS, L), lambda b, j: (f(b, j), 0, 0)),
            pl.BlockSpec((L, M_INNER), lambda b, j: (f(b, j), 3)),
        ]

    state_specs = [
        pl.BlockSpec((None, 2, M_HEADS, M_DK, M_DV), lambda b, j: (b, 0, 0, 0, 0)),
        pl.BlockSpec((None, 2, M_HEADS, 1, M_DK), lambda b, j: (b, 0, 0, 0, 0)),
        pl.BlockSpec((None, 2, M_HEADS, 1, 1), lambda b, j: (b, 0, 0, 0, 0)),
    ]
    out_rows = DEC_BATCH * DEC_SEQ // 2
    return pl.pallas_call(
        functools.partial(_mlstm_sample_kernel, nc=nc),
        out_shape=[jax.ShapeDtypeStruct((out_rows, M_INNER), BF16)] * 2,
        grid=(DEC_BATCH, nc),
        in_specs=specs(fwd) + specs(bwd) + [pl.BlockSpec((1, M_INNER), lambda b, j: (0, 0))] + state_specs,
        out_specs=[
            pl.BlockSpec((L, M_INNER), lambda b, j: (b * half + jnp.maximum(j, half) - half, 0)),
            pl.BlockSpec((L, M_INNER), lambda b, j: (b * half + jnp.minimum(nc - 1 - j, half - 1), 0)),
        ],
        scratch_shapes=[
            pltpu.VMEM((nc, M_HEADS, M_DV, L), F32),
            pltpu.VMEM((2 * M_HEADS, M_DV, M_DK), F32),
            pltpu.VMEM((2 * M_HEADS, 1, M_DK), F32),
            pltpu.VMEM((2 * M_HEADS, 1, LANES), F32),
        ],
        compiler_params=_params(("arbitrary", "arbitrary")),
        name="mlstm",
    )(qkvo, qkvo, vt, gates_t, qkvo, qkvo, qkvo, vt, gates_t, qkvo, gain, *init)


def _mlstm_prompt_kernel(q_ref, k_ref, vt_ref, gt_ref, o_ref, gain_ref, out_ref, co_ref, no_ref, mo_ref,
                         *, nseq):
    H = M_HEADS
    L = L_CHUNK
    G = nseq * H
    q = jnp.concatenate([_heads(q_ref[b * L:(b + 1) * L, :]) for b in range(nseq)], axis=0)
    k = jnp.concatenate([_heads(k_ref[b * L:(b + 1) * L, :]) for b in range(nseq)], axis=0)
    vt = vt_ref[...].reshape(G, M_DV, L)
    gt3 = jnp.concatenate([gt_ref[b] for b in range(nseq)], axis=0).reshape(G, GATE_STRIDE, L)
    pre3 = jnp.concatenate([_mlstm_prefix(gt_ref[b]) for b in range(nseq)],
                           axis=0).reshape(G, GATE_STRIDE, L)
    gain = gain_ref[...]
    hs = None
    for d in range(2):
        h, (m_new, ct_new, n_new) = _mlstm_group(d == 1, q, k, vt, gt3, pre3, None)
        hs = h if hs is None else hs + h
        for b in range(nseq):
            for hh in range(H):
                co_ref[b, d, hh] = ct_new[b * H + hh].T
                no_ref[b, d, hh] = n_new[b * H + hh]
                mo_ref[b, d, hh] = m_new[b * H + hh]
    for b in range(nseq):
        out_ref[b * L:(b + 1) * L, :] = _mlstm_finish(hs[b * H:(b + 1) * H], o_ref[b * L:(b + 1) * L, :], gain)


def _mlstm_prompt(qkvo, vt, gates_t, gain):
    nseq = MLSTM_SEQS
    L = L_CHUNK
    rows = nseq * L
    state_specs = [
        pl.BlockSpec((nseq, 2, M_HEADS, M_DK, M_DV), lambda i: (i, 0, 0, 0, 0)),
        pl.BlockSpec((nseq, 2, M_HEADS, 1, M_DK), lambda i: (i, 0, 0, 0, 0)),
        pl.BlockSpec((nseq, 2, M_HEADS, 1, 1), lambda i: (i, 0, 0, 0, 0)),
    ]
    return pl.pallas_call(
        functools.partial(_mlstm_prompt_kernel, nseq=nseq),
        out_shape=[
            jax.ShapeDtypeStruct((NP, M_INNER), BF16),
            jax.ShapeDtypeStruct((BATCH, 2, M_HEADS, M_DK, M_DV), F32),
            jax.ShapeDtypeStruct((BATCH, 2, M_HEADS, 1, M_DK), F32),
            jax.ShapeDtypeStruct((BATCH, 2, M_HEADS, 1, 1), F32),
        ],
        grid=(BATCH // nseq,),
        in_specs=[
            pl.BlockSpec((rows, M_INNER), lambda i: (i, 0)),
            pl.BlockSpec((rows, M_INNER), lambda i: (i, 1)),
            pl.BlockSpec((nseq, M_INNER, L), lambda i: (i, 0, 0)),
            pl.BlockSpec((nseq, GATE_ROWS, L), lambda i: (i, 0, 0)),
            pl.BlockSpec((rows, M_INNER), lambda i: (i, 3)),
            pl.BlockSpec((1, M_INNER), lambda i: (0, 0)),
        ],
        out_specs=[pl.BlockSpec((rows, M_INNER), lambda i: (i, 0))] + state_specs,
        compiler_params=_params(("arbitrary",)),
        name="mlstm_state",
    )(qkvo, qkvo, vt, gates_t, qkvo, gain)


@jax.jit
def kernel(x_prompt, x_sample, cache_k, cache_v, state_c, state_n, state_m, c, c_ctx, w_ada, b_ada,
           ln_g, ln_b, w_ff1, w_ff2, w_in_a, q_gain, k_gain, conv_w, conv_b, conv_ln_g, conv_ln_b,
           w_out_a, w_in_m, b_gate_m, mh_gain, w_out_m):
    xp = x_prompt.reshape(NP, D_MODEL)
    xs = x_sample.reshape(NS, D_MODEL)

    cond = jnp.concatenate(
        [c_ctx[None, :], c, jnp.zeros((N_COND - 1 - DEC_BATCH, D_MODEL), F32)], axis=0)
    mods = _adaln(cond, w_ada, b_ada)

    ln = lambda l, s: (ln_g[l, s].reshape(1, D_MODEL), ln_b[l, s].reshape(1, D_MODEL))

    q, k, v, kf, vf, u = _inproj_a(xp, xs, mods[0], w_in_a, q_gain[0], k_gain[0])
    att_p = _attention_prompt(q, k, v)
    ctx_k = cache_k[:, 0].reshape(DEC_BATCH, PAST_LEN, KV_W)
    ctx_v = cache_v[:, 0].reshape(DEC_BATCH, PAST_LEN, KV_W)
    att_s = _attention_sample(q, k, v, ctx_k, ctx_v)
    tail_src = (att_p, att_s, u, xp, xs, conv_w[0], conv_b[0], conv_ln_g[0], conv_ln_b[0])
    x2 = _block_tail("a", tail_src, 0, NT, mods[0], w_out_a, w_ff1, w_ff2, 0, ln(0, 0), ln(0, 1))

    wm = w_in_m[0]
    src = jnp.array([4 * M_INNER + g * M_HEADS + h
                     for h in range(M_HEADS) for g in (0, 2, 1, 3)], jnp.int32)
    dst = jnp.array([h * GATE_STRIDE + t for h in range(M_HEADS) for t in range(4)], jnp.int32)
    wg = jnp.zeros((D_MODEL, LANES), F32).at[:, dst].set(wm[:, src]).astype(BF16)
    bg = jnp.zeros((1, LANES), F32).at[0, dst].set(b_gate_m[0][src - 4 * M_INNER])
    qkvo, gates_t, vt_m = _inproj_m(x2, mods[1], w_in_m, wg, bg)
    gain = mh_gain[0].reshape(1, M_INNER)
    hg_p, st_c, st_n, st_m = _mlstm_prompt(qkvo, vt_m, gates_t, gain)
    init = (state_c[:, 0], state_n[:, 0].reshape(DEC_BATCH, 2, M_HEADS, 1, M_DK),
            state_m[:, 0].reshape(DEC_BATCH, 2, M_HEADS, 1, 1))
    hg_hi, hg_lo = _mlstm_sample(qkvo, vt_m, gates_t, gain, init)
    tail_src = (hg_p, hg_lo, hg_hi, x2)
    y_p = _block_tail("m", tail_src, 0, NP, mods[1], w_out_m, w_ff1, w_ff2, 1, ln(1, 0), ln(1, 1))
    y_s = _block_tail("m", tail_src, NP, NS, mods[1], w_out_m, w_ff1, w_ff2, 1, ln(1, 0), ln(1, 1))

    new_k = jnp.transpose(kf, (0, 3, 1, 2))[:, None]
    new_v = jnp.transpose(vf, (0, 3, 1, 2))[:, None]
    return (y_p.reshape(BATCH, SEQ, D_MODEL), y_s.reshape(DEC_BATCH, DEC_SEQ, D_MODEL),
            new_k, new_v,
            st_c.reshape(BATCH, 1, 2, M_HEADS, M_DK, M_DV),
            st_n.reshape(BATCH, 1, 2, M_HEADS, M_DK),
            st_m.reshape(BATCH, 1, 2, M_HEADS))
```

```python
import functools

import jax
import jax.numpy as jnp
from jax import lax
from jax.experimental import pallas as pl
from jax.experimental.pallas import tpu as pltpu

F32 = jnp.float32
BF16 = jnp.bfloat16

D_MODEL = 1024
BATCH = 32
SEQ = 256
DEPTH = 2
DEC_BATCH = 4
DEC_SEQ = 4096
PAST_LEN = 512
GRID_W = 64
ATT_HEADS = 8
ATT_KV_HEADS = 2
HEAD_DIM = 64
ATT_GROUP = ATT_HEADS // ATT_KV_HEADS
ATT_W = ATT_HEADS * HEAD_DIM
KV_W = ATT_KV_HEADS * HEAD_DIM
ROPE_AXIS_DIM = HEAD_DIM // 2
ROPE_THETA = 10000.0
CONV_CH = D_MODEL // 2
CONV_K = 31
EVEN_IN = ATT_W + 2 * KV_W + 2 * CONV_CH
M_HEADS = 4
M_INNER = D_MODEL
M_DK = M_INNER // M_HEADS
M_DV = M_INNER // M_HEADS
D_FF = 4 * D_MODEL
ALPHA = (2 * DEPTH) ** 0.25
EPS = 1e-6
LOG2E = 1.4426950408889634

NP = BATCH * SEQ
NS = DEC_BATCH * DEC_SEQ
NT = NP + NS
N_COND = 8

LANES = 128
VMEM_LIMIT = 56 * 1024 * 1024

TM_PROJ = 512
TF_FFN = 1024
TQ_ATT = 512
TK_ATT = 512
R_CONV = 256
HALO = 16
L_CHUNK = 256
GATE_STRIDE = 8
ONES_ROWS = 16
GATE_ROWS = 32
MLSTM_SEQS = 4
TAIL_CAST_STEPS = 16
INPROJ_SUBTILES = 4
ATT_UNROLL = 9


def _cond_index(i_global, tm):
    npt = NP // tm
    tps = DEC_SEQ // tm
    return jnp.where(i_global < npt, 0, 1 + (i_global - npt) // tps)


def _layernorm(r, g, b):
    mu = jnp.mean(r, axis=-1, keepdims=True)
    d = r - mu
    var = jnp.mean(d * d, axis=-1, keepdims=True)
    return d * lax.rsqrt(var + EPS) * g + b


def _sigmoid(x):
    return 1.0 / (1.0 + jnp.exp(-x))


def _params(sem, vmem=VMEM_LIMIT, flags=None):
    return pltpu.CompilerParams(dimension_semantics=sem, vmem_limit_bytes=vmem, flags=flags)


def _adaln_kernel(cond_ref, w_ref, b_ref, o_ref):
    c = cond_ref[...]
    s = (c * _sigmoid(c)).astype(BF16)
    o_ref[...] = jnp.dot(s, w_ref[...].astype(BF16), preferred_element_type=F32) + b_ref[...]


def _adaln(cond, w_ada, b_ada):
    tn = 1536
    n = 6 * D_MODEL
    out = pl.pallas_call(
        _adaln_kernel,
        out_shape=jax.ShapeDtypeStruct((DEPTH, N_COND, n), F32),
        grid=(DEPTH, n // tn),
        in_specs=[
            pl.BlockSpec((N_COND, D_MODEL), lambda l, j: (0, 0)),
            pl.BlockSpec((None, D_MODEL, tn), lambda l, j: (l, 0, j)),
            pl.BlockSpec((None, 1, tn), lambda l, j: (l, 0, j)),
        ],
        out_specs=pl.BlockSpec((None, N_COND, tn), lambda l, j: (l, 0, j)),
        compiler_params=_params(("arbitrary", "arbitrary")),
        name="adaln",
    )(cond, w_ada, b_ada.reshape(DEPTH, 1, n))
    return out.reshape(DEPTH, N_COND, 6, D_MODEL)


def _cast_once(w_ref, w_scr):
    @pl.when(pl.program_id(0) == 0)
    def _():
        w_scr[...] = w_ref[...].astype(BF16)


def _inproj_a_kernel(xp_ref, xs_ref, mod_ref, w_ref, qg_ref, kg_ref, cos_ref, sin_ref,
                     q_ref, k_ref, v_ref, kf_ref, vf_ref, u_ref, w_scr, *, npt):
    i = pl.program_id(0)
    _cast_once(w_ref, w_scr)
    mod = mod_ref[...]
    tm = u_ref.shape[0]
    sub = tm // INPROJ_SUBTILES

    ri = lax.broadcasted_iota(jnp.int32, (LANES, LANES), 0) // HEAD_DIM
    ci = lax.broadcasted_iota(jnp.int32, (LANES, LANES), 1) // HEAD_DIM
    seg = jnp.where(ri == ci, 1.0, 0.0).astype(BF16)
    lane = lax.broadcasted_iota(jnp.int32, (sub, LANES), 1)
    even = (lane % 2) == 0
    qg = qg_ref[...]
    kg = kg_ref[...]

    def norm(xc, gain):
        ss = jnp.dot((xc * xc).astype(BF16), seg, preferred_element_type=F32)
        return xc * lax.rsqrt(ss * (1.0 / HEAD_DIM) + EPS) * gain

    cache_rows = []
    for r in range(INPROJ_SUBTILES):
        rows = slice(r * sub, (r + 1) * sub)
        x = jnp.where(i < npt, xp_ref[rows, :], xs_ref[rows, :])
        h = (x * (1.0 + mod[1:2]) + mod[0:1]).astype(BF16)
        proj = jnp.dot(h, w_scr[...], preferred_element_type=F32)
        cos = cos_ref[rows, :]
        sin = sin_ref[rows, :]

        def rope(xn):
            partner = jnp.where(even, pltpu.roll(xn, LANES - 1, 1), pltpu.roll(xn, 1, 1))
            return xn * cos + partner * sin

        for c in range(ATT_W // LANES):
            qr = rope(norm(proj[:, c * LANES:(c + 1) * LANES], qg))
            qs = (qr * (HEAD_DIM ** -0.5 * LOG2E)).astype(BF16)
            q_ref[2 * c, rows, :] = qs[:, :HEAD_DIM]
            q_ref[2 * c + 1, rows, :] = qs[:, HEAD_DIM:]

        kn = norm(proj[:, ATT_W:ATT_W + KV_W], kg)
        kr = rope(kn).astype(BF16)
        k_ref[0, rows, :] = kr[:, :HEAD_DIM]
        k_ref[1, rows, :] = kr[:, HEAD_DIM:]

        v = proj[:, ATT_W + KV_W:ATT_W + 2 * KV_W]
        cache_rows.append((kn, v))
        vb = v.astype(BF16)
        v_ref[0, rows, :] = vb[:, :HEAD_DIM]
        v_ref[1, rows, :] = vb[:, HEAD_DIM:]

        off = ATT_W + 2 * KV_W
        a = proj[:, off:off + CONV_CH]
        gt = proj[:, off + CONV_CH:off + 2 * CONV_CH]
        u_ref[rows, :] = a * _sigmoid(gt)

    @pl.when(i < npt)
    def _():
        for r, (kn, v) in enumerate(cache_rows):
            j, t0 = (r * sub) // SEQ, (r * sub) % SEQ
            kf_ref[j, :, :, t0:t0 + sub] = kn.T.reshape(ATT_KV_HEADS, HEAD_DIM, sub)
            vf_ref[j, :, :, t0:t0 + sub] = v.T.reshape(ATT_KV_HEADS, HEAD_DIM, sub)


def _rope_tables(tm):
    t = jnp.arange(DEC_SEQ)
    row = (t // GRID_W).astype(F32)
    col = (t % GRID_W).astype(F32)
    freqs = ROPE_THETA ** (-jnp.arange(0, ROPE_AXIS_DIM, 2, dtype=F32) / ROPE_AXIS_DIM)
    ang = jnp.concatenate([row[:, None] * freqs, col[:, None] * freqs], axis=-1)
    pair = (jnp.arange(LANES) % HEAD_DIM) // 2
    sign = jnp.where(jnp.arange(LANES) % 2 == 0, -1.0, 1.0).astype(F32)
    cos = jnp.cos(ang)[:, pair]
    sin = jnp.sin(ang)[:, pair] * sign
    cos = jnp.concatenate([jnp.ones((tm, LANES), F32), cos], axis=0)
    sin = jnp.concatenate([jnp.zeros((tm, LANES), F32), sin], axis=0)
    return cos, sin


def _inproj_a(xp, xs, mod, w, q_gain, k_gain):
    tm = TM_PROJ
    npt = NP // tm
    tps = DEC_SEQ // tm
    nt = NT // tm
    cos, sin = _rope_tables(tm)
    qg = jnp.tile(q_gain, LANES // HEAD_DIM).reshape(1, LANES)
    kg = jnp.tile(k_gain, LANES // HEAD_DIM).reshape(1, LANES)

    def rope_idx(i):
        return (jnp.where(i < npt, 0, 1 + (i - npt) % tps), 0)

    return pl.pallas_call(
        functools.partial(_inproj_a_kernel, npt=npt),
        out_shape=[
            jax.ShapeDtypeStruct((ATT_HEADS, NT, HEAD_DIM), BF16),
            jax.ShapeDtypeStruct((ATT_KV_HEADS, NT, HEAD_DIM), BF16),
            jax.ShapeDtypeStruct((ATT_KV_HEADS, NT, HEAD_DIM), BF16),
            jax.ShapeDtypeStruct((BATCH, ATT_KV_HEADS, HEAD_DIM, SEQ), F32),
            jax.ShapeDtypeStruct((BATCH, ATT_KV_HEADS, HEAD_DIM, SEQ), F32),
            jax.ShapeDtypeStruct((NT, CONV_CH), F32),
        ],
        grid=(nt,),
        in_specs=[
            pl.BlockSpec((tm, D_MODEL), lambda i: (jnp.minimum(i, npt - 1), 0)),
            pl.BlockSpec((tm, D_MODEL), lambda i: (jnp.maximum(i - npt, 0), 0)),
            pl.BlockSpec((None, 6, D_MODEL), lambda i: (_cond_index(i, tm), 0, 0)),
            pl.BlockSpec((None, D_MODEL, EVEN_IN), lambda i: (0, 0, 0)),
            pl.BlockSpec((1, LANES), lambda i: (0, 0)),
            pl.BlockSpec((1, LANES), lambda i: (0, 0)),
            pl.BlockSpec((tm, LANES), rope_idx),
            pl.BlockSpec((tm, LANES), rope_idx),
        ],
        out_specs=[
            pl.BlockSpec((ATT_HEADS, tm, HEAD_DIM), lambda i: (0, i, 0)),
            pl.BlockSpec((ATT_KV_HEADS, tm, HEAD_DIM), lambda i: (0, i, 0)),
            pl.BlockSpec((ATT_KV_HEADS, tm, HEAD_DIM), lambda i: (0, i, 0)),
            pl.BlockSpec((tm // SEQ, ATT_KV_HEADS, HEAD_DIM, SEQ), lambda i: (jnp.minimum(i, npt - 1), 0, 0, 0)),
            pl.BlockSpec((tm // SEQ, ATT_KV_HEADS, HEAD_DIM, SEQ), lambda i: (jnp.minimum(i, npt - 1), 0, 0, 0)),
            pl.BlockSpec((tm, CONV_CH), lambda i: (i, 0)),
        ],
        scratch_shapes=[pltpu.VMEM((D_MODEL, EVEN_IN), BF16)],
        compiler_params=_params(("arbitrary",)),
        name="inproj_a",
    )(xp, xs, mod, w, qg, kg, cos, sin)


def _attn_chunk(qs, kc, vc, m, acc):
    s = lax.dot_general(qs, kc, (((1,), (1,)), ((), ())), preferred_element_type=F32)
    fold = s[:, :LANES]
    for c in range(1, s.shape[1] // LANES):
        fold = jnp.maximum(fold, s[:, c * LANES:(c + 1) * LANES])
    m_new = jnp.maximum(m, jnp.max(fold, axis=1, keepdims=True))
    alpha = jnp.exp2(m - m_new)
    p = jnp.exp2(s - m_new).astype(BF16)
    va = jnp.concatenate([vc, jnp.ones((vc.shape[0], LANES - HEAD_DIM), BF16)], axis=1)
    acc = alpha * acc + jnp.dot(p, va, preferred_element_type=F32)
    return m_new, acc


def _attn_init(nq):
    return jnp.full((nq, 1), -1e30, F32), jnp.zeros((nq, LANES), F32)


def _attn_finish(acc, o_ref):
    tq = o_ref.shape[0]
    o = acc[:, :HEAD_DIM] * (1.0 / acc[:, HEAD_DIM:HEAD_DIM + 1])
    o_ref[...] = jnp.concatenate([o[h * tq:(h + 1) * tq] for h in range(ATT_GROUP)],
                                 axis=1).astype(BF16)


def _stack_heads(q_ref):
    return q_ref[...].reshape(ATT_GROUP * q_ref.shape[1], HEAD_DIM)


def _attn_prompt_kernel(q_ref, k_ref, v_ref, o_ref):
    qs = _stack_heads(q_ref)
    _, acc = _attn_chunk(qs, k_ref[...], v_ref[...], *_attn_init(qs.shape[0]))
    _attn_finish(acc, o_ref)


def _attn_sample_kernel(q_ref, k_ref, v_ref, ck_ref, cv_ref, o_ref, k_scr, v_scr, *, nchunks):
    @pl.when(pl.program_id(2) == 0)
    def _():
        own = k_ref.shape[0]
        k_scr[0:own, :] = k_ref[...]
        v_scr[0:own, :] = v_ref[...]
        first = pl.program_id(1) == 0
        ck = ck_ref[...]
        cv = cv_ref[...]
        k_scr[own:, :] = jnp.where(first, ck[:, :HEAD_DIM], ck[:, HEAD_DIM:]).astype(BF16)
        v_scr[own:, :] = jnp.where(first, cv[:, :HEAD_DIM], cv[:, HEAD_DIM:]).astype(BF16)

    qs = _stack_heads(q_ref)

    def body(j, carry):
        rows = pl.ds(pl.multiple_of(j * TK_ATT, TK_ATT), TK_ATT)
        return _attn_chunk(qs, k_scr[rows, :], v_scr[rows, :], *carry)

    _, acc = lax.fori_loop(0, nchunks, body, _attn_init(qs.shape[0]), unroll=ATT_UNROLL)
    _attn_finish(acc, o_ref)


def _attention_prompt(q, k, v):
    return pl.pallas_call(
        _attn_prompt_kernel,
        out_shape=jax.ShapeDtypeStruct((NP, ATT_W), BF16),
        grid=(BATCH, ATT_KV_HEADS),
        in_specs=[
            pl.BlockSpec((ATT_GROUP, SEQ, HEAD_DIM), lambda b, g: (g, b, 0)),
            pl.BlockSpec((None, SEQ, HEAD_DIM), lambda b, g: (g, b, 0)),
            pl.BlockSpec((None, SEQ, HEAD_DIM), lambda b, g: (g, b, 0)),
        ],
        out_specs=pl.BlockSpec((SEQ, ATT_GROUP * HEAD_DIM), lambda b, g: (b, g)),
        compiler_params=_params(("arbitrary", "arbitrary")),
        name="attn_prompt",
    )(q, k, v)


def _attention_sample(q, k, v, ctx_k, ctx_v):
    tq = TQ_ATT
    nkeys = DEC_SEQ + PAST_LEN
    nchunks = nkeys // TK_ATT
    q_off = NP // tq
    kv_off = NP // DEC_SEQ
    return pl.pallas_call(
        functools.partial(_attn_sample_kernel, nchunks=nchunks),
        out_shape=jax.ShapeDtypeStruct((NS, ATT_W), BF16),
        grid=(DEC_BATCH, ATT_KV_HEADS, DEC_SEQ // tq),
        in_specs=[
            pl.BlockSpec((ATT_GROUP, tq, HEAD_DIM),
                         lambda b, g, i: (g, q_off + b * (DEC_SEQ // tq) + i, 0)),
            pl.BlockSpec((None, DEC_SEQ, HEAD_DIM), lambda b, g, i: (g, kv_off + b, 0)),
            pl.BlockSpec((None, DEC_SEQ, HEAD_DIM), lambda b, g, i: (g, kv_off + b, 0)),
            pl.BlockSpec((None, PAST_LEN, KV_W), lambda b, g, i: (b, 0, 0)),
            pl.BlockSpec((None, PAST_LEN, KV_W), lambda b, g, i: (b, 0, 0)),
        ],
        out_specs=pl.BlockSpec((tq, ATT_GROUP * HEAD_DIM),
                               lambda b, g, i: (b * (DEC_SEQ // tq) + i, g)),
        scratch_shapes=[pltpu.VMEM((nkeys, HEAD_DIM), BF16), pltpu.VMEM((nkeys, HEAD_DIM), BF16)],
        compiler_params=_params(("arbitrary", "arbitrary", "arbitrary")),
        name="attn_sample",
    )(q, k, v, ctx_k, ctx_v)


def _conv_window(left, cur, right, w_ref, b_ref, g_ref, bb_ref, win_ref, y_ref, out_ref):
    r = cur.shape[0]
    win_ref[0:HALO, :] = left
    win_ref[HALO:HALO + r, :] = cur
    win_ref[HALO + r:2 * HALO + r, :] = right
    base = HALO - CONV_K // 2
    sub = 8
    for c in range(CONV_CH // LANES):
        cs = slice(c * LANES, (c + 1) * LANES)
        acc = None
        for res in range(sub):
            p = None
            for k in range(CONV_K):
                if (base + k) % sub != res:
                    continue
                a = (base + k) - res
                term = win_ref[a:a + r + sub, cs] * w_ref[k:k + 1, cs]
                p = term if p is None else p + term
            if p is None:
                continue
            p = p[res:res + r]
            acc = p if acc is None else acc + p
        y_ref[:, cs] = acc + b_ref[:, cs]
    y = _layernorm(y_ref[...], g_ref[...], bb_ref[...])
    out_ref[...] = (y * _sigmoid(y)).astype(BF16)


def _tail_kernel(*refs, mode, ncast, npt, tps, off):
    nsrc = 11 if mode == "a" else 4
    src = refs[:nsrc]
    (mod_ref, wo_ref, w1_ref, w2_ref, g0_ref, b0_ref, g1_ref, b1_ref, o_ref,
     wo_scr, w1_scr, w2_scr) = refs[nsrc:nsrc + 12]
    s = pl.program_id(0)
    per = ncast // w1_scr.shape[0]
    wcol = w1_ref.shape[1]

    @pl.when(s < ncast)
    def _():
        wo_scr[s] = wo_ref[...].astype(BF16)
        w2_scr[s] = w2_ref[...].astype(BF16)

    for q in range(per):
        @pl.when(jnp.logical_and(s < ncast, s % per == q))
        def _():
            w1_scr[s // per, :, q * wcol:(q + 1) * wcol] = w1_ref[...].astype(BF16)

    def conv_half(r, tile):
        ul_ref, u_ref, ur_ref, cw_ref, cb_ref, cg_ref, cbb_ref = src[2:5] + src[7:11]
        win_scr, y_scr, uc_scr = refs[nsrc + 12:]
        ctx = tile < npt
        t = (tile - npt) % tps
        half = u_ref.shape[0] // 2
        starts = jnp.logical_or(ctx, jnp.logical_and(t == 0, r == 0))
        ends = jnp.logical_or(ctx, jnp.logical_and(t == tps - 1, r == 1))
        left = ul_ref[...] if r == 0 else u_ref[half - HALO:half, :]
        right = u_ref[half:half + HALO, :] if r == 0 else ur_ref[...]
        _conv_window(jnp.where(starts, 0.0, left), u_ref[r * half:(r + 1) * half, :],
                     jnp.where(ends, 0.0, right), cw_ref, cb_ref, cg_ref, cbb_ref,
                     win_scr.at[r], y_scr.at[r], uc_scr.at[r])
        return uc_scr[r]

    @pl.when(s >= ncast)
    def _():
        tile = s - ncast + off
        ctx = tile < npt
        mod = mod_ref[...]
        w_out = wo_scr[...].reshape(D_MODEL, D_MODEL)
        nchunk = w1_scr.shape[0]
        rows_per = w2_scr.shape[0] // nchunk
        half = o_ref.shape[0] // 2
        for r in range(2):
            rows = slice(r * half, (r + 1) * half)
            if mode == "a":
                ap_ref, as_ref, xp_ref, xs_ref = src[0], src[1], src[5], src[6]
                att = jnp.where(ctx, ap_ref[rows, :], as_ref[rows, :])
                x = jnp.where(ctx, xp_ref[rows, :], xs_ref[rows, :])
                y = jnp.dot(att, w_out[0:ATT_W, :], preferred_element_type=F32)
                y = y + jnp.dot(conv_half(r, tile), w_out[ATT_W:, :], preferred_element_type=F32)
            else:
                hp_ref, lo_ref, hi_ref, x_ref = src
                upper = ((tile - npt) % tps) >= tps // 2
                hg = jnp.where(ctx, hp_ref[rows, :], jnp.where(upper, hi_ref[rows, :], lo_ref[rows, :]))
                x = x_ref[rows, :]
                y = jnp.dot(hg, w_out, preferred_element_type=F32)
            x1 = _layernorm(ALPHA * x + mod[2:3] * y, g0_ref[...], b0_ref[...])
            h = (x1 * (1.0 + mod[4:5]) + mod[3:4]).astype(BF16)
            acc = None
            for c in range(nchunk):
                a = jnp.maximum(jnp.dot(h, w1_scr[c], preferred_element_type=F32), 0.0)
                w2 = w2_scr[c * rows_per:(c + 1) * rows_per].reshape(TF_FFN, D_MODEL)
                t = jnp.dot((a * a).astype(BF16), w2, preferred_element_type=F32)
                acc = t if acc is None else acc + t
            o_ref[rows, :] = _layernorm(ALPHA * x1 + mod[5:6] * acc, g1_ref[...], b1_ref[...])


def _block_tail(mode, src, row0, nrows, mod, w_out, w1, w2, layer, ln0, ln1):
    tm = TM_PROJ
    ncast = TAIL_CAST_STEPS
    npt = NP // tm
    tps = DEC_SEQ // tm
    hps = tps // 2
    off = row0 // tm
    ntiles = nrows // tm
    tile = lambda s: jnp.maximum(s - ncast, 0) + off
    blk = lambda s: jnp.minimum(s, ncast - 1)
    first = lambda s: (jnp.minimum(tile(s), npt - 1), 0)
    second = lambda s: (jnp.maximum(tile(s) - npt, 0), 0)

    def half_idx(s, upper):
        t0 = jnp.maximum(tile(s) - npt, 0)
        t = t0 % tps
        t = jnp.maximum(t - hps, 0) if upper else jnp.minimum(t, hps - 1)
        return ((t0 // tps) * hps + t, 0)

    scratch = []
    if mode == "a":
        assert tm == 2 * R_CONV
        att_p, att_s, u, xp, xs, conv_w, conv_b, cln_g, cln_b = src
        hb = tm // HALO
        nh = NT // HALO
        cvec = pl.BlockSpec((1, CONV_CH), lambda s: (0, 0))
        src = (att_p, att_s, u, u, u, xp, xs,
               jnp.concatenate([conv_w, jnp.zeros((1, CONV_CH), F32)], axis=0),
               conv_b.reshape(1, CONV_CH), cln_g.reshape(1, CONV_CH), cln_b.reshape(1, CONV_CH))
        src_specs = [
            pl.BlockSpec((tm, ATT_W), first),
            pl.BlockSpec((tm, ATT_W), second),
            pl.BlockSpec((HALO, CONV_CH), lambda s: (jnp.maximum(tile(s) * hb - 1, 0), 0)),
            pl.BlockSpec((tm, CONV_CH), lambda s: (tile(s), 0)),
            pl.BlockSpec((HALO, CONV_CH), lambda s: (jnp.minimum((tile(s) + 1) * hb, nh - 1), 0)),
            pl.BlockSpec((tm, D_MODEL), first),
            pl.BlockSpec((tm, D_MODEL), second),
            pl.BlockSpec((CONV_K + 1, CONV_CH), lambda s: (0, 0)),
            cvec, cvec, cvec,
        ]
        scratch = [pltpu.VMEM((2, R_CONV + 2 * HALO, CONV_CH), F32), pltpu.VMEM((2, R_CONV, CONV_CH), F32),
                   pltpu.VMEM((2, R_CONV, CONV_CH), BF16)]
    else:
        src_specs = [
            pl.BlockSpec((tm, M_INNER), first),
            pl.BlockSpec((tm, M_INNER), lambda s: half_idx(s, False)),
            pl.BlockSpec((tm, M_INNER), lambda s: half_idx(s, True)),
            pl.BlockSpec((tm, D_MODEL), lambda s: (tile(s), 0)),
        ]
    nchunk = D_FF // TF_FFN
    vec = pl.BlockSpec((1, D_MODEL), lambda s: (0, 0))
    return pl.pallas_call(
        functools.partial(_tail_kernel, mode=mode, ncast=ncast, npt=npt, tps=tps, off=off),
        out_shape=jax.ShapeDtypeStruct((nrows, D_MODEL), F32),
        grid=(ncast + ntiles,),
        in_specs=src_specs + [
            pl.BlockSpec((None, 6, D_MODEL), lambda s: (_cond_index(tile(s), tm), 0, 0)),
            pl.BlockSpec((None, D_MODEL // ncast, D_MODEL), lambda s: (0, blk(s), 0)),
            pl.BlockSpec((None, D_MODEL, D_FF // ncast), lambda s: (layer, 0, blk(s))),
            pl.BlockSpec((None, D_FF // ncast, D_MODEL), lambda s: (layer, blk(s), 0)),
            vec, vec, vec, vec,
        ],
        out_specs=pl.BlockSpec((tm, D_MODEL), lambda s: (jnp.maximum(s - ncast, 0), 0)),
        scratch_shapes=[
            pltpu.VMEM((ncast, D_MODEL // ncast, D_MODEL), BF16),
            pltpu.VMEM((nchunk, D_MODEL, TF_FFN), BF16),
            pltpu.VMEM((ncast, D_FF // ncast, D_MODEL), BF16),
        ] + scratch,
        compiler_params=_params(("arbitrary",)),
        name="tail_" + mode,
    )(*src, mod, w_out, w1, w2, *ln0, *ln1)


def _inproj_m_kernel(x_ref, mod_ref, w_ref, wg_ref, bg_ref, o_ref, gt_ref, vt_ref, wg_scr):
    nslab = vt_ref.shape[0]
    L = L_CHUNK
    mod = mod_ref[...]
    h = (x_ref[...] * (1.0 + mod[1:2]) + mod[0:1]).astype(BF16)

    @pl.when(pl.program_id(0) == 0)
    def _():
        ngate = wg_ref.shape[1]
        c = lax.broadcasted_iota(jnp.int32, (ngate, LANES), 0)
        g, hd = c // M_HEADS, c % M_HEADS
        t = jnp.where(g == 1, 2, jnp.where(g == 2, 1, g))
        lane = lax.broadcasted_iota(jnp.int32, (ngate, LANES), 1)
        place = jnp.where(lane == hd * GATE_STRIDE + t, 1.0, 0.0).astype(BF16)
        wg_scr[...] = jnp.dot(wg_ref[...].astype(BF16), place,
                              preferred_element_type=F32).astype(BF16)

    gates = jnp.dot(h, wg_scr[...], preferred_element_type=F32) + bg_ref[...]
    kind = lax.broadcasted_iota(jnp.int32, gates.shape, 1) % GATE_STRIDE
    gates = jnp.where((kind == 2) | (kind == 3), _log_sigmoid(gates), gates)
    gates_t = gates.T
    for s in range(nslab):
        gt_ref[s] = gates_t[0:GATE_ROWS, s * L:(s + 1) * L]

    for j in range(4):
        cols = slice(j * M_INNER, (j + 1) * M_INNER)
        p = jnp.dot(h, w_ref[:, cols], preferred_element_type=F32)
        if j == 1:
            p = p * (M_DK ** -0.5)
        o_ref[:, cols] = p.astype(BF16)
        if j == 2:
            v_t = p.T.astype(BF16)
            for s in range(nslab):
                vt_ref[s] = v_t[:, s * L:(s + 1) * L]


def _inproj_m(x, mod, w, wg, bg):
    tm = TM_PROJ
    nslab = tm // L_CHUNK
    return pl.pallas_call(
        _inproj_m_kernel,
        out_shape=[
            jax.ShapeDtypeStruct((NT, 4 * M_INNER), BF16),
            jax.ShapeDtypeStruct((NT // L_CHUNK, GATE_ROWS, L_CHUNK), F32),
            jax.ShapeDtypeStruct((NT // L_CHUNK, M_INNER, L_CHUNK), BF16),
        ],
        grid=(NT // tm,),
        in_specs=[
            pl.BlockSpec((tm, D_MODEL), lambda i: (i, 0)),
            pl.BlockSpec((None, 6, D_MODEL), lambda i: (_cond_index(i, tm), 0, 0)),
            pl.BlockSpec((D_MODEL, 4 * M_INNER), lambda i: (0, 0)),
            pl.BlockSpec(wg.shape, lambda i: (0, 0)),
            pl.BlockSpec((1, LANES), lambda i: (0, 0)),
        ],
        out_specs=[
            pl.BlockSpec((tm, 4 * M_INNER), lambda i: (i, 0)),
            pl.BlockSpec((nslab, GATE_ROWS, L_CHUNK), lambda i: (i, 0, 0)),
            pl.BlockSpec((nslab, M_INNER, L_CHUNK), lambda i: (i, 0, 0)),
        ],
        scratch_shapes=[pltpu.VMEM((D_MODEL, LANES), BF16)],
        compiler_params=_params(("arbitrary",)),
        name="inproj_m",
    )(x, mod, w, wg, bg)


def _log_sigmoid(x):
    return jnp.minimum(x, 0.0) - jnp.log1p(jnp.exp(-jnp.abs(x)))


def _split3(x):
    x1 = x.astype(BF16)
    r1 = x - x1.astype(F32)
    x2 = r1.astype(BF16)
    x3 = (r1 - x2.astype(F32)).astype(BF16)
    return x1, x2, x3


def _mlstm_prefix(gt):
    L = gt.shape[1]
    parts = jnp.concatenate([t.astype(F32) for t in _split3(gt)]
                            + [jnp.zeros((GATE_ROWS, L), F32)], axis=0).astype(BF16)
    upper = jnp.where(lax.broadcasted_iota(jnp.int32, (L, L), 0)
                      <= lax.broadcasted_iota(jnp.int32, (L, L), 1), 1.0, 0.0).astype(BF16)
    r = jnp.dot(parts, upper, preferred_element_type=F32)
    return r[0:GATE_ROWS] + r[GATE_ROWS:2 * GATE_ROWS] + r[2 * GATE_ROWS:3 * GATE_ROWS]


def _rows3(x, n):
    g = x.shape[0]
    return jnp.concatenate([t.astype(F32) for t in _split3(x)]
                           + [jnp.zeros((g, ONES_ROWS - 3, n), F32)], axis=1).astype(BF16)


def _sum3(x, r0):
    return x[:, r0:r0 + 1] + x[:, r0 + 1:r0 + 2] + x[:, r0 + 2:r0 + 3]


def _mlstm_group(rev, q, k, vt, gt, pre, state):
    G, L = q.shape[0], q.shape[1]
    d = 1 if rev else 0
    tpos = lax.broadcasted_iota(jnp.int32, (L, L), 0)
    spos = lax.broadcasted_iota(jnp.int32, (L, L), 1)
    keep = (tpos >= spos) if rev else (tpos <= spos)
    if rev:
        tot = pre[:, 3:4, L - 1:L]
        a_row = tot - pre[:, 3:4, :] + gt[:, 3:4, :]
    else:
        a_row = pre[:, 2:3, :]
        tot = a_row[:, :, L - 1:L]
    b_row = gt[:, d:d + 1, :] - a_row
    m = jnp.zeros((G, 1, 1), F32) if state is None else state[0]

    gp = -(-G // 8) * 8
    rows = [b_row[p] for p in range(G)] + [jnp.zeros((gp - G, L), F32)] * (1 if gp > G else 0)
    bt = jnp.concatenate(rows, axis=0).T
    bm = jnp.stack([jnp.where(keep, bt[:, p:p + 1], -jnp.inf) for p in range(G)])
    mm = jnp.maximum(m, jnp.max(bm, axis=1, keepdims=True))
    w = jnp.exp(bm - mm)
    s = jnp.einsum('gsd,gtd->gst', k, q, preferred_element_type=F32)
    qk = s * w
    num = jnp.einsum('gvs,gst->gvt', vt, qk.astype(BF16), preferred_element_type=F32)
    den = jnp.sum(qk, axis=1, keepdims=True)
    if state is not None:
        _, ct, n = state
        cn = jnp.concatenate([ct.astype(BF16), _rows3(n, n.shape[2])], axis=1)
        inter = jnp.einsum('gcd,gtd->gct', cn, q, preferred_element_type=F32)
        s_inter = jnp.exp(m - mm)
        num = num + s_inter * inter[:, :M_DV]
        den = den + s_inter * _sum3(inter, M_DV)
    h = num * (1.0 / jnp.maximum(jnp.abs(den), jnp.exp(-(a_row + mm))))

    g_row = tot + b_row
    m_new = jnp.maximum(tot + m, jnp.max(g_row, axis=2, keepdims=True))
    ws = jnp.exp(g_row - m_new)
    lhs = jnp.concatenate([(vt.astype(F32) * ws).astype(BF16), _rows3(ws, L)], axis=1)
    upd = jnp.einsum('gcs,gsd->gcd', lhs, k, preferred_element_type=F32)
    ct_new = upd[:, :M_DV]
    n_new = _sum3(upd, M_DV)
    if state is not None:
        decay = jnp.exp(tot + m - m_new)
        ct_new = decay * ct + ct_new
        n_new = decay * n + n_new
    return h, (m_new, ct_new, n_new)


def _heads(x):
    w = x.shape[1] // M_HEADS
    return jnp.stack([x[:, h * w:(h + 1) * w] for h in range(M_HEADS)])


def _mlstm_finish(ht, o, gain):
    ms = jnp.mean(ht * ht, axis=1, keepdims=True)
    hn = ht * lax.rsqrt(ms + EPS)
    hn = jnp.concatenate([hn[h].T for h in range(M_HEADS)], axis=1)
    return (_sigmoid(o.astype(F32)) * hn * gain).astype(BF16)


def _mlstm_sample_kernel(qf_ref, kf_ref, vtf_ref, gtf_ref, of_ref, qb_ref, kb_ref, vtb_ref, gtb_ref,
                         ob_ref, gain_ref, c0_ref, n0_ref, m0_ref, hi_ref, lo_ref,
                         hs_ref, ct_scr, n_scr, m_scr, *, nc):
    j = pl.program_id(1)
    jb = nc - 1 - j
    H = M_HEADS
    L = L_CHUNK

    @pl.when(j == 0)
    def _():
        for d in range(2):
            for h in range(H):
                ct_scr[d * H + h] = c0_ref[d, h].T
                n_scr[d * H + h] = n0_ref[d, h]
                m_scr[d * H + h] = jnp.broadcast_to(m0_ref[d, h], (1, LANES))

    def run(rev, q_ref, k_ref, vt_ref, gt_ref, lo):
        gt = gt_ref[...]
        pre = _mlstm_prefix(gt).reshape(H, GATE_STRIDE, L)
        state = (m_scr[lo:lo + H][:, :, 0:1], ct_scr[lo:lo + H], n_scr[lo:lo + H])
        h, (m_new, ct_new, n_new) = _mlstm_group(
            rev, _heads(q_ref[...]), _heads(k_ref[...]), vt_ref[...].reshape(H, M_DV, L),
            gt.reshape(H, GATE_STRIDE, L), pre, state)
        ct_scr[lo:lo + H] = ct_new
        n_scr[lo:lo + H] = n_new
        m_scr[lo:lo + H] = jnp.broadcast_to(m_new, (H, 1, LANES))
        return h

    h_f = run(False, qf_ref, kf_ref, vtf_ref, gtf_ref, 0)
    h_b = run(True, qb_ref, kb_ref, vtb_ref, gtb_ref, H)

    @pl.when(j < nc // 2)
    def _():
        hs_ref[j] = h_f
        hs_ref[jb] = h_b

    @pl.when(j >= nc // 2)
    def _():
        gain = gain_ref[...]
        hi_ref[...] = _mlstm_finish(hs_ref[j] + h_f, of_ref[...], gain)
        lo_ref[...] = _mlstm_finish(hs_ref[jb] + h_b, ob_ref[...], gain)


def _mlstm_sample(qkvo, vt, gates_t, gain, init):
    nc = DEC_SEQ // L_CHUNK
    half = nc // 2
    c0 = NP // L_CHUNK
    L = L_CHUNK
    fwd = lambda b, j: c0 + b * nc + j
    bwd = lambda b, j: c0 + b * nc + nc - 1 - j

    def specs(f):
        return [
            pl.BlockSpec((L, M_INNER), lambda b, j: (f(b, j), 0)),
            pl.BlockSpec((L, M_INNER), lambda b, j: (f(b, j), 1)),
            pl.BlockSpec((None, M_INNER, L), lambda b, j: (f(b, j), 0, 0)),
            pl.BlockSpec((None, GATE_ROWS, L), lambda b, j: (f(b, j), 0, 0)),
            pl.BlockSpec((L, M_INNER), lambda b, j: (f(b, j), 3)),
        ]

    state_specs = [
        pl.BlockSpec((None, 2, M_HEADS, M_DK, M_DV), lambda b, j: (b, 0, 0, 0, 0)),
        pl.BlockSpec((None, 2, M_HEADS, 1, M_DK), lambda b, j: (b, 0, 0, 0, 0)),
        pl.BlockSpec((None, 2, M_HEADS, 1, 1), lambda b, j: (b, 0, 0, 0, 0)),
    ]
    out_rows = DEC_BATCH * DEC_SEQ // 2
    return pl.pallas_call(
        functools.partial(_mlstm_sample_kernel, nc=nc),
        out_shape=[jax.ShapeDtypeStruct((out_rows, M_INNER), BF16)] * 2,
        grid=(DEC_BATCH, nc),
        in_specs=specs(fwd) + specs(bwd) + [pl.BlockSpec((1, M_INNER), lambda b, j: (0, 0))] + state_specs,
        out_specs=[
            pl.BlockSpec((L, M_INNER), lambda b, j: (b * half + jnp.maximum(j, half) - half, 0)),
            pl.BlockSpec((L, M_INNER), lambda b, j: (b * half + jnp.minimum(nc - 1 - j, half - 1), 0)),
        ],
        scratch_shapes=[
            pltpu.VMEM((nc, M_HEADS, M_DV, L), F32),
            pltpu.VMEM((2 * M_HEADS, M_DV, M_DK), F32),
            pltpu.VMEM((2 * M_HEADS, 1, M_DK), F32),
            pltpu.VMEM((2 * M_HEADS, 1, LANES), F32),
        ],
        compiler_params=_params(("arbitrary", "arbitrary")),
        name="mlstm",
    )(qkvo, qkvo, vt, gates_t, qkvo, qkvo, qkvo, vt, gates_t, qkvo, gain, *init)


def _mlstm_prompt_kernel(q_ref, k_ref, vt_ref, gt_ref, o_ref, gain_ref, out_ref, co_ref, no_ref, mo_ref,
                         *, nseq):
    H = M_HEADS
    L = L_CHUNK
    G = nseq * H
    q = jnp.concatenate([_heads(q_ref[b * L:(b + 1) * L, :]) for b in range(nseq)], axis=0)
    k = jnp.concatenate([_heads(k_ref[b * L:(b + 1) * L, :]) for b in range(nseq)], axis=0)
    vt = vt_ref[...].reshape(G, M_DV, L)
    gt3 = jnp.concatenate([gt_ref[b] for b in range(nseq)], axis=0).reshape(G, GATE_STRIDE, L)
    pre3 = jnp.concatenate([_mlstm_prefix(gt_ref[b]) for b in range(nseq)],
                           axis=0).reshape(G, GATE_STRIDE, L)
    gain = gain_ref[...]
    hs = None
    for d in range(2):
        h, (m_new, ct_new, n_new) = _mlstm_group(d == 1, q, k, vt, gt3, pre3, None)
        hs = h if hs is None else hs + h
        for b in range(nseq):
            for hh in range(H):
                co_ref[b, d, hh] = ct_new[b * H + hh].T
                no_ref[b, d, hh] = n_new[b * H + hh]
                mo_ref[b, d, hh] = m_new[b * H + hh]
    for b in range(nseq):
        out_ref[b * L:(b + 1) * L, :] = _mlstm_finish(hs[b * H:(b + 1) * H], o_ref[b * L:(b + 1) * L, :], gain)


def _mlstm_prompt(qkvo, vt, gates_t, gain):
    nseq = MLSTM_SEQS
    L = L_CHUNK
    rows = nseq * L
    state_specs = [
        pl.BlockSpec((nseq, 2, M_HEADS, M_DK, M_DV), lambda i: (i, 0, 0, 0, 0)),
        pl.BlockSpec((nseq, 2, M_HEADS, 1, M_DK), lambda i: (i, 0, 0, 0, 0)),
        pl.BlockSpec((nseq, 2, M_HEADS, 1, 1), lambda i: (i, 0, 0, 0, 0)),
    ]
    return pl.pallas_call(
        functools.partial(_mlstm_prompt_kernel, nseq=nseq),
        out_shape=[
            jax.ShapeDtypeStruct((NP, M_INNER), BF16),
            jax.ShapeDtypeStruct((BATCH, 2, M_HEADS, M_DK, M_DV), F32),
            jax.ShapeDtypeStruct((BATCH, 2, M_HEADS, 1, M_DK), F32),
            jax.ShapeDtypeStruct((BATCH, 2, M_HEADS, 1, 1), F32),
        ],
        grid=(BATCH // nseq,),
        in_specs=[
            pl.BlockSpec((rows, M_INNER), lambda i: (i, 0)),
            pl.BlockSpec((rows, M_INNER), lambda i: (i, 1)),
            pl.BlockSpec((nseq, M_INNER, L), lambda i: (i, 0, 0)),
            pl.BlockSpec((nseq, GATE_ROWS, L), lambda i: (i, 0, 0)),
            pl.BlockSpec((rows, M_INNER), lambda i: (i, 3)),
            pl.BlockSpec((1, M_INNER), lambda i: (0, 0)),
        ],
        out_specs=[pl.BlockSpec((rows, M_INNER), lambda i: (i, 0))] + state_specs,
        compiler_params=_params(("arbitrary",)),
        name="mlstm_state",
    )(qkvo, qkvo, vt, gates_t, qkvo, gain)


@jax.jit
def kernel(x_prompt, x_sample, cache_k, cache_v, state_c, state_n, state_m, c, c_ctx, w_ada, b_ada,
           ln_g, ln_b, w_ff1, w_ff2, w_in_a, q_gain, k_gain, conv_w, conv_b, conv_ln_g, conv_ln_b,
           w_out_a, w_in_m, b_gate_m, mh_gain, w_out_m):
    xp = x_prompt.reshape(NP, D_MODEL)
    xs = x_sample.reshape(NS, D_MODEL)

    cond = jnp.concatenate(
        [c_ctx[None, :], c, jnp.zeros((N_COND - 1 - DEC_BATCH, D_MODEL), F32)], axis=0)
    mods = _adaln(cond, w_ada, b_ada)

    ln = lambda l, s: (ln_g[l, s].reshape(1, D_MODEL), ln_b[l, s].reshape(1, D_MODEL))

    q, k, v, kf, vf, u = _inproj_a(xp, xs, mods[0], w_in_a, q_gain[0], k_gain[0])
    att_p = _attention_prompt(q, k, v)
    ctx_k = cache_k[:, 0].reshape(DEC_BATCH, PAST_LEN, KV_W)
    ctx_v = cache_v[:, 0].reshape(DEC_BATCH, PAST_LEN, KV_W)
    att_s = _attention_sample(q, k, v, ctx_k, ctx_v)
    tail_src = (att_p, att_s, u, xp, xs, conv_w[0], conv_b[0], conv_ln_g[0], conv_ln_b[0])
    x2 = _block_tail("a", tail_src, 0, NT, mods[0], w_out_a, w_ff1, w_ff2, 0, ln(0, 0), ln(0, 1))

    wm = w_in_m[0]
    src = jnp.array([4 * M_INNER + g * M_HEADS + h
                     for h in range(M_HEADS) for g in (0, 2, 1, 3)], jnp.int32)
    dst = jnp.array([h * GATE_STRIDE + t for h in range(M_HEADS) for t in range(4)], jnp.int32)
    wg = wm[:, 4 * M_INNER:]
    bg = jnp.zeros((1, LANES), F32).at[0, dst].set(b_gate_m[0][src - 4 * M_INNER])
    qkvo, gates_t, vt_m = _inproj_m(x2, mods[1], wm[:, :4 * M_INNER].astype(BF16), wg, bg)
    gain = mh_gain[0].reshape(1, M_INNER)
    hg_p, st_c, st_n, st_m = _mlstm_prompt(qkvo, vt_m, gates_t, gain)
    init = (state_c[:, 0], state_n[:, 0].reshape(DEC_BATCH, 2, M_HEADS, 1, M_DK),
            state_m[:, 0].reshape(DEC_BATCH, 2, M_HEADS, 1, 1))
    hg_hi, hg_lo = _mlstm_sample(qkvo, vt_m, gates_t, gain, init)
    tail_src = (hg_p, hg_lo, hg_hi, x2)
    y_p = _block_tail("m", tail_src, 0, NP, mods[1], w_out_m, w_ff1, w_ff2, 1, ln(1, 0), ln(1, 1))
    y_s = _block_tail("m", tail_src, NP, NS, mods[1], w_out_m, w_ff1, w_ff2, 1, ln(1, 0), ln(1, 1))

    new_k = jnp.transpose(kf, (0, 3, 1, 2))[:, None]
    new_v = jnp.transpose(vf, (0, 3, 1, 2))[:, None]
    return (y_p.reshape(BATCH, SEQ, D_MODEL), y_s.reshape(DEC_BATCH, DEC_SEQ, D_MODEL),
            new_k, new_v,
            st_c.reshape(BATCH, 1, 2, M_HEADS, M_DK, M_DV),
            st_n.reshape(BATCH, 1, 2, M_HEADS, M_DK),
            st_m.reshape(BATCH, 1, 2, M_HEADS))
```
